```python
import jax
import jax.numpy as jnp
from jax import lax
import numpy as np

D_MODEL = 1024
BATCH = 2
SEQ = 8192
DEPTH = 4

N_MIXERS = 3
N_MLA_LAYERS = (DEPTH + N_MIXERS - 1) // N_MIXERS
N_MLSTM_LAYERS = (DEPTH + N_MIXERS - 2) // N_MIXERS
N_RWKV_LAYERS = DEPTH // N_MIXERS

RMS_EPS = 1e-6
FFN_HIDDEN = 2816
MLA_HEADS = 8
MLA_Q_RANK = 512
MLA_KV_RANK = 256
MLA_NOPE = 128
MLA_ROPE = 64
MLA_V = 128
ROPE_THETA = 10000.0
Q_BLOCK = 128
MLSTM_HEADS = 8
MLSTM_QK = 64
MLSTM_V = 128
MLSTM_CONV = 4
MLSTM_CHUNK = 64
RWKV_HEAD = 64
RWKV_HEADS = D_MODEL // RWKV_HEAD
RWKV_DECAY_LORA = 64
RWKV_A_LORA = 64
RWKV_GATE_LORA = 128
RWKV_GN_EPS = 64e-5

kernel_name = 'hybrid_mla_mlstm_rwkv7_macaron'

F32 = jnp.float32


def rmsnorm(x, g, eps=RMS_EPS):
    xf = x.astype(F32)
    y = xf * lax.rsqrt(jnp.mean(xf * xf, axis=-1, keepdims=True) + eps)
    return (y * g.astype(F32)).astype(x.dtype)


def swiglu(h, w_gate, w_up, w_down):
    return (jax.nn.silu(h @ w_gate) * (h @ w_up)) @ w_down


def rope(t, cos, sin):
    t1, t2 = jnp.split(t, 2, axis=-1)
    out = jnp.concatenate([t1 * cos - t2 * sin, t1 * sin + t2 * cos], axis=-1)
    return out.astype(t.dtype)


def causal_block_attention(q, k, v, scale):
    B, S, H, Dk = q.shape
    Dv = v.shape[-1]
    nb = S // Q_BLOCK
    qb = q.reshape(B, nb, Q_BLOCK, H, Dk).transpose(1, 0, 3, 2, 4)
    kpos = jnp.arange(S)

    def one_block(args):
        q_blk, bi = args
        s = jnp.einsum('bhqd,bkhd->bhqk', q_blk, k).astype(F32) * scale
        qpos = bi * Q_BLOCK + jnp.arange(Q_BLOCK)
        mask = kpos[None, :] <= qpos[:, None]
        s = jnp.where(mask[None, None], s, -jnp.inf)
        p = jax.nn.softmax(s, axis=-1).astype(v.dtype)
        return jnp.einsum('bhqk,bkhd->bqhd', p, v)

    out = lax.map(one_block, (qb, jnp.arange(nb)))
    return out.transpose(1, 0, 2, 3, 4).reshape(B, S, H, Dv)


def mla(h, positions, w_a, q_norm, w_qb, kv_norm, w_kvb, w_o):
    B, S, _ = h.shape
    H = MLA_HEADS
    lat = h @ w_a
    c_q = lat[..., :MLA_Q_RANK]
    c_kv = lat[..., MLA_Q_RANK:MLA_Q_RANK + MLA_KV_RANK]
    k_pe = lat[..., MLA_Q_RANK + MLA_KV_RANK:]
    q = (rmsnorm(c_q, q_norm) @ w_qb).reshape(B, S, H, MLA_NOPE + MLA_ROPE)
    kv = (rmsnorm(c_kv, kv_norm) @ w_kvb).reshape(B, S, H, MLA_NOPE + MLA_V)
    q_nope, q_pe = q[..., :MLA_NOPE], q[..., MLA_NOPE:]
    k_nope, v = kv[..., :MLA_NOPE], kv[..., MLA_NOPE:]
    inv_freq = 1.0 / (ROPE_THETA ** (jnp.arange(0, MLA_ROPE, 2, dtype=F32) / MLA_ROPE))
    ang = positions.astype(F32)[..., None] * inv_freq
    cos, sin = jnp.cos(ang), jnp.sin(ang)
    q_pe = rope(q_pe, cos[:, :, None], sin[:, :, None])
    k_pe = rope(k_pe, cos, sin)
    q = jnp.concatenate([q_nope, q_pe], axis=-1)
    k = jnp.concatenate([k_nope, jnp.broadcast_to(k_pe[:, :, None], (B, S, H, MLA_ROPE))], axis=-1)
    o = causal_block_attention(q, k, v, (MLA_NOPE + MLA_ROPE) ** -0.5)
    return o.reshape(B, S, H * MLA_V) @ w_o


def causal_depthwise_conv(x, w, b):
    K, C = w.shape
    y = lax.conv_general_dilated(x, w[:, None, :].astype(x.dtype), window_strides=(1,),
                                 padding=((K - 1, 0),), dimension_numbers=('NWC', 'WIO', 'NWC'),
                                 feature_group_count=C)
    return y + b


def mlstm_chunkwise(q, k, v, i_pre, log_f):
    B, S, H, DK = q.shape
    DV = v.shape[-1]
    L = MLSTM_CHUNK
    nc = S // L
    to_c = lambda t: t.astype(F32).reshape(B, nc, L, H, -1).transpose(1, 0, 3, 2, 4)
    to_cg = lambda t: t.astype(F32).reshape(B, nc, L, H).transpose(1, 0, 3, 2)
    causal = jnp.tril(jnp.ones((L, L), dtype=bool))

    def step(carry, inp):
        C, n, m = carry
        q_, k_, v_, i_, lf = inp
        Fc = jnp.cumsum(lf, axis=-1)
        D = Fc[..., :, None] - Fc[..., None, :] + i_[..., None, :]
        D = jnp.where(causal, D, -jnp.inf)
        inter = Fc + m[..., None]
        m_t = jnp.maximum(inter, jnp.max(D, axis=-1))
        Dw = jnp.exp(D - m_t[..., None])
        inter_w = jnp.exp(inter - m_t)
        s = jnp.einsum('bhtd,bhsd->bhts', q_, k_) * Dw
        num = jnp.einsum('bhts,bhsv->bhtv', s, v_) + inter_w[..., None] * jnp.einsum('bhtd,bhdv->bhtv', q_, C)
        den = jnp.sum(s, axis=-1) + inter_w * jnp.einsum('bhtd,bhd->bht', q_, n)
        h = num / jnp.maximum(jnp.abs(den), jnp.exp(-m_t))[..., None]
        m_new = m_t[..., -1]
        carry_decay = jnp.exp(Fc[..., -1] + m - m_new)
        w_s = jnp.exp(Fc[..., -1:] - Fc + i_ - m_new[..., None])
        C_new = carry_decay[..., None, None] * C + jnp.einsum('bhs,bhsd,bhsv->bhdv', w_s, k_, v_)
        n_new = carry_decay[..., None] * n + jnp.einsum('bhs,bhsd->bhd', w_s, k_)
        return (C_new, n_new, m_new), h

    init = (jnp.zeros((B, H, DK, DV), F32), jnp.zeros((B, H, DK), F32), jnp.zeros((B, H), F32))
    _, hs = lax.scan(step, init, (to_c(q), to_c(k), to_c(v), to_cg(i_pre), to_cg(log_f)))
    return hs.transpose(1, 0, 3, 2, 4).reshape(B, S, H, DV)


def mlstm(h, w_in, b_if, conv_w, conv_b, out_norm, w_o):
    B, S, _ = h.shape
    H, DK, DV = MLSTM_HEADS, MLSTM_QK, MLSTM_V
    QK = 2 * H * DK
    V = H * DV
    p = h @ w_in
    qk = jax.nn.silu(causal_depthwise_conv(p[..., :QK], conv_w, conv_b))
    v = p[..., QK:QK + V].reshape(B, S, H, DV)
    o = p[..., QK + V:QK + 2 * V]
    gates = p[..., QK + 2 * V:].astype(F32) + b_if.astype(F32)
    q = qk[..., :H * DK].reshape(B, S, H, DK)
    k = qk[..., H * DK:].reshape(B, S, H, DK) * (DK ** -0.5)
    i_pre = gates[..., :H]
    log_f = jax.nn.log_sigmoid(gates[..., H:])
    hid = mlstm_chunkwise(q, k, v, i_pre, log_f)
    hid = hid * lax.rsqrt(jnp.mean(hid * hid, axis=-1, keepdims=True) + RMS_EPS)
    hid = hid.reshape(B, S, V) * out_norm.astype(F32) * jax.nn.sigmoid(o.astype(F32))
    return hid.astype(h.dtype) @ w_o


def rwkv7_recurrence(r, decay, k, v, a, b):
    B, S, H, N = r.shape

    def step(state, inp):
        r_, w_, k_, v_, a_, b_ = inp
        sa = jnp.einsum('bhij,bhj->bhi', state, a_)
        state = (state * w_[:, :, None, :] + sa[..., :, None] * b_[:, :, None, :]
                 + v_[..., :, None] * k_[:, :, None, :])
        return state, jnp.einsum('bhij,bhj->bhi', state, r_)

    tm = lambda t: jnp.moveaxis(t, 1, 0)
    _, ys = lax.scan(step, jnp.zeros((B, H, N, N), F32), (tm(r), tm(decay), tm(k), tm(v), tm(a), tm(b)))
    return jnp.moveaxis(ys, 0, 1)


def rwkv7(h, mu, w_r, w_k, w_v, w0, w1, w2, a0, a1, a2, g1, g2, k_k, k_a, r_k, ln_w, ln_b, w_o):
    B, S, D = h.shape
    H, N = RWKV_HEADS, RWKV_HEAD
    xx = jnp.pad(h, ((0, 0), (1, 0), (0, 0)))[:, :S] - h
    xr, xw, xk, xv, xa, xg = [h + xx * mu[j] for j in range(6)]
    r = (xr @ w_r).astype(F32)
    k = (xk @ w_k).astype(F32)
    v = (xv @ w_v).astype(F32)
    w_log = -jax.nn.softplus(-(w0 + jnp.tanh(xw @ w1) @ w2).astype(F32)) - 0.5
    decay = jnp.exp(-jnp.exp(w_log))
    a = jax.nn.sigmoid((a0 + (xa @ a1) @ a2).astype(F32))
    g = (jax.nn.sigmoid(xg @ g1) @ g2).astype(F32)
    kk = (k * k_k.astype(F32)).reshape(B, S, H, N)
    kk = kk / jnp.maximum(jnp.sqrt(jnp.sum(kk * kk, axis=-1, keepdims=True)), 1e-12)
    k = k * (1.0 + (a - 1.0) * k_a.astype(F32))
    rh, kh, vh = r.reshape(B, S, H, N), k.reshape(B, S, H, N), v.reshape(B, S, H, N)
    ah = a.reshape(B, S, H, N)
    y = rwkv7_recurrence(rh, decay.reshape(B, S, H, N), kh, vh, -kk, kk * ah)
    mean = jnp.mean(y, axis=-1, keepdims=True)
    var = jnp.mean(jnp.square(y - mean), axis=-1, keepdims=True)
    y = ((y - mean) * lax.rsqrt(var + RWKV_GN_EPS)).reshape(B, S, D) * ln_w.astype(F32) + ln_b.astype(F32)
    bonus = jnp.sum(rh * kh * r_k.astype(F32), axis=-1, keepdims=True) * vh
    y = y + bonus.reshape(B, S, D)
    return (y * g).astype(h.dtype) @ w_o


def setup_inputs(seed: int = 0) -> dict:
    key = jax.random.key(seed)
    keys = list(jax.random.split(key, 48))
    nk = lambda: keys.pop()
    D, F = D_MODEL, FFN_HIDDEN
    nA, nB, nC = N_MLA_LAYERS, N_MLSTM_LAYERS, N_RWKV_LAYERS

    def dense(shape, fan_in, scale=1.0):
        return jax.random.normal(nk(), shape, F32) * (scale * fan_in ** -0.5)

    def gain(shape):
        return 1.0 + 0.02 * jax.random.normal(nk(), shape, F32)

    def small(shape, s=0.02, c=0.0):
        return c + s * jax.random.normal(nk(), shape, F32)

    x = jax.random.normal(nk(), (BATCH, SEQ, D), F32)
    positions = (jnp.arange(SEQ, dtype=jnp.int32)[None, :]
                 + jax.random.randint(nk(), (BATCH, 1), 0, 4096, dtype=jnp.int32))
    Hm, DK, DV = MLSTM_HEADS, MLSTM_QK, MLSTM_V
    b_if = jnp.concatenate([small((nB, Hm), 0.1),
                            jnp.linspace(3.0, 6.0, Hm, dtype=F32)[None] + small((nB, Hm), 0.1)], axis=-1)
    return {
        'x': x,
        'positions': positions,
        'ffn_norm': gain((DEPTH, 2, D)),
        'ffn_w_gate': dense((DEPTH, 2, D, F), D),
        'ffn_w_up': dense((DEPTH, 2, D, F), D),
        'ffn_w_down': dense((DEPTH, 2, F, D), F),
        'mix_norm': gain((DEPTH, D)),
        'final_norm': gain((D,)),
        'mla_w_a': dense((nA, D, MLA_Q_RANK + MLA_KV_RANK + MLA_ROPE), D),
        'mla_q_norm': gain((nA, MLA_Q_RANK)),
        'mla_w_qb': dense((nA, MLA_Q_RANK, MLA_HEADS * (MLA_NOPE + MLA_ROPE)), MLA_Q_RANK),
        'mla_kv_norm': gain((nA, MLA_KV_RANK)),
        'mla_w_kvb': dense((nA, MLA_KV_RANK, MLA_HEADS * (MLA_NOPE + MLA_V)), MLA_KV_RANK),
        'mla_w_o': dense((nA, MLA_HEADS * MLA_V, D), MLA_HEADS * MLA_V),
        'ml_w_in': dense((nB, D, 2 * Hm * DK + 2 * Hm * DV + 2 * Hm), D),
        'ml_b_if': b_if,
        'ml_conv_w': dense((nB, MLSTM_CONV, 2 * Hm * DK), MLSTM_CONV),
        'ml_conv_b': small((nB, 2 * Hm * DK)),
        'ml_out_norm': gain((nB, Hm * DV)),
        'ml_w_o': dense((nB, Hm * DV, D), Hm * DV),
        'rw_mu': jax.random.uniform(nk(), (nC, 6, D), F32),
        'rw_w_r': dense((nC, D, D), D),
        'rw_w_k': dense((nC, D, D), D),
        'rw_w_v': dense((nC, D, D), D),
        'rw_w0': jnp.linspace(-6.0, 1.0, D, dtype=F32)[None] + small((nC, D), 0.1),
        'rw_w1': dense((nC, D, RWKV_DECAY_LORA), D),
        'rw_w2': dense((nC, RWKV_DECAY_LORA, D), RWKV_DECAY_LORA, 0.1),
        'rw_a0': small((nC, D), 0.1),
        'rw_a1': dense((nC, D, RWKV_A_LORA), D),
        'rw_a2': dense((nC, RWKV_A_LORA, D), RWKV_A_LORA, 0.1),
        'rw_g1': dense((nC, D, RWKV_GATE_LORA), D),
        'rw_g2': dense((nC, RWKV_GATE_LORA, D), RWKV_GATE_LORA),
        'rw_k_k': small((nC, D), 0.05, 0.85),
        'rw_k_a': small((nC, D), 0.05, 1.0),
        'rw_r_k': small((nC, RWKV_HEADS, RWKV_HEAD), 0.1, -0.04),
        'rw_ln_w': gain((nC, D)),
        'rw_ln_b': small((nC, D)),
        'rw_w_o': dense((nC, D, D), D),
    }


def reference(x, positions, ffn_norm, ffn_w_gate, ffn_w_up, ffn_w_down, mix_norm, final_norm,
              mla_w_a, mla_q_norm, mla_w_qb, mla_kv_norm, mla_w_kvb, mla_w_o,
              ml_w_in, ml_b_if, ml_conv_w, ml_conv_b, ml_out_norm, ml_w_o,
              rw_mu, rw_w_r, rw_w_k, rw_w_v, rw_w0, rw_w1, rw_w2, rw_a0, rw_a1, rw_a2,
              rw_g1, rw_g2, rw_k_k, rw_k_a, rw_r_k, rw_ln_w, rw_ln_b, rw_w_o):
    for layer in range(DEPTH):
        x = x + 0.5 * swiglu(rmsnorm(x, ffn_norm[layer, 0]), ffn_w_gate[layer, 0],
                             ffn_w_up[layer, 0], ffn_w_down[layer, 0])
        h = rmsnorm(x, mix_norm[layer])
        kind = layer % N_MIXERS
        j = layer // N_MIXERS
        if kind == 0:
            y = mla(h, positions, mla_w_a[j], mla_q_norm[j], mla_w_qb[j], mla_kv_norm[j],
                    mla_w_kvb[j], mla_w_o[j])
        elif kind == 1:
            y = mlstm(h, ml_w_in[j], ml_b_if[j], ml_conv_w[j], ml_conv_b[j], ml_out_norm[j], ml_w_o[j])
        else:
            y = rwkv7(h, rw_mu[j], rw_w_r[j], rw_w_k[j], rw_w_v[j], rw_w0[j], rw_w1[j], rw_w2[j],
                      rw_a0[j], rw_a1[j], rw_a2[j], rw_g1[j], rw_g2[j], rw_k_k[j], rw_k_a[j],
                      rw_r_k[j], rw_ln_w[j], rw_ln_b[j], rw_w_o[j])
        x = x + y.astype(x.dtype)
        x = x + 0.5 * swiglu(rmsnorm(x, ffn_norm[layer, 1]), ffn_w_gate[layer, 1],
                             ffn_w_up[layer, 1], ffn_w_down[layer, 1])
    return rmsnorm(x, final_norm)
```

```python
import functools
import math

import jax
import jax.numpy as jnp
from jax import lax
from jax.experimental import pallas as pl
from jax.experimental.pallas import tpu as pltpu

F32 = jnp.float32
BF16 = jnp.bfloat16

RMS_EPS = 1e-6
GN_EPS = 64e-5
LANES = 128
MXU_DIM = 256
VMEM_LIMIT = 56 * 1024 * 1024

MLA_HEADS = 8
MLA_Q_RANK = 512
MLA_KV_RANK = 256
MLA_NOPE = 128
MLA_ROPE = 64
MLA_V = 128
ROPE_THETA = 10000.0
ML_HEADS = 8
ML_QK = 64
ML_V = 128
ML_CONV = 4
RW_HEAD = 64

TM_FFN = 512
TM_PROJ = 256
TQ_ATTN = 512
ML_CHUNK = 128
RW_CHUNK = 64
RW_GROUP = MXU_DIM // RW_HEAD


def _params(n_axes):
    return pltpu.CompilerParams(dimension_semantics=("arbitrary",) * n_axes,
                                vmem_limit_bytes=VMEM_LIMIT)


def _resident(shape):
    nd = len(shape)
    return pl.BlockSpec(shape, lambda *_: (0,) * nd, pipeline_mode=pl.Buffered(1))


def _rms(x, g, eps=RMS_EPS):
    return x * lax.rsqrt(jnp.mean(x * x, axis=-1, keepdims=True) + eps) * g


def _dot(a, b):
    return jnp.dot(a.astype(BF16), b.astype(BF16), preferred_element_type=F32)


def _dot_nt(a, b):
    return lax.dot_general(a.astype(BF16), b.astype(BF16), (((1,), (1,)), ((), ())),
                           preferred_element_type=F32)


def _dot_tn(a, b):
    return lax.dot_general(a.astype(BF16), b.astype(BF16), (((0,), (0,)), ((), ())),
                           preferred_element_type=F32)


def _split3(x):
    hi = x.astype(BF16)
    r1 = x - hi.astype(F32)
    mid = r1.astype(BF16)
    lo = (r1 - mid.astype(F32)).astype(BF16)
    return hi, mid, lo


def _softplus(y):
    return jnp.maximum(y, 0.0) + jnp.log1p(jnp.exp(-jnp.abs(y)))


def _sigmoid(y):
    return 1.0 / (1.0 + jnp.exp(-y))


def _ffn_kernel(x_ref, g_ref, wg_ref, wu_ref, wd_ref, o_ref):
    x = x_ref[...]
    h = _rms(x, g_ref[...]).astype(BF16)
    gate = jnp.dot(h, wg_ref[...], preferred_element_type=F32)
    up = jnp.dot(h, wu_ref[...], preferred_element_type=F32)
    act = (gate * _sigmoid(gate) * up).astype(BF16)
    y = jnp.dot(act, wd_ref[...], preferred_element_type=F32)
    o_ref[...] = x + 0.5 * y


def _ffn(x, g, wg, wu, wd):
    T, D = x.shape
    F = wg.shape[1]
    tm = min(TM_FFN, T)
    return pl.pallas_call(
        _ffn_kernel,
        grid=(T // tm,),
        in_specs=[pl.BlockSpec((tm, D), lambda i: (i, 0)),
                  _resident((1, D)), _resident((D, F)), _resident((D, F)), _resident((F, D))],
        out_specs=pl.BlockSpec((tm, D), lambda i: (i, 0)),
        out_shape=jax.ShapeDtypeStruct((T, D), F32),
        compiler_params=_params(1),
        name="ffn",
    )(x, g.reshape(1, D), wg, wu, wd)


def _norm_kernel(x_ref, g_ref, o_ref):
    o_ref[...] = _rms(x_ref[...], g_ref[...])


def _final_norm(x, g):
    T, D = x.shape
    tm = min(TM_FFN, T)
    return pl.pallas_call(
        _norm_kernel,
        grid=(T // tm,),
        in_specs=[pl.BlockSpec((tm, D), lambda i: (i, 0)), _resident((1, D))],
        out_specs=pl.BlockSpec((tm, D), lambda i: (i, 0)),
        out_shape=jax.ShapeDtypeStruct((T, D), F32),
        compiler_params=_params(1),
        name="final_norm",
    )(x, g.reshape(1, D))


def _out_proj_kernel(a_ref, w_ref, res_ref, o_ref):
    o_ref[...] = res_ref[...] + jnp.dot(a_ref[...], w_ref[...], preferred_element_type=F32)


def _out_proj(a, w, res):
    T, K = a.shape
    N = w.shape[1]
    tm = min(TM_FFN, T)
    return pl.pallas_call(
        _out_proj_kernel,
        grid=(T // tm,),
        in_specs=[pl.BlockSpec((tm, K), lambda i: (i, 0)), _resident((K, N)),
                  pl.BlockSpec((tm, N), lambda i: (i, 0))],
        out_specs=pl.BlockSpec((tm, N), lambda i: (i, 0)),
        out_shape=jax.ShapeDtypeStruct((T, N), F32),
        compiler_params=_params(1),
        name="out_proj",
    )(a, w, res)


def _mla_proj_kernel(x_ref, g_ref, pos_ref, invf_ref, sgn_ref, wa_ref, qn_ref, wqb_ref,
                     kvn_ref, wkvb_ref, q_ref, k_ref, v_ref, *, q_scale):
    H = MLA_HEADS
    h = _rms(x_ref[...], g_ref[...]).astype(BF16)
    lat = jnp.dot(h, wa_ref[...], preferred_element_type=F32)
    cq = _rms(lat[:, :MLA_Q_RANK], qn_ref[...]).astype(BF16)
    ckv = _rms(lat[:, MLA_Q_RANK:MLA_Q_RANK + MLA_KV_RANK], kvn_ref[...]).astype(BF16)
    q = jnp.dot(cq, wqb_ref[...], preferred_element_type=F32) * q_scale
    kv = jnp.dot(ckv, wkvb_ref[...], preferred_element_type=F32)
    ang = pos_ref[...].astype(F32) * invf_ref[...]
    cos = jnp.cos(ang)
    sin = jnp.sin(ang) * sgn_ref[...]
    kpe0 = MLA_Q_RANK + MLA_KV_RANK
    kx = lat[:, kpe0:kpe0 + LANES] * cos + lat[:, kpe0 + LANES:kpe0 + 2 * LANES] * sin
    half = lax.broadcasted_iota(jnp.int32, kx.shape, 1) // MLA_ROPE
    pe0 = H * MLA_NOPE
    sw0 = pe0 + H * MLA_ROPE
    for hd in range(H):
        p, m = divmod(hd, LANES // MLA_ROPE)
        qpe = (q[:, pe0 + p * LANES:pe0 + (p + 1) * LANES] * cos
               + q[:, sw0 + p * LANES:sw0 + (p + 1) * LANES] * sin)
        sel = half == m
        q_ref[0, hd, :, 0:LANES] = q[:, hd * MLA_NOPE:(hd + 1) * MLA_NOPE].astype(BF16)
        q_ref[0, hd, :, LANES:2 * LANES] = jnp.where(sel, qpe, 0.0).astype(BF16)
        k0 = hd * (MLA_NOPE + MLA_V)
        k_ref[0, hd, :, 0:LANES] = kv[:, k0:k0 + MLA_NOPE].astype(BF16)
        k_ref[0, hd, :, LANES:2 * LANES] = jnp.where(sel, kx, 0.0).astype(BF16)
        v_ref[0, hd] = kv[:, k0 + MLA_NOPE:k0 + MLA_NOPE + MLA_V].astype(BF16)


def _attn_kernel(q_ref, k_ref, v_ref, o_ref, *, tq):
    i = pl.program_id(2)
    q = q_ref[0, 0]

    def step(j, carry, diagonal):
        m, l, acc = carry
        start = pl.multiple_of(j * tq, tq)
        kt = k_ref[0, 0, pl.ds(start, tq), :]
        vt = v_ref[0, 0, pl.ds(start, tq), :]
        s = lax.dot_general(q, kt, (((1,), (1,)), ((), ())), preferred_element_type=F32)
        if diagonal:
            row = lax.broadcasted_iota(jnp.int32, s.shape, 0)
            col = lax.broadcasted_iota(jnp.int32, s.shape, 1)
            s = jnp.where(col <= row, s, -jnp.inf)
        m_new = jnp.maximum(m, jnp.max(s, axis=-1, keepdims=True))
        alpha = jnp.exp2(m - m_new)
        p = jnp.exp2(s - m_new)
        l = alpha * l + jnp.sum(p, axis=-1, keepdims=True)
        acc = alpha * acc + jnp.dot(p.astype(BF16), vt, preferred_element_type=F32)
        return m_new, l, acc

    init = (jnp.full((tq, 1), -jnp.inf, F32), jnp.zeros((tq, 1), F32),
            jnp.zeros((tq, v_ref.shape[-1]), F32))
    carry = lax.fori_loop(0, i, lambda j, c: step(j, c, False), init)
    _, l, acc = step(i, carry, True)
    o_ref[...] = (acc / l).astype(o_ref.dtype)


def _mla(x, positions, mix_g, w_a, q_norm, w_qb, kv_norm, w_kvb, w_o, B, S):
    T, D = x.shape
    H = MLA_HEADS
    kpe = w_a[:, MLA_Q_RANK + MLA_KV_RANK:]
    kpe_sw = jnp.concatenate([kpe[:, MLA_ROPE // 2:], kpe[:, :MLA_ROPE // 2]], axis=1)
    reps = LANES // MLA_ROPE
    wa2 = jnp.concatenate([w_a[:, :MLA_Q_RANK + MLA_KV_RANK], jnp.tile(kpe, (1, reps)),
                           jnp.tile(kpe_sw, (1, reps))], axis=1).astype(BF16)
    wq3 = w_qb.reshape(MLA_Q_RANK, H, MLA_NOPE + MLA_ROPE)
    q_nope = wq3[:, :, :MLA_NOPE].reshape(MLA_Q_RANK, H * MLA_NOPE)
    q_pe = wq3[:, :, MLA_NOPE:]
    q_pe_sw = jnp.concatenate([q_pe[:, :, MLA_ROPE // 2:], q_pe[:, :, :MLA_ROPE // 2]], axis=2)
    wqb2 = jnp.concatenate([q_nope, q_pe.reshape(MLA_Q_RANK, H * MLA_ROPE),
                            q_pe_sw.reshape(MLA_Q_RANK, H * MLA_ROPE)], axis=1).astype(BF16)
    inv_freq = 1.0 / (ROPE_THETA ** (jnp.arange(0, MLA_ROPE, 2, dtype=F32) / MLA_ROPE))
    invf = jnp.tile(inv_freq, LANES // (MLA_ROPE // 2)).reshape(1, LANES)
    first_half = (jnp.arange(LANES) % MLA_ROPE) < MLA_ROPE // 2
    sgn = jnp.where(first_half, -1.0, 1.0).astype(F32).reshape(1, LANES)
    q_scale = (MLA_NOPE + MLA_ROPE) ** -0.5 * math.log2(math.e)

    tm = min(TM_PROJ, S)
    nt = S // tm
    wa_cols = wa2.shape[1]
    wq_cols = wqb2.shape[1]
    kv_cols = w_kvb.shape[1]
    qk_dim = 2 * LANES
    q, k, v = pl.pallas_call(
        functools.partial(_mla_proj_kernel, q_scale=q_scale),
        grid=(B, nt),
        in_specs=[pl.BlockSpec((tm, D), lambda b, i: (b * nt + i, 0)),
                  _resident((1, D)),
                  pl.BlockSpec((tm, 1), lambda b, i: (b * nt + i, 0)),
                  _resident((1, LANES)), _resident((1, LANES)),
                  _resident((D, wa_cols)), _resident((1, MLA_Q_RANK)),
                  _resident((MLA_Q_RANK, wq_cols)), _resident((1, MLA_KV_RANK)),
                  _resident((MLA_KV_RANK, kv_cols))],
        out_specs=[pl.BlockSpec((1, H, tm, qk_dim), lambda b, i: (b, 0, i, 0)),
                   pl.BlockSpec((1, H, tm, qk_dim), lambda b, i: (b, 0, i, 0)),
                   pl.BlockSpec((1, H, tm, MLA_V), lambda b, i: (b, 0, i, 0))],
        out_shape=[jax.ShapeDtypeStruct((B, H, S, qk_dim), BF16),
                   jax.ShapeDtypeStruct((B, H, S, qk_dim), BF16),
                   jax.ShapeDtypeStruct((B, H, S, MLA_V), BF16)],
        compiler_params=_params(2),
        name="mla_proj",
    )(x, mix_g.reshape(1, D), positions.reshape(T, 1), invf, sgn, wa2, q_norm.reshape(1, -1),
      wqb2, kv_norm.reshape(1, -1), w_kvb.astype(BF16))

    tq = min(TQ_ATTN, S)
    nq = S // tq
    o = pl.pallas_call(
        functools.partial(_attn_kernel, tq=tq),
        grid=(B, H, nq),
        in_specs=[pl.BlockSpec((1, 1, tq, qk_dim), lambda b, h, i: (b, h, i, 0)),
                  pl.BlockSpec((1, 1, S, qk_dim), lambda b, h, i: (b, h, 0, 0)),
                  pl.BlockSpec((1, 1, S, MLA_V), lambda b, h, i: (b, h, 0, 0))],
        out_specs=pl.BlockSpec((tq, MLA_V), lambda b, h, i: (b * nq + i, h)),
        out_shape=jax.ShapeDtypeStruct((T, H * MLA_V), BF16),
        compiler_params=_params(3),
        name="mla_attn",
    )(q, k, v)
    return _out_proj(o, w_o.astype(BF16), x)


def _ml_proj_kernel(x_ref, g_ref, w_ref, wg_ref, bif_ref, cw_ref, cb_ref,
                    q_ref, k_ref, v_ref, o_ref, gates_ref, pbuf, *, tiles_per_seq):
    tm = x_ref.shape[0]
    QK = ML_HEADS * ML_QK
    V = ML_HEADS * ML_V
    h = _rms(x_ref[...], g_ref[...]).astype(BF16)
    p = jnp.dot(h, w_ref[...], preferred_element_type=F32)
    gates_ref[...] = jnp.dot(h, wg_ref[...], preferred_element_type=F32) + bif_ref[...]
    v_ref[...] = p[:, 2 * QK:2 * QK + V].astype(BF16)
    o_ref[...] = p[:, 2 * QK + V:2 * QK + 2 * V]

    @pl.when(pl.program_id(0) % tiles_per_seq == 0)
    def _():
        pbuf[0:8, :] = jnp.zeros((8, 2 * QK), F32)

    pbuf[8:8 + tm, :] = p[:, :2 * QK]
    acc = cb_ref[...] + cw_ref[ML_CONV - 1:ML_CONV, :] * p[:, :2 * QK]
    for kk in range(ML_CONV - 1):
        shift = ML_CONV - 1 - kk
        acc = acc + cw_ref[kk:kk + 1, :] * pbuf[8 - shift:8 - shift + tm, :]
    pbuf[0:8, :] = pbuf[tm:tm + 8, :]
    qk = acc * _sigmoid(acc)
    q_ref[...] = qk[:, :QK].astype(BF16)
    k_ref[...] = (qk[:, QK:] * ML_QK ** -0.5).astype(BF16)


def _ml_chunk_kernel(q_ref, k_ref, v_ref, o_ref, g_ref, on_ref, out_ref, c_scr, m_scr):
    L = q_ref.shape[0]
    H = ML_HEADS
    pair = LANES // ML_QK

    @pl.when(pl.program_id(1) == 0)
    def _():
        c_scr[...] = jnp.zeros(c_scr.shape, F32)
        m_scr[...] = jnp.zeros(m_scr.shape, F32)

    G = g_ref[...]
    GT = G.T
    lfG = -_softplus(-G)
    lfGT = -_softplus(-GT)
    row = lax.broadcasted_iota(jnp.int32, (L, L), 0)
    col = lax.broadcasted_iota(jnp.int32, (L, L), 1)
    tril = col <= row
    tril_b = tril.astype(BF16)
    triu_b = (row <= col).astype(BF16)
    fc_cols = sum(jnp.dot(tril_b, piece, preferred_element_type=F32) for piece in _split3(lfG))
    fc_rows = sum(jnp.dot(piece, triu_b, preferred_element_type=F32) for piece in _split3(lfGT))
    m_row = m_scr[...]
    lane = lax.broadcasted_iota(jnp.int32, (L, LANES), 1)
    ones_blk = (lane == 0).astype(BF16)
    lane_half = lane // ML_QK
    srow_half = lax.broadcasted_iota(jnp.int32, (pair * ML_QK, 2 * ML_V), 0) // ML_QK
    lane1 = lax.broadcasted_iota(jnp.int32, (1, LANES), 1)
    m_out = m_row
    for hd in range(H):
        p, mm = divmod(hd, pair)
        fc_col = fc_cols[:, H + hd:H + hd + 1]
        fc_row = fc_rows[H + hd:H + hd + 1, :]
        i_col = G[:, hd:hd + 1]
        i_row = GT[hd:hd + 1, :]
        m_prev = m_row[:, hd:hd + 1]
        dmat = jnp.where(tril, fc_col - fc_row + i_row, -jnp.inf)
        inter = fc_col + m_prev
        m_t = jnp.maximum(inter, jnp.max(dmat, axis=-1, keepdims=True))
        dw = jnp.exp(dmat - m_t)
        inter_w = jnp.exp(inter - m_t)
        qp = q_ref[:, p * LANES:(p + 1) * LANES]
        kp = k_ref[:, p * LANES:(p + 1) * LANES]
        sel = lane_half == mm
        qm = jnp.where(sel, qp, jnp.zeros_like(qp))
        km = jnp.where(sel, kp, jnp.zeros_like(kp))
        s = _dot_nt(qm, kp) * dw
        vh = jnp.concatenate([v_ref[:, hd * ML_V:(hd + 1) * ML_V], ones_blk], axis=1)
        cp = c_scr[p]
        nd = _dot(s, vh) + inter_w * _dot(qm, cp)
        num = nd[:, :ML_V]
        den = nd[:, ML_V:ML_V + 1]
        hid = num / jnp.maximum(jnp.abs(den), jnp.exp(-m_t))
        hid = hid * lax.rsqrt(jnp.mean(hid * hid, axis=-1, keepdims=True) + RMS_EPS)
        og = o_ref[:, hd * ML_V:(hd + 1) * ML_V]
        out_ref[:, hd * ML_V:(hd + 1) * ML_V] = (
            hid * on_ref[:, hd * ML_V:(hd + 1) * ML_V] * _sigmoid(og)).astype(out_ref.dtype)
        m_new = m_t[L - 1:L, :]
        fc_last = fc_col[L - 1:L, :]
        decay = jnp.exp(fc_last + m_prev - m_new)
        w_col = jnp.exp(fc_last - fc_col + i_col - m_new)
        upd = _dot_tn(km, w_col * vh.astype(F32))
        c_scr[p] = jnp.where(srow_half == mm, decay * cp, cp) + upd
        m_out = jnp.where(lane1 == hd, m_new, m_out)
    m_scr[...] = m_out


def _mlstm(x, mix_g, w_in, b_if, conv_w, conv_b, out_norm, w_o, B, S):
    T, D = x.shape
    H = ML_HEADS
    QK = H * ML_QK
    V = H * ML_V
    main = 2 * QK + 2 * V
    w_main = w_in[:, :main].astype(BF16)
    w_gate = jnp.pad(w_in[:, main:], ((0, 0), (0, LANES - 2 * H))).astype(BF16)
    bif = jnp.pad(b_if, (0, LANES - 2 * H)).reshape(1, LANES)
    tm = min(TM_PROJ, S)
    nt = S // tm
    row_spec = lambda n: pl.BlockSpec((tm, n), lambda i: (i, 0))
    q, k, v, o, gates = pl.pallas_call(
        functools.partial(_ml_proj_kernel, tiles_per_seq=nt),
        grid=(T // tm,),
        in_specs=[row_spec(D), _resident((1, D)), _resident((D, main)), _resident((D, LANES)),
                  _resident((1, LANES)), _resident((ML_CONV, 2 * QK)), _resident((1, 2 * QK))],
        out_specs=[row_spec(QK), row_spec(QK), row_spec(V), row_spec(V), row_spec(LANES)],
        out_shape=[jax.ShapeDtypeStruct((T, QK), BF16), jax.ShapeDtypeStruct((T, QK), BF16),
                   jax.ShapeDtypeStruct((T, V), BF16), jax.ShapeDtypeStruct((T, V), F32),
                   jax.ShapeDtypeStruct((T, LANES), F32)],
        scratch_shapes=[pltpu.VMEM((tm + 8, 2 * QK), F32)],
        compiler_params=_params(1),
        name="mlstm_proj",
    )(x, mix_g.reshape(1, D), w_main, w_gate, bif, conv_w, conv_b.reshape(1, -1))

    L = min(ML_CHUNK, S)
    nc = S // L
    chunk_spec = lambda n: pl.BlockSpec((L, n), lambda b, c: (b * nc + c, 0))
    hid = pl.pallas_call(
        _ml_chunk_kernel,
        grid=(B, nc),
        in_specs=[chunk_spec(QK), chunk_spec(QK), chunk_spec(V), chunk_spec(V), chunk_spec(LANES),
                  _resident((1, V))],
        out_specs=chunk_spec(V),
        out_shape=jax.ShapeDtypeStruct((T, V), BF16),
        scratch_shapes=[pltpu.VMEM((H * ML_QK // LANES, LANES, 2 * ML_V), F32),
                        pltpu.VMEM((1, LANES), F32)],
        compiler_params=_params(2),
        name="mlstm_chunk",
    )(q, k, v, o, gates, out_norm.reshape(1, V))
    return _out_proj(hid, w_o.astype(BF16), x)


def _rw_proj_kernel(x_ref, g_ref, mu_ref, wr_ref, wk_ref, wv_ref, w1_ref, w2_ref, a1_ref, a2_ref,
                    g1_ref, g2_ref, w0_ref, a0_ref, kk_ref, ka_ref,
                    r_out, k_out, v_out, lw_out, a_out, kkraw_out, g_out, hbuf, *, tiles_per_seq):
    tm = x_ref.shape[0]
    D = x_ref.shape[1]
    h = _rms(x_ref[...], g_ref[...])

    @pl.when(pl.program_id(0) % tiles_per_seq == 0)
    def _():
        hbuf[0:8, :] = jnp.zeros((8, D), F32)

    hbuf[8:8 + tm, :] = h
    xx = hbuf[7:7 + tm, :] - h
    hbuf[0:8, :] = hbuf[tm:tm + 8, :]
    mix = lambda j: (h + xx * mu_ref[j:j + 1, :]).astype(BF16)
    r = jnp.dot(mix(0), wr_ref[...], preferred_element_type=F32)
    k = jnp.dot(mix(2), wk_ref[...], preferred_element_type=F32)
    v = jnp.dot(mix(3), wv_ref[...], preferred_element_type=F32)
    zw = w0_ref[...] + _dot(jnp.tanh(jnp.dot(mix(1), w1_ref[...], preferred_element_type=F32)),
                            w2_ref[...])
    w_log = -_softplus(-zw) - 0.5
    a = _sigmoid(a0_ref[...] + _dot(jnp.dot(mix(4), a1_ref[...], preferred_element_type=F32),
                                    a2_ref[...]))
    g = _dot(_sigmoid(jnp.dot(mix(5), g1_ref[...], preferred_element_type=F32)), g2_ref[...])
    r_out[...] = r
    v_out[...] = v
    lw_out[...] = -jnp.exp(w_log)
    a_out[...] = a
    kkraw_out[...] = k * kk_ref[...]
    k_out[...] = k * (1.0 + (a - 1.0) * ka_ref[...])
    g_out[...] = g


def _group_sum(x, ones_bd):
    hi = x.astype(BF16)
    lo = (x - hi.astype(F32)).astype(BF16)
    return (jnp.dot(hi, ones_bd, preferred_element_type=F32)
            + jnp.dot(lo, ones_bd, preferred_element_type=F32))


def _rw_chunk_kernel(r_ref, k_ref, v_ref, lw_ref, a_ref, kk_ref, g_ref, rk_ref, lnw_ref, lnb_ref,
                     out_ref, s_scr):
    L = r_ref.shape[0]
    W = r_ref.shape[1]
    N = RW_HEAD
    nh = W // N

    @pl.when(pl.program_id(2) == 0)
    def _():
        s_scr[...] = jnp.zeros(s_scr.shape, F32)

    lane_head = lax.broadcasted_iota(jnp.int32, (L, W), 1) // N
    er = lax.broadcasted_iota(jnp.int32, (nh * L, W), 0)
    ec = lax.broadcasted_iota(jnp.int32, (nh * L, W), 1)
    same_head = (er // L) == (ec // N)
    sr = lax.broadcasted_iota(jnp.int32, (W, W), 0)
    sc = lax.broadcasted_iota(jnp.int32, (W, W), 1)
    ones_bd = ((sr // N) == (sc // N)).astype(BF16)
    gr = lax.broadcasted_iota(jnp.int32, (nh * L, nh * L), 0)
    gc = lax.broadcasted_iota(jnp.int32, (nh * L, nh * L), 1)
    g_same = (gr // L) == (gc // L)
    strict = g_same & ((gc % L) < (gr % L))
    incl = g_same & ((gc % L) <= (gr % L))

    def expand(x):
        return jnp.where(same_head, jnp.concatenate([x] * nh, axis=0), 0.0)

    r = r_ref[...]
    k = k_ref[...]
    v = v_ref[...]
    lw = lw_ref[...]
    a_sig = a_ref[...]
    kk = kk_ref[...]
    kk = kk / jnp.maximum(jnp.sqrt(_group_sum(kk * kk, ones_bd)), 1e-12)
    av = -kk
    bv = kk * a_sig

    tr = lax.broadcasted_iota(jnp.int32, (L, L), 0)
    tc = lax.broadcasted_iota(jnp.int32, (L, L), 1)
    tril_b = (tc <= tr).astype(BF16)
    cum = sum(jnp.dot(tril_b, piece, preferred_element_type=F32) for piece in _split3(lw))
    p_in = jnp.exp(cum)
    p_inv = jnp.exp(-cum)
    p_prev = jnp.exp(cum - lw)
    p_last = p_in[L - 1:L, :]

    a_e = expand(p_prev * av)
    r_e = expand(p_in * r)
    b_e = expand(p_inv * bv)
    k_e = expand(p_inv * k)
    v_e = expand(v)
    m_ab = jnp.where(strict, _dot_nt(a_e, b_e), 0.0)
    m_ak = jnp.where(strict, _dot_nt(a_e, k_e), 0.0)
    m_rb = jnp.where(incl, _dot_nt(r_e, b_e), 0.0)
    m_rk = jnp.where(incl, _dot_nt(r_e, k_e), 0.0)

    eye = (gr == gc).astype(F32)
    pw = m_ab
    tinv = eye + m_ab
    for _ in range(int(math.log2(L)) - 1):
        pw = _dot(pw, pw)
        tinv = tinv + _dot(tinv, pw)

    s0 = s_scr[...]
    w1 = _dot(tinv, a_e)
    w2 = _dot(tinv, _dot(m_ak, v_e))
    u_e = _dot_nt(w1, s0) + w2
    y_e = _dot(m_rb, u_e) + _dot(m_rk, v_e)
    y = _dot_nt(p_in * r, s0)
    for hh in range(nh):
        y = y + y_e[hh * L:(hh + 1) * L, :]
    s_scr[...] = s0 * p_last + _dot_tn(u_e, b_e * p_last) + _dot_tn(v_e, k_e * p_last)

    inv_n = 1.0 / N
    mean = _group_sum(y, ones_bd) * inv_n
    yc = y - mean
    var = _group_sum(yc * yc, ones_bd) * inv_n
    yn = yc * lax.rsqrt(var + GN_EPS) * lnw_ref[...] + lnb_ref[...]
    bonus = _group_sum(r * k * rk_ref[...], ones_bd) * v
    out_ref[...] = ((yn + bonus) * g_ref[...]).astype(out_ref.dtype)
    del lane_head


def _rwkv(x, mix_g, mu, w_r, w_k, w_v, w0, w1, w2, a0, a1, a2, g1, g2, k_k, k_a, r_k,
          ln_w, ln_b, w_o, B, S):
    T, D = x.shape
    pad_cols = lambda w: jnp.pad(w, ((0, 0), (0, LANES - w.shape[1]))).astype(BF16)
    pad_rows = lambda w: jnp.pad(w, ((0, LANES - w.shape[0]), (0, 0))).astype(BF16)
    vec = lambda t: t.reshape(1, D)
    mu8 = jnp.pad(mu, ((0, 8 - mu.shape[0]), (0, 0)))
    tm = min(TM_PROJ, S)
    nt = S // tm
    row_spec = pl.BlockSpec((tm, D), lambda i: (i, 0))
    outs = pl.pallas_call(
        functools.partial(_rw_proj_kernel, tiles_per_seq=nt),
        grid=(T // tm,),
        in_specs=[row_spec, _resident((1, D)), _resident((8, D)),
                  _resident((D, D)), _resident((D, D)), _resident((D, D)),
                  _resident((D, LANES)), _resident((LANES, D)),
                  _resident((D, LANES)), _resident((LANES, D)),
                  _resident((D, LANES)), _resident((LANES, D)),
                  _resident((1, D)), _resident((1, D)), _resident((1, D)), _resident((1, D))],
        out_specs=[row_spec] * 7,
        out_shape=[jax.ShapeDtypeStruct((T, D), F32)] * 7,
        scratch_shapes=[pltpu.VMEM((tm + 8, D), F32)],
        compiler_params=_params(1),
        name="rwkv_proj",
    )(x, vec(mix_g), mu8, w_r.astype(BF16), w_k.astype(BF16), w_v.astype(BF16),
      pad_cols(w1), pad_rows(w2), pad_cols(a1), pad_rows(a2), pad_cols(g1), pad_rows(g2),
      vec(w0), vec(a0), vec(k_k), vec(k_a))
    r, k, v, lw, a, kkraw, g = outs

    L = min(RW_CHUNK, S)
    nc = S // L
    W = RW_GROUP * RW_HEAD
    ng = D // W
    chunk_spec = pl.BlockSpec((L, W), lambda b, gi, c: (b * nc + c, gi))
    vec_spec = pl.BlockSpec((1, W), lambda b, gi, c: (0, gi))
    yg = pl.pallas_call(
        _rw_chunk_kernel,
        grid=(B, ng, nc),
        in_specs=[chunk_spec] * 7 + [vec_spec] * 3,
        out_specs=chunk_spec,
        out_shape=jax.ShapeDtypeStruct((T, D), BF16),
        scratch_shapes=[pltpu.VMEM((W, W), F32)],
        compiler_params=_params(3),
        name="rwkv_chunk",
    )(r, k, v, lw, a, kkraw, g, vec(r_k), vec(ln_w), vec(ln_b))
    return _out_proj(yg, w_o.astype(BF16), x)


def kernel(x, positions, ffn_norm, ffn_w_gate, ffn_w_up, ffn_w_down, mix_norm, final_norm, mla_w_a, mla_q_norm, mla_w_qb, mla_kv_norm, mla_w_kvb, mla_w_o, ml_w_in, ml_b_if, ml_conv_w, ml_conv_b, ml_out_norm, ml_w_o, rw_mu, rw_w_r, rw_w_k, rw_w_v, rw_w0, rw_w1, rw_w2, rw_a0, rw_a1, rw_a2, rw_g1, rw_g2, rw_k_k, rw_k_a, rw_r_k, rw_ln_w, rw_ln_b, rw_w_o):
    B, S, D = x.shape
    depth = mix_norm.shape[0]
    n_mixers = 3
    h = x.reshape(B * S, D)

    def ffn(h, layer, half):
        return _ffn(h, ffn_norm[layer, half], ffn_w_gate[layer, half].astype(BF16),
                    ffn_w_up[layer, half].astype(BF16), ffn_w_down[layer, half].astype(BF16))

    for layer in range(depth):
        h = ffn(h, layer, 0)
        kind, j = layer % n_mixers, layer // n_mixers
        if kind == 0:
            h = _mla(h, positions, mix_norm[layer], mla_w_a[j], mla_q_norm[j], mla_w_qb[j],
                     mla_kv_norm[j], mla_w_kvb[j], mla_w_o[j], B, S)
        elif kind == 1:
            h = _mlstm(h, mix_norm[layer], ml_w_in[j], ml_b_if[j], ml_conv_w[j], ml_conv_b[j],
                       ml_out_norm[j], ml_w_o[j], B, S)
        else:
            h = _rwkv(h, mix_norm[layer], rw_mu[j], rw_w_r[j], rw_w_k[j], rw_w_v[j], rw_w0[j],
                      rw_w1[j], rw_w2[j], rw_a0[j], rw_a1[j], rw_a2[j], rw_g1[j], rw_g2[j],
                      rw_k_k[j], rw_k_a[j], rw_r_k[j], rw_ln_w[j], rw_ln_b[j], rw_w_o[j], B, S)
        h = ffn(h, layer, 1)
    return _final_norm(h, final_norm).reshape(B, S, D)
```

```python
import functools
import math

import jax
import jax.numpy as jnp
from jax import lax
from jax.experimental import pallas as pl
from jax.experimental.pallas import tpu as pltpu

F32 = jnp.float32
BF16 = jnp.bfloat16

RMS_EPS = 1e-6
GN_EPS = 64e-5
LANES = 128
MXU_DIM = 256
VMEM_LIMIT = 56 * 1024 * 1024

MLA_HEADS = 8
MLA_Q_RANK = 512
MLA_KV_RANK = 256
MLA_NOPE = 128
MLA_ROPE = 64
MLA_V = 128
ROPE_THETA = 10000.0
ML_HEADS = 8
ML_QK = 64
ML_V = 128
ML_CONV = 4
RW_HEAD = 64

TM_FFN = 512
TM_PROJ = 256
TQ_ATTN = 512
ATTN_HEADS_PER_STEP = 4
ML_CHUNK = 128
RW_CHUNK = 64
RW_GROUP = MXU_DIM // RW_HEAD


def _params(n_axes):
    return pltpu.CompilerParams(dimension_semantics=("arbitrary",) * n_axes,
                                vmem_limit_bytes=VMEM_LIMIT)


def _resident(shape):
    nd = len(shape)
    return pl.BlockSpec(shape, lambda *_: (0,) * nd, pipeline_mode=pl.Buffered(1))


def _rms(x, g, eps=RMS_EPS):
    return x * lax.rsqrt(jnp.mean(x * x, axis=-1, keepdims=True) + eps) * g


def _dot(a, b):
    return jnp.dot(a.astype(BF16), b.astype(BF16), preferred_element_type=F32)


def _dot_nt(a, b):
    return lax.dot_general(a.astype(BF16), b.astype(BF16), (((1,), (1,)), ((), ())),
                           preferred_element_type=F32)


def _dot_tn(a, b):
    return lax.dot_general(a.astype(BF16), b.astype(BF16), (((0,), (0,)), ((), ())),
                           preferred_element_type=F32)


def _split3(x):
    hi = x.astype(BF16)
    r1 = x - hi.astype(F32)
    mid = r1.astype(BF16)
    lo = (r1 - mid.astype(F32)).astype(BF16)
    return hi, mid, lo


def _softplus(y):
    return jnp.maximum(y, 0.0) + jnp.log1p(jnp.exp(-jnp.abs(y)))


def _sigmoid(y):
    return 1.0 / (1.0 + jnp.exp(-y))


def _ffn_kernel(x_ref, g_ref, wg_ref, wu_ref, wd_ref, o_ref):
    x = x_ref[...]
    h = _rms(x, g_ref[...]).astype(BF16)
    gate = jnp.dot(h, wg_ref[...], preferred_element_type=F32)
    up = jnp.dot(h, wu_ref[...], preferred_element_type=F32)
    act = (gate * _sigmoid(gate) * up).astype(BF16)
    y = jnp.dot(act, wd_ref[...], preferred_element_type=F32)
    o_ref[...] = x + 0.5 * y


def _ffn(x, g, wg, wu, wd):
    T, D = x.shape
    F = wg.shape[1]
    tm = min(TM_FFN, T)
    return pl.pallas_call(
        _ffn_kernel,
        grid=(T // tm,),
        in_specs=[pl.BlockSpec((tm, D), lambda i: (i, 0)),
                  _resident((1, D)), _resident((D, F)), _resident((D, F)), _resident((F, D))],
        out_specs=pl.BlockSpec((tm, D), lambda i: (i, 0)),
        out_shape=jax.ShapeDtypeStruct((T, D), F32),
        compiler_params=_params(1),
        name="ffn",
    )(x, g.reshape(1, D), wg, wu, wd)


def _norm_kernel(x_ref, g_ref, o_ref):
    o_ref[...] = _rms(x_ref[...], g_ref[...])


def _final_norm(x, g):
    T, D = x.shape
    tm = min(TM_FFN, T)
    return pl.pallas_call(
        _norm_kernel,
        grid=(T // tm,),
        in_specs=[pl.BlockSpec((tm, D), lambda i: (i, 0)), _resident((1, D))],
        out_specs=pl.BlockSpec((tm, D), lambda i: (i, 0)),
        out_shape=jax.ShapeDtypeStruct((T, D), F32),
        compiler_params=_params(1),
        name="final_norm",
    )(x, g.reshape(1, D))


def _out_proj_kernel(a_ref, w_ref, res_ref, o_ref):
    o_ref[...] = res_ref[...] + jnp.dot(a_ref[...], w_ref[...], preferred_element_type=F32)


def _out_proj(a, w, res):
    T, K = a.shape
    N = w.shape[1]
    tm = min(TM_FFN, T)
    return pl.pallas_call(
        _out_proj_kernel,
        grid=(T // tm,),
        in_specs=[pl.BlockSpec((tm, K), lambda i: (i, 0)), _resident((K, N)),
                  pl.BlockSpec((tm, N), lambda i: (i, 0))],
        out_specs=pl.BlockSpec((tm, N), lambda i: (i, 0)),
        out_shape=jax.ShapeDtypeStruct((T, N), F32),
        compiler_params=_params(1),
        name="out_proj",
    )(a, w, res)


def _mla_proj_kernel(x_ref, g_ref, pos_ref, invf_ref, sgn_ref, wa_ref, qn_ref, wqb_ref,
                     kvn_ref, wkvb_ref, q_ref, k_ref, vt_ref, *, q_scale):
    H = MLA_HEADS
    h = _rms(x_ref[...], g_ref[...]).astype(BF16)
    lat = jnp.dot(h, wa_ref[...], preferred_element_type=F32)
    cq = _rms(lat[:, :MLA_Q_RANK], qn_ref[...]).astype(BF16)
    ckv = _rms(lat[:, MLA_Q_RANK:MLA_Q_RANK + MLA_KV_RANK], kvn_ref[...]).astype(BF16)
    q = jnp.dot(cq, wqb_ref[...], preferred_element_type=F32) * q_scale
    kv = jnp.dot(ckv, wkvb_ref[...], preferred_element_type=F32)
    ang = pos_ref[...].astype(F32) * invf_ref[...]
    cos = jnp.cos(ang)
    sin = jnp.sin(ang) * sgn_ref[...]
    kpe0 = MLA_Q_RANK + MLA_KV_RANK
    kx = lat[:, kpe0:kpe0 + LANES] * cos + lat[:, kpe0 + LANES:kpe0 + 2 * LANES] * sin
    half = lax.broadcasted_iota(jnp.int32, kx.shape, 1) // MLA_ROPE
    pe0 = H * MLA_NOPE
    sw0 = pe0 + H * MLA_ROPE
    for hd in range(H):
        p, m = divmod(hd, LANES // MLA_ROPE)
        qpe = (q[:, pe0 + p * LANES:pe0 + (p + 1) * LANES] * cos
               + q[:, sw0 + p * LANES:sw0 + (p + 1) * LANES] * sin)
        sel = half == m
        q_ref[0, hd, :, 0:LANES] = q[:, hd * MLA_NOPE:(hd + 1) * MLA_NOPE].astype(BF16)
        q_ref[0, hd, :, LANES:2 * LANES] = jnp.where(sel, qpe, 0.0).astype(BF16)
        k0 = hd * (MLA_NOPE + MLA_V)
        k_ref[0, hd, :, 0:LANES] = kv[:, k0:k0 + MLA_NOPE].astype(BF16)
        k_ref[0, hd, :, LANES:2 * LANES] = jnp.where(sel, kx, 0.0).astype(BF16)
        vt_ref[0, hd] = kv[:, k0 + MLA_NOPE:k0 + MLA_NOPE + MLA_V].T.astype(BF16)


def _attn_kernel(q_ref, k_ref, vt_ref, o_ref, *, tq):
    i = pl.program_id(2)
    heads = range(q_ref.shape[1])
    dv = vt_ref.shape[2]
    q = [q_ref[0, h] for h in heads]

    def step(j, carry, diagonal):
        ms, ls, accs = carry
        start = pl.multiple_of(j * tq, tq)
        s = [lax.dot_general(k_ref[0, h, pl.ds(start, tq), :], q[h], (((1,), (1,)), ((), ())),
                             preferred_element_type=F32) for h in heads]
        if diagonal:
            key = lax.broadcasted_iota(jnp.int32, s[0].shape, 0)
            qry = lax.broadcasted_iota(jnp.int32, s[0].shape, 1)
            s = [jnp.where(key <= qry, x, -jnp.inf) for x in s]
        m_new = [jnp.maximum(ms[h], jnp.max(s[h], axis=0, keepdims=True)) for h in heads]
        alpha = [jnp.exp2(ms[h] - m_new[h]) for h in heads]
        p = [jnp.exp2(s[h] - m_new[h]) for h in heads]
        ls = [alpha[h] * ls[h] + jnp.sum(p[h], axis=0, keepdims=True) for h in heads]
        accs = [alpha[h] * accs[h] + jnp.dot(vt_ref[0, h, :, pl.ds(start, tq)], p[h].astype(BF16),
                                             preferred_element_type=F32) for h in heads]
        return m_new, ls, accs

    init = ([jnp.full((1, tq), -jnp.inf, F32) for _ in heads],
            [jnp.zeros((1, tq), F32) for _ in heads],
            [jnp.zeros((dv, tq), F32) for _ in heads])
    carry = lax.fori_loop(0, i, lambda j, c: step(j, c, False), init)
    _, ls, accs = step(i, carry, True)
    for h in heads:
        o_ref[:, h * dv:(h + 1) * dv] = (accs[h] / ls[h]).T.astype(o_ref.dtype)


def _mla(x, positions, mix_g, w_a, q_norm, w_qb, kv_norm, w_kvb, w_o, B, S):
    T, D = x.shape
    H = MLA_HEADS
    kpe = w_a[:, MLA_Q_RANK + MLA_KV_RANK:]
    kpe_sw = jnp.concatenate([kpe[:, MLA_ROPE // 2:], kpe[:, :MLA_ROPE // 2]], axis=1)
    reps = LANES // MLA_ROPE
    wa2 = jnp.concatenate([w_a[:, :MLA_Q_RANK + MLA_KV_RANK], jnp.tile(kpe, (1, reps)),
                           jnp.tile(kpe_sw, (1, reps))], axis=1).astype(BF16)
    wq3 = w_qb.reshape(MLA_Q_RANK, H, MLA_NOPE + MLA_ROPE)
    q_nope = wq3[:, :, :MLA_NOPE].reshape(MLA_Q_RANK, H * MLA_NOPE)
    q_pe = wq3[:, :, MLA_NOPE:]
    q_pe_sw = jnp.concatenate([q_pe[:, :, MLA_ROPE // 2:], q_pe[:, :, :MLA_ROPE // 2]], axis=2)
    wqb2 = jnp.concatenate([q_nope, q_pe.reshape(MLA_Q_RANK, H * MLA_ROPE),
                            q_pe_sw.reshape(MLA_Q_RANK, H * MLA_ROPE)], axis=1).astype(BF16)
    inv_freq = 1.0 / (ROPE_THETA ** (jnp.arange(0, MLA_ROPE, 2, dtype=F32) / MLA_ROPE))
    invf = jnp.tile(inv_freq, LANES // (MLA_ROPE // 2)).reshape(1, LANES)
    first_half = (jnp.arange(LANES) % MLA_ROPE) < MLA_ROPE // 2
    sgn = jnp.where(first_half, -1.0, 1.0).astype(F32).reshape(1, LANES)
    q_scale = (MLA_NOPE + MLA_ROPE) ** -0.5 * math.log2(math.e)

    tm = min(TM_PROJ, S)
    nt = S // tm
    wa_cols = wa2.shape[1]
    wq_cols = wqb2.shape[1]
    kv_cols = w_kvb.shape[1]
    qk_dim = 2 * LANES
    q, k, v = pl.pallas_call(
        functools.partial(_mla_proj_kernel, q_scale=q_scale),
        grid=(B, nt),
        in_specs=[pl.BlockSpec((tm, D), lambda b, i: (b * nt + i, 0)),
                  _resident((1, D)),
                  pl.BlockSpec((tm, 1), lambda b, i: (b * nt + i, 0)),
                  _resident((1, LANES)), _resident((1, LANES)),
                  _resident((D, wa_cols)), _resident((1, MLA_Q_RANK)),
                  _resident((MLA_Q_RANK, wq_cols)), _resident((1, MLA_KV_RANK)),
                  _resident((MLA_KV_RANK, kv_cols))],
        out_specs=[pl.BlockSpec((1, H, tm, qk_dim), lambda b, i: (b, 0, i, 0)),
                   pl.BlockSpec((1, H, tm, qk_dim), lambda b, i: (b, 0, i, 0)),
                   pl.BlockSpec((1, H, MLA_V, tm), lambda b, i: (b, 0, 0, i))],
        out_shape=[jax.ShapeDtypeStruct((B, H, S, qk_dim), BF16),
                   jax.ShapeDtypeStruct((B, H, S, qk_dim), BF16),
                   jax.ShapeDtypeStruct((B, H, MLA_V, S), BF16)],
        compiler_params=_params(2),
        name="mla_proj",
    )(x, mix_g.reshape(1, D), positions.reshape(T, 1), invf, sgn, wa2, q_norm.reshape(1, -1),
      wqb2, kv_norm.reshape(1, -1), w_kvb.astype(BF16))

    tq = min(TQ_ATTN, S)
    nq = S // tq
    hp = ATTN_HEADS_PER_STEP
    o = pl.pallas_call(
        functools.partial(_attn_kernel, tq=tq),
        grid=(B, H // hp, nq),
        in_specs=[pl.BlockSpec((1, hp, tq, qk_dim), lambda b, h, i: (b, h, i, 0)),
                  pl.BlockSpec((1, hp, S, qk_dim), lambda b, h, i: (b, h, 0, 0),
                               pipeline_mode=pl.Buffered(1)),
                  pl.BlockSpec((1, hp, MLA_V, S), lambda b, h, i: (b, h, 0, 0),
                               pipeline_mode=pl.Buffered(1))],
        out_specs=pl.BlockSpec((tq, hp * MLA_V), lambda b, h, i: (b * nq + i, h)),
        out_shape=jax.ShapeDtypeStruct((T, H * MLA_V), BF16),
        compiler_params=_params(3),
        name="mla_attn",
    )(q, k, v)
    return _out_proj(o, w_o.astype(BF16), x)


def _ml_proj_kernel(x_ref, g_ref, w_ref, wg_ref, bif_ref, cw_ref, cb_ref,
                    q_ref, k_ref, v_ref, o_ref, gates_ref, pbuf, *, tiles_per_seq):
    tm = x_ref.shape[0]
    QK = ML_HEADS * ML_QK
    V = ML_HEADS * ML_V
    h = _rms(x_ref[...], g_ref[...]).astype(BF16)
    p = jnp.dot(h, w_ref[...], preferred_element_type=F32)
    gates_ref[...] = jnp.dot(h, wg_ref[...], preferred_element_type=F32) + bif_ref[...]
    v_ref[...] = p[:, 2 * QK:2 * QK + V].astype(BF16)
    o_ref[...] = p[:, 2 * QK + V:2 * QK + 2 * V]

    @pl.when(pl.program_id(0) % tiles_per_seq == 0)
    def _():
        pbuf[0:8, :] = jnp.zeros((8, 2 * QK), F32)

    pbuf[8:8 + tm, :] = p[:, :2 * QK]
    acc = cb_ref[...] + cw_ref[ML_CONV - 1:ML_CONV, :] * p[:, :2 * QK]
    for kk in range(ML_CONV - 1):
        shift = ML_CONV - 1 - kk
        acc = acc + cw_ref[kk:kk + 1, :] * pbuf[8 - shift:8 - shift + tm, :]
    pbuf[0:8, :] = pbuf[tm:tm + 8, :]
    qk = acc * _sigmoid(acc)
    q_ref[...] = qk[:, :QK].astype(BF16)
    k_ref[...] = (qk[:, QK:] * ML_QK ** -0.5).astype(BF16)


def _ml_chunk_kernel(q_ref, k_ref, v_ref, o_ref, g_ref, on_ref, out_ref, c_scr, m_scr):
    L = q_ref.shape[0]
    H = ML_HEADS
    pair = LANES // ML_QK

    @pl.when(pl.program_id(1) == 0)
    def _():
        c_scr[...] = jnp.zeros(c_scr.shape, F32)
        m_scr[...] = jnp.zeros(m_scr.shape, F32)

    G = g_ref[...]
    GT = G.T
    lfG = -_softplus(-G)
    lfGT = -_softplus(-GT)
    row = lax.broadcasted_iota(jnp.int32, (L, L), 0)
    col = lax.broadcasted_iota(jnp.int32, (L, L), 1)
    tril = col <= row
    tril_b = tril.astype(BF16)
    triu_b = (row <= col).astype(BF16)
    fc_cols = sum(jnp.dot(tril_b, piece, preferred_element_type=F32) for piece in _split3(lfG))
    fc_rows = sum(jnp.dot(piece, triu_b, preferred_element_type=F32) for piece in _split3(lfGT))
    m_row = m_scr[...]
    lane = lax.broadcasted_iota(jnp.int32, (L, LANES), 1)
    ones_blk = (lane == 0).astype(BF16)
    lane_half = lane // ML_QK
    srow_half = lax.broadcasted_iota(jnp.int32, (pair * ML_QK, 2 * ML_V), 0) // ML_QK
    lane1 = lax.broadcasted_iota(jnp.int32, (1, LANES), 1)
    m_out = m_row
    for hd in range(H):
        p, mm = divmod(hd, pair)
        fc_col = fc_cols[:, H + hd:H + hd + 1]
        fc_row = fc_rows[H + hd:H + hd + 1, :]
        i_col = G[:, hd:hd + 1]
        i_row = GT[hd:hd + 1, :]
        m_prev = m_row[:, hd:hd + 1]
        dmat = jnp.where(tril, fc_col - fc_row + i_row, -jnp.inf)
        inter = fc_col + m_prev
        m_t = jnp.maximum(inter, jnp.max(dmat, axis=-1, keepdims=True))
        dw = jnp.exp(dmat - m_t)
        inter_w = jnp.exp(inter - m_t)
        qp = q_ref[:, p * LANES:(p + 1) * LANES]
        kp = k_ref[:, p * LANES:(p + 1) * LANES]
        sel = lane_half == mm
        qm = jnp.where(sel, qp, jnp.zeros_like(qp))
        km = jnp.where(sel, kp, jnp.zeros_like(kp))
        s = _dot_nt(qm, kp) * dw
        vh = jnp.concatenate([v_ref[:, hd * ML_V:(hd + 1) * ML_V], ones_blk], axis=1)
        cp = c_scr[p]
        nd = _dot(s, vh) + inter_w * _dot(qm, cp)
        num = nd[:, :ML_V]
        den = nd[:, ML_V:ML_V + 1]
        hid = num / jnp.maximum(jnp.abs(den), jnp.exp(-m_t))
        hid = hid * lax.rsqrt(jnp.mean(hid * hid, axis=-1, keepdims=True) + RMS_EPS)
        og = o_ref[:, hd * ML_V:(hd + 1) * ML_V]
        out_ref[:, hd * ML_V:(hd + 1) * ML_V] = (
            hid * on_ref[:, hd * ML_V:(hd + 1) * ML_V] * _sigmoid(og)).astype(out_ref.dtype)
        m_new = m_t[L - 1:L, :]
        fc_last = fc_col[L - 1:L, :]
        decay = jnp.exp(fc_last + m_prev - m_new)
        w_col = jnp.exp(fc_last - fc_col + i_col - m_new)
        upd = _dot_tn(km, w_col * vh.astype(F32))
        c_scr[p] = jnp.where(srow_half == mm, decay * cp, cp) + upd
        m_out = jnp.where(lane1 == hd, m_new, m_out)
    m_scr[...] = m_out


def _mlstm(x, mix_g, w_in, b_if, conv_w, conv_b, out_norm, w_o, B, S):
    T, D = x.shape
    H = ML_HEADS
    QK = H * ML_QK
    V = H * ML_V
    main = 2 * QK + 2 * V
    w_main = w_in[:, :main].astype(BF16)
    w_gate = jnp.pad(w_in[:, main:], ((0, 0), (0, LANES - 2 * H))).astype(BF16)
    bif = jnp.pad(b_if, (0, LANES - 2 * H)).reshape(1, LANES)
    tm = min(TM_PROJ, S)
    nt = S // tm
    row_spec = lambda n: pl.BlockSpec((tm, n), lambda i: (i, 0))
    q, k, v, o, gates = pl.pallas_call(
        functools.partial(_ml_proj_kernel, tiles_per_seq=nt),
        grid=(T // tm,),
        in_specs=[row_spec(D), _resident((1, D)), _resident((D, main)), _resident((D, LANES)),
                  _resident((1, LANES)), _resident((ML_CONV, 2 * QK)), _resident((1, 2 * QK))],
        out_specs=[row_spec(QK), row_spec(QK), row_spec(V), row_spec(V), row_spec(LANES)],
        out_shape=[jax.ShapeDtypeStruct((T, QK), BF16), jax.ShapeDtypeStruct((T, QK), BF16),
                   jax.ShapeDtypeStruct((T, V), BF16), jax.ShapeDtypeStruct((T, V), F32),
                   jax.ShapeDtypeStruct((T, LANES), F32)],
        scratch_shapes=[pltpu.VMEM((tm + 8, 2 * QK), F32)],
        compiler_params=_params(1),
        name="mlstm_proj",
    )(x, mix_g.reshape(1, D), w_main, w_gate, bif, conv_w, conv_b.reshape(1, -1))

    L = min(ML_CHUNK, S)
    nc = S // L
    chunk_spec = lambda n: pl.BlockSpec((L, n), lambda b, c: (b * nc + c, 0))
    hid = pl.pallas_call(
        _ml_chunk_kernel,
        grid=(B, nc),
        in_specs=[chunk_spec(QK), chunk_spec(QK), chunk_spec(V), chunk_spec(V), chunk_spec(LANES),
                  _resident((1, V))],
        out_specs=chunk_spec(V),
        out_shape=jax.ShapeDtypeStruct((T, V), BF16),
        scratch_shapes=[pltpu.VMEM((H * ML_QK // LANES, LANES, 2 * ML_V), F32),
                        pltpu.VMEM((1, LANES), F32)],
        compiler_params=_params(2),
        name="mlstm_chunk",
    )(q, k, v, o, gates, out_norm.reshape(1, V))
    return _out_proj(hid, w_o.astype(BF16), x)


def _rw_proj_kernel(x_ref, g_ref, mu_ref, wr_ref, wk_ref, wv_ref, w1_ref, w2_ref, a1_ref, a2_ref,
                    g1_ref, g2_ref, w0_ref, a0_ref, kk_ref, ka_ref,
                    r_out, k_out, v_out, lw_out, a_out, kkraw_out, g_out, hbuf, *, tiles_per_seq):
    tm = x_ref.shape[0]
    D = x_ref.shape[1]
    h = _rms(x_ref[...], g_ref[...])

    @pl.when(pl.program_id(0) % tiles_per_seq == 0)
    def _():
        hbuf[0:8, :] = jnp.zeros((8, D), F32)

    hbuf[8:8 + tm, :] = h
    xx = hbuf[7:7 + tm, :] - h
    hbuf[0:8, :] = hbuf[tm:tm + 8, :]
    mix = lambda j: (h + xx * mu_ref[j:j + 1, :]).astype(BF16)
    r = jnp.dot(mix(0), wr_ref[...], preferred_element_type=F32)
    k = jnp.dot(mix(2), wk_ref[...], preferred_element_type=F32)
    v = jnp.dot(mix(3), wv_ref[...], preferred_element_type=F32)
    zw = w0_ref[...] + _dot(jnp.tanh(jnp.dot(mix(1), w1_ref[...], preferred_element_type=F32)),
                            w2_ref[...])
    w_log = -_softplus(-zw) - 0.5
    a = _sigmoid(a0_ref[...] + _dot(jnp.dot(mix(4), a1_ref[...], preferred_element_type=F32),
                                    a2_ref[...]))
    g = _dot(_sigmoid(jnp.dot(mix(5), g1_ref[...], preferred_element_type=F32)), g2_ref[...])
    r_out[...] = r
    v_out[...] = v
    lw_out[...] = -jnp.exp(w_log)
    a_out[...] = a
    kkraw_out[...] = k * kk_ref[...]
    k_out[...] = k * (1.0 + (a - 1.0) * ka_ref[...])
    g_out[...] = g


def _group_sum(x, ones_bd):
    hi = x.astype(BF16)
    lo = (x - hi.astype(F32)).astype(BF16)
    return (jnp.dot(hi, ones_bd, preferred_element_type=F32)
            + jnp.dot(lo, ones_bd, preferred_element_type=F32))


def _rw_chunk_kernel(r_ref, k_ref, v_ref, lw_ref, a_ref, kk_ref, g_ref, rk_ref, lnw_ref, lnb_ref,
                     out_ref, s_scr):
    L = r_ref.shape[0]
    W = s_scr.shape[1]
    N = RW_HEAD
    nh = W // N
    ng = r_ref.shape[1] // W

    @pl.when(pl.program_id(1) == 0)
    def _():
        s_scr[...] = jnp.zeros(s_scr.shape, F32)

    er = lax.broadcasted_iota(jnp.int32, (nh * L, W), 0)
    ec = lax.broadcasted_iota(jnp.int32, (nh * L, W), 1)
    same_head = (er // L) == (ec // N)
    sr = lax.broadcasted_iota(jnp.int32, (W, W), 0)
    sc = lax.broadcasted_iota(jnp.int32, (W, W), 1)
    state_bd = (sr // N) == (sc // N)
    ones_bd = state_bd.astype(BF16)
    lt = lax.broadcasted_iota(jnp.int32, (L, W), 0)
    ls = lax.broadcasted_iota(jnp.int32, (L, W), 1) % L
    strict = ls < lt
    incl = ls <= lt
    eye = (ls == lt).astype(F32)
    tr = lax.broadcasted_iota(jnp.int32, (L, L), 0)
    tc = lax.broadcasted_iota(jnp.int32, (L, L), 1)
    tril_b = (tc <= tr).astype(BF16)
    inv_n = 1.0 / N
    zero_b = jnp.zeros((nh * L, W), BF16)

    def expand(x):
        return jnp.where(same_head, jnp.concatenate([x.astype(BF16)] * nh, axis=0), zero_b)

    groups = range(ng)
    cols = [slice(gi * W, (gi + 1) * W) for gi in groups]
    r = [r_ref[:, c] for c in cols]
    k = [k_ref[:, c] for c in cols]
    v = [v_ref[:, c] for c in cols]
    lw = [lw_ref[:, c] for c in cols]
    kk = [kk_ref[:, c] for c in cols]
    kk = [x / jnp.maximum(jnp.sqrt(_group_sum(x * x, ones_bd)), 1e-12) for x in kk]
    bv = [kk[g] * a_ref[:, cols[g]] for g in groups]
    cum = [sum(jnp.dot(tril_b, piece, preferred_element_type=F32) for piece in _split3(x))
           for x in lw]
    cum_last = [x[L - 1:L, :] for x in cum]
    ar = [jnp.concatenate([-jnp.exp(cum[g] - lw[g]) * kk[g], jnp.exp(cum[g]) * r[g]],
                          axis=0).astype(BF16) for g in groups]
    p_inv = [jnp.exp(-x) for x in cum]
    b_e = [expand(p_inv[g] * bv[g]) for g in groups]
    k_e = [expand(p_inv[g] * k[g]) for g in groups]
    v_e = [expand(x) for x in v]
    gb = [_dot_nt(ar[g], b_e[g]) for g in groups]
    gk = [_dot_nt(ar[g], k_e[g]) for g in groups]
    m_ab = [jnp.where(strict, x[:L], 0.0) for x in gb]
    m_rb = [jnp.where(incl, x[L:], 0.0) for x in gb]
    m_akrk = [jnp.concatenate([jnp.where(strict, x[:L], 0.0), jnp.where(incl, x[L:], 0.0)],
                              axis=0) for x in gk]

    pw = [_dot(x, expand(x)) for x in m_ab]
    tinv = [eye + x for x in m_ab]
    n_sq = int(math.log2(L)) - 1
    for it in range(n_sq):
        pw_e = [expand(x) for x in pw]
        if it + 1 < n_sq:
            both = [_dot(jnp.concatenate([pw[g], tinv[g]], axis=0), pw_e[g]) for g in groups]
            pw = [x[:L] for x in both]
            tinv = [tinv[g] + both[g][L:] for g in groups]
        else:
            tinv = [tinv[g] + _dot(tinv[g], pw_e[g]) for g in groups]

    s0 = [s_scr[g] for g in groups]
    ars = [_dot_nt(ar[g], s0[g]) for g in groups]
    mv = [_dot(m_akrk[g], v_e[g]) for g in groups]
    u = [_dot(tinv[g], expand(ars[g][:L] + mv[g][:L])) for g in groups]
    y = [ars[g][L:] + mv[g][L:] + _dot(m_rb[g], expand(u[g])) for g in groups]
    for g in groups:
        p_out = jnp.exp(cum_last[g] - cum[g])
        upd = _dot_tn(jnp.concatenate([u[g], v[g]], axis=0),
                      jnp.concatenate([p_out * bv[g], p_out * k[g]], axis=0))
        s_scr[g] = s0[g] * jnp.exp(cum_last[g]) + jnp.where(state_bd, upd, 0.0)

    mean = [_group_sum(x, ones_bd) * inv_n for x in y]
    yc = [y[g] - mean[g] for g in groups]
    var = [_group_sum(x * x, ones_bd) * inv_n for x in yc]
    for g in groups:
        c = cols[g]
        yn = yc[g] * lax.rsqrt(var[g] + GN_EPS) * lnw_ref[:, c] + lnb_ref[:, c]
        bonus = _group_sum(r[g] * k[g] * rk_ref[:, c], ones_bd) * v[g]
        out_ref[:, c] = ((yn + bonus) * g_ref[:, c]).astype(out_ref.dtype)


def _rwkv(x, mix_g, mu, w_r, w_k, w_v, w0, w1, w2, a0, a1, a2, g1, g2, k_k, k_a, r_k,
          ln_w, ln_b, w_o, B, S):
    T, D = x.shape
    pad_cols = lambda w: jnp.pad(w, ((0, 0), (0, LANES - w.shape[1]))).astype(BF16)
    pad_rows = lambda w: jnp.pad(w, ((0, LANES - w.shape[0]), (0, 0))).astype(BF16)
    vec = lambda t: t.reshape(1, D)
    mu8 = jnp.pad(mu, ((0, 8 - mu.shape[0]), (0, 0)))
    tm = min(TM_PROJ, S)
    nt = S // tm
    row_spec = pl.BlockSpec((tm, D), lambda i: (i, 0))
    outs = pl.pallas_call(
        functools.partial(_rw_proj_kernel, tiles_per_seq=nt),
        grid=(T // tm,),
        in_specs=[row_spec, _resident((1, D)), _resident((8, D)),
                  _resident((D, D)), _resident((D, D)), _resident((D, D)),
                  _resident((D, LANES)), _resident((LANES, D)),
                  _resident((D, LANES)), _resident((LANES, D)),
                  _resident((D, LANES)), _resident((LANES, D)),
                  _resident((1, D)), _resident((1, D)), _resident((1, D)), _resident((1, D))],
        out_specs=[row_spec] * 7,
        out_shape=[jax.ShapeDtypeStruct((T, D), F32)] * 7,
        scratch_shapes=[pltpu.VMEM((tm + 8, D), F32)],
        compiler_params=_params(1),
        name="rwkv_proj",
    )(x, vec(mix_g), mu8, w_r.astype(BF16), w_k.astype(BF16), w_v.astype(BF16),
      pad_cols(w1), pad_rows(w2), pad_cols(a1), pad_rows(a2), pad_cols(g1), pad_rows(g2),
      vec(w0), vec(a0), vec(k_k), vec(k_a))
    r, k, v, lw, a, kkraw, g = outs

    L = min(RW_CHUNK, S)
    nc = S // L
    W = RW_GROUP * RW_HEAD
    chunk_spec = pl.BlockSpec((L, D), lambda b, c: (b * nc + c, 0))
    yg = pl.pallas_call(
        _rw_chunk_kernel,
        grid=(B, nc),
        in_specs=[chunk_spec] * 7 + [_resident((1, D))] * 3,
        out_specs=chunk_spec,
        out_shape=jax.ShapeDtypeStruct((T, D), BF16),
        scratch_shapes=[pltpu.VMEM((D // W, W, W), F32)],
        compiler_params=_params(2),
        name="rwkv_chunk",
    )(r, k, v, lw, a, kkraw, g, vec(r_k), vec(ln_w), vec(ln_b))
    return _out_proj(yg, w_o.astype(BF16), x)


def kernel(x, positions, ffn_norm, ffn_w_gate, ffn_w_up, ffn_w_down, mix_norm, final_norm, mla_w_a, mla_q_norm, mla_w_qb, mla_kv_norm, mla_w_kvb, mla_w_o, ml_w_in, ml_b_if, ml_conv_w, ml_conv_b, ml_out_norm, ml_w_o, rw_mu, rw_w_r, rw_w_k, rw_w_v, rw_w0, rw_w1, rw_w2, rw_a0, rw_a1, rw_a2, rw_g1, rw_g2, rw_k_k, rw_k_a, rw_r_k, rw_ln_w, rw_ln_b, rw_w_o):
    B, S, D = x.shape
    depth = mix_norm.shape[0]
    n_mixers = 3
    h = x.reshape(B * S, D)

    def ffn(h, layer, half):
        return _ffn(h, ffn_norm[layer, half], ffn_w_gate[layer, half].astype(BF16),
                    ffn_w_up[layer, half].astype(BF16), ffn_w_down[layer, half].astype(BF16))

    for layer in range(depth):
        h = ffn(h, layer, 0)
        kind, j = layer % n_mixers, layer // n_mixers
        if kind == 0:
            h = _mla(h, positions, mix_norm[layer], mla_w_a[j], mla_q_norm[j], mla_w_qb[j],
                     mla_kv_norm[j], mla_w_kvb[j], mla_w_o[j], B, S)
        elif kind == 1:
            h = _mlstm(h, mix_norm[layer], ml_w_in[j], ml_b_if[j], ml_conv_w[j], ml_conv_b[j],
                       ml_out_norm[j], ml_w_o[j], B, S)
        else:
            h = _rwkv(h, mix_norm[layer], rw_mu[j], rw_w_r[j], rw_w_k[j], rw_w_v[j], rw_w0[j],
                      rw_w1[j], rw_w2[j], rw_a0[j], rw_a1[j], rw_a2[j], rw_g1[j], rw_g2[j],
                      rw_k_k[j], rw_k_a[j], rw_r_k[j], rw_ln_w[j], rw_ln_b[j], rw_w_o[j], B, S)
        h = ffn(h, layer, 1)
    return _final_norm(h, final_norm).reshape(B, S, D)
```

```python
import functools
import math

import jax
import jax.numpy as jnp
from jax import lax
from jax.experimental import pallas as pl
from jax.experimental.pallas import tpu as pltpu

F32 = jnp.float32
BF16 = jnp.bfloat16

RMS_EPS = 1e-6
GN_EPS = 64e-5
LANES = 128
MXU_DIM = 256
VMEM_LIMIT = 56 * 1024 * 1024

MLA_HEADS = 8
MLA_Q_RANK = 512
MLA_KV_RANK = 256
MLA_NOPE = 128
MLA_ROPE = 64
MLA_V = 128
ROPE_THETA = 10000.0
ML_HEADS = 8
ML_QK = 64
ML_V = 128
ML_CONV = 4
RW_HEAD = 64

TM_FFN = 512
TM_PROJ = 256
TQ_ATTN = 512
V_PAD = 16
ATTN_HEADS_PER_STEP = 4
ML_CHUNK = 128
RW_CHUNK = 64
RW_GROUP = MXU_DIM // RW_HEAD


def _params(n_axes):
    return pltpu.CompilerParams(dimension_semantics=("arbitrary",) * n_axes,
                                vmem_limit_bytes=VMEM_LIMIT)


def _resident(shape):
    nd = len(shape)
    return pl.BlockSpec(shape, lambda *_: (0,) * nd, pipeline_mode=pl.Buffered(1))


def _rms(x, g, eps=RMS_EPS):
    return x * lax.rsqrt(jnp.mean(x * x, axis=-1, keepdims=True) + eps) * g


def _dot(a, b):
    return jnp.dot(a.astype(BF16), b.astype(BF16), preferred_element_type=F32)


def _dot_nt(a, b):
    return lax.dot_general(a.astype(BF16), b.astype(BF16), (((1,), (1,)), ((), ())),
                           preferred_element_type=F32)


def _dot_tn(a, b):
    return lax.dot_general(a.astype(BF16), b.astype(BF16), (((0,), (0,)), ((), ())),
                           preferred_element_type=F32)


def _split3(x):
    hi = x.astype(BF16)
    r1 = x - hi.astype(F32)
    mid = r1.astype(BF16)
    lo = (r1 - mid.astype(F32)).astype(BF16)
    return hi, mid, lo


def _shift_rows(x, tail, shift):
    rolled = pltpu.roll(x, shift, 0)
    row = lax.broadcasted_iota(jnp.int32, tail.shape, 0)
    head = jnp.where(row < shift, pltpu.roll(tail, shift, 0), rolled[0:8])
    return jnp.concatenate([head, rolled[8:]], axis=0)


def _softplus(y):
    return jnp.maximum(y, 0.0) + jnp.log1p(jnp.exp(-jnp.abs(y)))


def _sigmoid(y):
    return 1.0 / (1.0 + jnp.exp(-y))


def _ffn_kernel(*refs, mixer, final):
    x_ref, g_ref, wg_ref, wu_ref, wd_ref = refs[:5]
    o_ref = refs[-1]
    x = x_ref[...]
    if mixer:
        a_ref, wo_ref = refs[5:7]
        x = x + jnp.dot(a_ref[...], wo_ref[...], preferred_element_type=F32)
    h = _rms(x, g_ref[...]).astype(BF16)
    gate = jnp.dot(h, wg_ref[...], preferred_element_type=F32)
    up = jnp.dot(h, wu_ref[...], preferred_element_type=F32)
    act = (gate * _sigmoid(gate) * up).astype(BF16)
    y = jnp.dot(act, wd_ref[...], preferred_element_type=F32)
    out = x + 0.5 * y
    if final:
        out = _rms(out, refs[-2][...])
    o_ref[...] = out


def _ffn(x, g, wg, wu, wd, mixer=None, final_g=None):
    T, D = x.shape
    F = wg.shape[1]
    tm = min(TM_FFN, T)
    row_spec = lambda n: pl.BlockSpec((tm, n), lambda i: (i, 0))
    args = [x, g.reshape(1, D), wg, wu, wd]
    in_specs = [row_spec(D), _resident((1, D)), _resident((D, F)), _resident((D, F)),
                _resident((F, D))]
    if mixer is not None:
        a, w_o = mixer
        args += [a, w_o]
        in_specs += [row_spec(a.shape[1]), _resident(w_o.shape)]
    if final_g is not None:
        args.append(final_g.reshape(1, D))
        in_specs.append(_resident((1, D)))
    return pl.pallas_call(
        functools.partial(_ffn_kernel, mixer=mixer is not None, final=final_g is not None),
        grid=(T // tm,),
        in_specs=in_specs,
        out_specs=row_spec(D),
        out_shape=jax.ShapeDtypeStruct((T, D), F32),
        compiler_params=_params(1),
        name="ffn",
    )(*args)


def _mla_proj_kernel(x_ref, g_ref, pos_ref, invf_ref, sgn_ref, wa_ref, qn_ref, wqb_ref,
                     kvn_ref, wkvb_ref, q_ref, k_ref, vt_ref, *, q_scale):
    H = MLA_HEADS
    h = _rms(x_ref[...], g_ref[...]).astype(BF16)
    lat = jnp.dot(h, wa_ref[...], preferred_element_type=F32)
    cq = _rms(lat[:, :MLA_Q_RANK], qn_ref[...]).astype(BF16)
    ckv = _rms(lat[:, MLA_Q_RANK:MLA_Q_RANK + MLA_KV_RANK], kvn_ref[...]).astype(BF16)
    q = jnp.dot(cq, wqb_ref[...], preferred_element_type=F32) * q_scale
    kv = jnp.dot(ckv, wkvb_ref[...], preferred_element_type=F32)
    ang = pos_ref[...].astype(F32) * invf_ref[...]
    cos = jnp.cos(ang)
    sin = jnp.sin(ang) * sgn_ref[...]
    kpe0 = MLA_Q_RANK + MLA_KV_RANK
    kx = lat[:, kpe0:kpe0 + LANES] * cos + lat[:, kpe0 + LANES:kpe0 + 2 * LANES] * sin
    half = lax.broadcasted_iota(jnp.int32, kx.shape, 1) // MLA_ROPE
    pe0 = H * MLA_NOPE
    sw0 = pe0 + H * MLA_ROPE
    for hd in range(H):
        p, m = divmod(hd, LANES // MLA_ROPE)
        qpe = (q[:, pe0 + p * LANES:pe0 + (p + 1) * LANES] * cos
               + q[:, sw0 + p * LANES:sw0 + (p + 1) * LANES] * sin)
        sel = half == m
        q_ref[0, hd, :, 0:LANES] = q[:, hd * MLA_NOPE:(hd + 1) * MLA_NOPE].astype(BF16)
        q_ref[0, hd, :, LANES:2 * LANES] = jnp.where(sel, qpe, 0.0).astype(BF16)
        k0 = hd * (MLA_NOPE + MLA_V)
        k_ref[0, hd, :, 0:LANES] = kv[:, k0:k0 + MLA_NOPE].astype(BF16)
        k_ref[0, hd, :, LANES:2 * LANES] = jnp.where(sel, kx, 0.0).astype(BF16)
        vt_ref[0, hd, 0:MLA_V, :] = kv[:, k0 + MLA_NOPE:k0 + MLA_NOPE + MLA_V].T.astype(BF16)
        vt_ref[0, hd, MLA_V:MLA_V + V_PAD, :] = jnp.ones((V_PAD, kv.shape[0]), BF16)


def _attn_kernel(q_ref, k_ref, vt_ref, o_ref, m_scr, acc_scr, *, tq):
    i = pl.program_id(2)
    heads = range(q_ref.shape[1])
    dv = o_ref.shape[1] // q_ref.shape[1]
    q = [q_ref[0, h] for h in heads]
    for h in heads:
        m_scr[h] = jnp.full(m_scr.shape[1:], -jnp.inf, F32)
        acc_scr[h] = jnp.zeros(acc_scr.shape[1:], F32)

    def step(j, diagonal):
        start = pl.multiple_of(j * tq, tq)
        s = [lax.dot_general(k_ref[0, h, pl.ds(start, tq), :], q[h], (((1,), (1,)), ((), ())),
                             preferred_element_type=F32) for h in heads]
        if diagonal:
            key = lax.broadcasted_iota(jnp.int32, s[0].shape, 0)
            qry = lax.broadcasted_iota(jnp.int32, s[0].shape, 1)
            s = [jnp.where(key <= qry, x, -jnp.inf) for x in s]
        m_old = [m_scr[h, 0:1, :] for h in heads]
        m_new = [jnp.maximum(m_old[h], jnp.max(s[h], axis=0, keepdims=True)) for h in heads]
        alpha = [jnp.exp2(m_old[h] - m_new[h]) for h in heads]
        p = [jnp.exp2(s[h] - m_new[h]).astype(BF16) for h in heads]
        for h in heads:
            m_scr[h, 0:1, :] = m_new[h]
            acc_scr[h] = alpha[h] * acc_scr[h] + jnp.dot(vt_ref[0, h, :, pl.ds(start, tq)], p[h],
                                                         preferred_element_type=F32)

    def pair(jj, c):
        step(2 * jj, False)
        step(2 * jj + 1, False)
        return c

    lax.fori_loop(0, i // 2, pair, 0)

    @pl.when(i % 2 == 1)
    def _():
        step(i - 1, False)

    step(i, True)
    for h in heads:
        acc = acc_scr[h]
        o_ref[:, h * dv:(h + 1) * dv] = (acc[:dv] / acc[dv:dv + 1]).T.astype(o_ref.dtype)


def _mla(x, positions, mix_g, w_a, q_norm, w_qb, kv_norm, w_kvb, w_o, B, S):
    T, D = x.shape
    H = MLA_HEADS
    kpe = w_a[:, MLA_Q_RANK + MLA_KV_RANK:]
    kpe_sw = jnp.concatenate([kpe[:, MLA_ROPE // 2:], kpe[:, :MLA_ROPE // 2]], axis=1)
    reps = LANES // MLA_ROPE
    wa2 = jnp.concatenate([w_a[:, :MLA_Q_RANK + MLA_KV_RANK], jnp.tile(kpe, (1, reps)),
                           jnp.tile(kpe_sw, (1, reps))], axis=1).astype(BF16)
    wq3 = w_qb.reshape(MLA_Q_RANK, H, MLA_NOPE + MLA_ROPE)
    q_nope = wq3[:, :, :MLA_NOPE].reshape(MLA_Q_RANK, H * MLA_NOPE)
    q_pe = wq3[:, :, MLA_NOPE:]
    q_pe_sw = jnp.concatenate([q_pe[:, :, MLA_ROPE // 2:], q_pe[:, :, :MLA_ROPE // 2]], axis=2)
    wqb2 = jnp.concatenate([q_nope, q_pe.reshape(MLA_Q_RANK, H * MLA_ROPE),
                            q_pe_sw.reshape(MLA_Q_RANK, H * MLA_ROPE)], axis=1).astype(BF16)
    inv_freq = 1.0 / (ROPE_THETA ** (jnp.arange(0, MLA_ROPE, 2, dtype=F32) / MLA_ROPE))
    invf = jnp.tile(inv_freq, LANES // (MLA_ROPE // 2)).reshape(1, LANES)
    first_half = (jnp.arange(LANES) % MLA_ROPE) < MLA_ROPE // 2
    sgn = jnp.where(first_half, -1.0, 1.0).astype(F32).reshape(1, LANES)
    q_scale = (MLA_NOPE + MLA_ROPE) ** -0.5 * math.log2(math.e)

    tm = min(TM_PROJ, S)
    nt = S // tm
    wa_cols = wa2.shape[1]
    wq_cols = wqb2.shape[1]
    kv_cols = w_kvb.shape[1]
    qk_dim = 2 * LANES
    q, k, v = pl.pallas_call(
        functools.partial(_mla_proj_kernel, q_scale=q_scale),
        grid=(B, nt),
        in_specs=[pl.BlockSpec((tm, D), lambda b, i: (b * nt + i, 0)),
                  _resident((1, D)),
                  pl.BlockSpec((tm, 1), lambda b, i: (b * nt + i, 0)),
                  _resident((1, LANES)), _resident((1, LANES)),
                  _resident((D, wa_cols)), _resident((1, MLA_Q_RANK)),
                  _resident((MLA_Q_RANK, wq_cols)), _resident((1, MLA_KV_RANK)),
                  _resident((MLA_KV_RANK, kv_cols))],
        out_specs=[pl.BlockSpec((1, H, tm, qk_dim), lambda b, i: (b, 0, i, 0)),
                   pl.BlockSpec((1, H, tm, qk_dim), lambda b, i: (b, 0, i, 0)),
                   pl.BlockSpec((1, H, MLA_V + V_PAD, tm), lambda b, i: (b, 0, 0, i))],
        out_shape=[jax.ShapeDtypeStruct((B, H, S, qk_dim), BF16),
                   jax.ShapeDtypeStruct((B, H, S, qk_dim), BF16),
                   jax.ShapeDtypeStruct((B, H, MLA_V + V_PAD, S), BF16)],
        compiler_params=_params(2),
        name="mla_proj",
    )(x, mix_g.reshape(1, D), positions.reshape(T, 1), invf, sgn, wa2, q_norm.reshape(1, -1),
      wqb2, kv_norm.reshape(1, -1), w_kvb.astype(BF16))

    tq = min(TQ_ATTN, S)
    nq = S // tq
    hp = ATTN_HEADS_PER_STEP
    o = pl.pallas_call(
        functools.partial(_attn_kernel, tq=tq),
        grid=(B, H // hp, nq),
        in_specs=[pl.BlockSpec((1, hp, tq, qk_dim), lambda b, h, i: (b, h, i, 0)),
                  pl.BlockSpec((1, hp, S, qk_dim), lambda b, h, i: (b, h, 0, 0),
                               pipeline_mode=pl.Buffered(1)),
                  pl.BlockSpec((1, hp, MLA_V + V_PAD, S), lambda b, h, i: (b, h, 0, 0),
                               pipeline_mode=pl.Buffered(1))],
        out_specs=pl.BlockSpec((tq, hp * MLA_V), lambda b, h, i: (b * nq + i, h)),
        out_shape=jax.ShapeDtypeStruct((T, H * MLA_V), BF16),
        scratch_shapes=[pltpu.VMEM((hp, 8, tq), F32), pltpu.VMEM((hp, MLA_V + V_PAD, tq), F32)],
        compiler_params=_params(3),
        name="mla_attn",
    )(q, k, v)
    return o, w_o.astype(BF16)


def _ml_proj_kernel(x_ref, g_ref, w_ref, wg_ref, bif_ref, cw_ref, cb_ref,
                    q_ref, k_ref, v_ref, o_ref, gates_ref, pbuf, *, tiles_per_seq):
    tm = x_ref.shape[0]
    QK = ML_HEADS * ML_QK
    V = ML_HEADS * ML_V
    h = _rms(x_ref[...], g_ref[...]).astype(BF16)
    p = jnp.dot(h, w_ref[...], preferred_element_type=F32)
    gates_ref[...] = jnp.dot(h, wg_ref[...], preferred_element_type=F32) + bif_ref[...]
    v_ref[...] = p[:, 2 * QK:2 * QK + V].astype(BF16)
    o_ref[...] = p[:, 2 * QK + V:2 * QK + 2 * V]

    @pl.when(pl.program_id(0) % tiles_per_seq == 0)
    def _():
        pbuf[...] = jnp.zeros(pbuf.shape, F32)

    pqk = p[:, :2 * QK]
    tail = pbuf[...]
    acc = cb_ref[...] + cw_ref[ML_CONV - 1:ML_CONV, :] * pqk
    for kk in range(ML_CONV - 1):
        acc = acc + cw_ref[kk:kk + 1, :] * _shift_rows(pqk, tail, ML_CONV - 1 - kk)
    pbuf[...] = pqk[tm - 8:tm, :]
    qk = acc * _sigmoid(acc)
    q_ref[...] = qk[:, :QK].astype(BF16)
    k_ref[...] = (qk[:, QK:] * ML_QK ** -0.5).astype(BF16)


def _ml_chunk_kernel(q_ref, k_ref, v_ref, o_ref, g_ref, on_ref, out_ref, c_scr, m_scr):
    L = q_ref.shape[0]
    H = ML_HEADS
    pair = LANES // ML_QK

    @pl.when(pl.program_id(1) == 0)
    def _():
        c_scr[...] = jnp.zeros(c_scr.shape, F32)
        m_scr[...] = jnp.zeros(m_scr.shape, F32)

    G = g_ref[...]
    GT = G.T
    lfG = -_softplus(-G)
    lfGT = -_softplus(-GT)
    row = lax.broadcasted_iota(jnp.int32, (L, L), 0)
    col = lax.broadcasted_iota(jnp.int32, (L, L), 1)
    tril = col <= row
    tril_b = tril.astype(BF16)
    triu_b = (row <= col).astype(BF16)
    fc_cols = sum(jnp.dot(tril_b, piece, preferred_element_type=F32) for piece in _split3(lfG))
    fc_rows = sum(jnp.dot(piece, triu_b, preferred_element_type=F32) for piece in _split3(lfGT))
    m_row = m_scr[...]
    lane = lax.broadcasted_iota(jnp.int32, (L, LANES), 1)
    ones_blk = (lane == 0).astype(BF16)
    lane_half = lane // ML_QK
    srow_half = lax.broadcasted_iota(jnp.int32, (pair * ML_QK, 2 * ML_V), 0) // ML_QK
    lane1 = lax.broadcasted_iota(jnp.int32, (1, LANES), 1)
    heads = range(H)
    pr = [hd // pair for hd in heads]
    mm = [hd % pair for hd in heads]
    fc_col = [fc_cols[:, H + hd:H + hd + 1] for hd in heads]
    i_col = [G[:, hd:hd + 1] for hd in heads]
    m_prev = [m_row[:, hd:hd + 1] for hd in heads]
    dmat = [jnp.where(tril, fc_col[hd] - fc_rows[H + hd:H + hd + 1, :] + GT[hd:hd + 1, :], -jnp.inf)
            for hd in heads]
    inter = [fc_col[hd] + m_prev[hd] for hd in heads]
    m_t = [jnp.maximum(inter[hd], jnp.max(dmat[hd], axis=-1, keepdims=True)) for hd in heads]
    dw = [jnp.exp(dmat[hd] - m_t[hd]) for hd in heads]
    inter_w = [jnp.exp(inter[hd] - m_t[hd]) for hd in heads]
    qp = [q_ref[:, p * LANES:(p + 1) * LANES] for p in range(H // pair)]
    kp = [k_ref[:, p * LANES:(p + 1) * LANES] for p in range(H // pair)]
    qm = [jnp.where(lane_half == mm[hd], qp[pr[hd]], jnp.zeros_like(qp[0])) for hd in heads]
    km = [jnp.where(lane_half == mm[hd], kp[pr[hd]], jnp.zeros_like(kp[0])) for hd in heads]
    s = [_dot_nt(qm[hd], kp[pr[hd]]) * dw[hd] for hd in heads]
    vh = [jnp.concatenate([v_ref[:, hd * ML_V:(hd + 1) * ML_V], ones_blk], axis=1)
          for hd in heads]
    cp = [c_scr[p] for p in range(H // pair)]
    nd = [_dot(s[hd], vh[hd]) + inter_w[hd] * _dot(qm[hd], cp[pr[hd]]) for hd in heads]
    hid = [nd[hd][:, :ML_V] / jnp.maximum(jnp.abs(nd[hd][:, ML_V:ML_V + 1]), jnp.exp(-m_t[hd]))
           for hd in heads]
    hid = [x * lax.rsqrt(jnp.mean(x * x, axis=-1, keepdims=True) + RMS_EPS) for x in hid]
    for hd in heads:
        cols = slice(hd * ML_V, (hd + 1) * ML_V)
        out_ref[:, cols] = (hid[hd] * on_ref[:, cols] * _sigmoid(o_ref[:, cols])).astype(out_ref.dtype)
    m_new = [m_t[hd][L - 1:L, :] for hd in heads]
    fc_last = [fc_col[hd][L - 1:L, :] for hd in heads]
    decay = [jnp.exp(fc_last[hd] + m_prev[hd] - m_new[hd]) for hd in heads]
    w_col = [jnp.exp(fc_last[hd] - fc_col[hd] + i_col[hd] - m_new[hd]) for hd in heads]
    upd = [_dot_tn(km[hd], w_col[hd] * vh[hd].astype(F32)) for hd in heads]
    for p in range(H // pair):
        scale = decay[p * pair]
        for k in range(1, pair):
            scale = jnp.where(srow_half == k, decay[p * pair + k], scale)
        c_scr[p] = scale * cp[p] + sum(upd[p * pair + k] for k in range(pair))
    m_out = m_row
    for hd in heads:
        m_out = jnp.where(lane1 == hd, m_new[hd], m_out)
    m_scr[...] = m_out


def _mlstm(x, mix_g, w_in, b_if, conv_w, conv_b, out_norm, w_o, B, S):
    T, D = x.shape
    H = ML_HEADS
    QK = H * ML_QK
    V = H * ML_V
    main = 2 * QK + 2 * V
    w_main = w_in[:, :main].astype(BF16)
    w_gate = jnp.pad(w_in[:, main:], ((0, 0), (0, LANES - 2 * H))).astype(BF16)
    bif = jnp.pad(b_if, (0, LANES - 2 * H)).reshape(1, LANES)
    tm = min(TM_PROJ, S)
    nt = S // tm
    row_spec = lambda n: pl.BlockSpec((tm, n), lambda i: (i, 0))
    q, k, v, o, gates = pl.pallas_call(
        functools.partial(_ml_proj_kernel, tiles_per_seq=nt),
        grid=(T // tm,),
        in_specs=[row_spec(D), _resident((1, D)), _resident((D, main)), _resident((D, LANES)),
                  _resident((1, LANES)), _resident((ML_CONV, 2 * QK)), _resident((1, 2 * QK))],
        out_specs=[row_spec(QK), row_spec(QK), row_spec(V), row_spec(V), row_spec(LANES)],
        out_shape=[jax.ShapeDtypeStruct((T, QK), BF16), jax.ShapeDtypeStruct((T, QK), BF16),
                   jax.ShapeDtypeStruct((T, V), BF16), jax.ShapeDtypeStruct((T, V), F32),
                   jax.ShapeDtypeStruct((T, LANES), F32)],
        scratch_shapes=[pltpu.VMEM((8, 2 * QK), F32)],
        compiler_params=_params(1),
        name="mlstm_proj",
    )(x, mix_g.reshape(1, D), w_main, w_gate, bif, conv_w, conv_b.reshape(1, -1))

    L = min(ML_CHUNK, S)
    nc = S // L
    chunk_spec = lambda n: pl.BlockSpec((L, n), lambda b, c: (b * nc + c, 0))
    hid = pl.pallas_call(
        _ml_chunk_kernel,
        grid=(B, nc),
        in_specs=[chunk_spec(QK), chunk_spec(QK), chunk_spec(V), chunk_spec(V), chunk_spec(LANES),
                  _resident((1, V))],
        out_specs=chunk_spec(V),
        out_shape=jax.ShapeDtypeStruct((T, V), BF16),
        scratch_shapes=[pltpu.VMEM((H * ML_QK // LANES, LANES, 2 * ML_V), F32),
                        pltpu.VMEM((1, LANES), F32)],
        compiler_params=_params(2),
        name="mlstm_chunk",
    )(q, k, v, o, gates, out_norm.reshape(1, V))
    return hid, w_o.astype(BF16)


def _rw_proj_kernel(x_ref, g_ref, mu_ref, wr_ref, wk_ref, wv_ref, w1_ref, w2_ref, a1_ref, a2_ref,
                    g1_ref, g2_ref, w0_ref, a0_ref, kk_ref, ka_ref,
                    r_out, k_out, v_out, lw_out, a_out, kkraw_out, g_out, hbuf, *, tiles_per_seq):
    tm = x_ref.shape[0]
    D = x_ref.shape[1]
    h = _rms(x_ref[...], g_ref[...])

    @pl.when(pl.program_id(0) % tiles_per_seq == 0)
    def _():
        hbuf[...] = jnp.zeros(hbuf.shape, F32)

    xx = _shift_rows(h, hbuf[...], 1) - h
    hbuf[...] = h[tm - 8:tm, :]
    mix = lambda j: (h + xx * mu_ref[j:j + 1, :]).astype(BF16)
    r = jnp.dot(mix(0), wr_ref[...], preferred_element_type=F32)
    k = jnp.dot(mix(2), wk_ref[...], preferred_element_type=F32)
    v = jnp.dot(mix(3), wv_ref[...], preferred_element_type=F32)
    zw = w0_ref[...] + _dot(jnp.tanh(jnp.dot(mix(1), w1_ref[...], preferred_element_type=F32)),
                            w2_ref[...])
    w_log = -_softplus(-zw) - 0.5
    a = _sigmoid(a0_ref[...] + _dot(jnp.dot(mix(4), a1_ref[...], preferred_element_type=F32),
                                    a2_ref[...]))
    g = _dot(_sigmoid(jnp.dot(mix(5), g1_ref[...], preferred_element_type=F32)), g2_ref[...])
    r_out[...] = r.astype(r_out.dtype)
    v_out[...] = v.astype(v_out.dtype)
    lw_out[...] = -jnp.exp(w_log)
    a_out[...] = a.astype(a_out.dtype)
    kkraw_out[...] = (k * kk_ref[...]).astype(kkraw_out.dtype)
    k_out[...] = (k * (1.0 + (a - 1.0) * ka_ref[...])).astype(k_out.dtype)
    g_out[...] = g.astype(g_out.dtype)


def _group_sum(x, ones_bd):
    hi = x.astype(BF16)
    lo = (x - hi.astype(F32)).astype(BF16)
    return (jnp.dot(hi, ones_bd, preferred_element_type=F32)
            + jnp.dot(lo, ones_bd, preferred_element_type=F32))


def _rw_chunk_kernel(r_ref, k_ref, v_ref, lw_ref, a_ref, kk_ref, g_ref, rk_ref, lnw_ref, lnb_ref,
                     out_ref, s_scr):
    nb, L = r_ref.shape[0], r_ref.shape[1]
    W = s_scr.shape[1]
    N = RW_HEAD
    nh = W // N
    ng = r_ref.shape[2] // W

    @pl.when(pl.program_id(0) == 0)
    def _():
        s_scr[...] = jnp.zeros(s_scr.shape, F32)

    er = lax.broadcasted_iota(jnp.int32, (nh * L, W), 0)
    ec = lax.broadcasted_iota(jnp.int32, (nh * L, W), 1)
    same_head = (er // L) == (ec // N)
    sr = lax.broadcasted_iota(jnp.int32, (W, W), 0)
    sc = lax.broadcasted_iota(jnp.int32, (W, W), 1)
    state_bd = (sr // N) == (sc // N)
    ones_bd = state_bd.astype(BF16)
    lt = lax.broadcasted_iota(jnp.int32, (L, W), 0)
    ls = lax.broadcasted_iota(jnp.int32, (L, W), 1) % L
    strict = ls < lt
    incl = ls <= lt
    eye = (ls == lt).astype(F32)
    tr = lax.broadcasted_iota(jnp.int32, (L, L), 0)
    tc = lax.broadcasted_iota(jnp.int32, (L, L), 1)
    tril_b = (tc <= tr).astype(BF16)
    inv_n = 1.0 / N
    zero_b = jnp.zeros((nh * L, W), BF16)

    def expand(x):
        return jnp.where(same_head, jnp.concatenate([x.astype(BF16)] * nh, axis=0), zero_b)

    groups = range(nb * ng)
    bat = [g // ng for g in groups]
    cols = [slice((g % ng) * W, (g % ng + 1) * W) for g in groups]
    load = lambda ref: [ref[bat[g], :, cols[g]].astype(F32) for g in groups]
    r, k, v, lw, kk, a_sig = (load(ref) for ref in (r_ref, k_ref, v_ref, lw_ref, kk_ref, a_ref))
    kk = [x / jnp.maximum(jnp.sqrt(_group_sum(x * x, ones_bd)), 1e-12) for x in kk]
    bv = [kk[g] * a_sig[g] for g in groups]
    cum = [sum(jnp.dot(tril_b, piece, preferred_element_type=F32) for piece in _split3(x))
           for x in lw]
    cum_last = [x[L - 1:L, :] for x in cum]
    ar = [jnp.concatenate([-jnp.exp(cum[g] - lw[g]) * kk[g], jnp.exp(cum[g]) * r[g]],
                          axis=0).astype(BF16) for g in groups]
    p_inv = [jnp.exp(-x) for x in cum]
    b_e = [expand(p_inv[g] * bv[g]) for g in groups]
    k_e = [expand(p_inv[g] * k[g]) for g in groups]
    v_e = [expand(x) for x in v]
    gb = [_dot_nt(ar[g], b_e[g]) for g in groups]
    gk = [_dot_nt(ar[g], k_e[g]) for g in groups]
    m_ab = [jnp.where(strict, x[:L], 0.0) for x in gb]
    m_rb = [jnp.where(incl, x[L:], 0.0) for x in gb]
    m_akrk = [jnp.concatenate([jnp.where(strict, x[:L], 0.0), jnp.where(incl, x[L:], 0.0)],
                              axis=0) for x in gk]

    pw = [_dot(x, expand(x)) for x in m_ab]
    tinv = [eye + x for x in m_ab]
    n_sq = int(math.log2(L)) - 1
    for it in range(n_sq):
        pw_e = [expand(x) for x in pw]
        if it + 1 < n_sq:
            both = [_dot(jnp.concatenate([pw[g], tinv[g]], axis=0), pw_e[g]) for g in groups]
            pw = [x[:L] for x in both]
            tinv = [tinv[g] + both[g][L:] for g in groups]
        else:
            tinv = [tinv[g] + _dot(tinv[g], pw_e[g]) for g in groups]

    s0 = [s_scr[g] for g in groups]
    ars = [_dot_nt(ar[g], s0[g]) for g in groups]
    mv = [_dot(m_akrk[g], v_e[g]) for g in groups]
    u = [_dot(tinv[g], expand(ars[g][:L] + mv[g][:L])) for g in groups]
    y = [ars[g][L:] + mv[g][L:] + _dot(m_rb[g], expand(u[g])) for g in groups]
    for g in groups:
        p_out = jnp.exp(cum_last[g] - cum[g])
        upd = _dot_tn(jnp.concatenate([u[g], v[g]], axis=0),
                      jnp.concatenate([p_out * bv[g], p_out * k[g]], axis=0))
        s_scr[g] = s0[g] * jnp.exp(cum_last[g]) + jnp.where(state_bd, upd, 0.0)

    mean = [_group_sum(x, ones_bd) * inv_n for x in y]
    yc = [y[g] - mean[g] for g in groups]
    var = [_group_sum(x * x, ones_bd) * inv_n for x in yc]
    for g in groups:
        c = cols[g]
        yn = yc[g] * lax.rsqrt(var[g] + GN_EPS) * lnw_ref[:, c] + lnb_ref[:, c]
        bonus = _group_sum(r[g] * k[g] * rk_ref[:, c], ones_bd) * v[g]
        out_ref[bat[g], :, c] = ((yn + bonus) * g_ref[bat[g], :, c].astype(F32)).astype(out_ref.dtype)


def _rwkv(x, mix_g, mu, w_r, w_k, w_v, w0, w1, w2, a0, a1, a2, g1, g2, k_k, k_a, r_k,
          ln_w, ln_b, w_o, B, S):
    T, D = x.shape
    pad_cols = lambda w: jnp.pad(w, ((0, 0), (0, LANES - w.shape[1]))).astype(BF16)
    pad_rows = lambda w: jnp.pad(w, ((0, LANES - w.shape[0]), (0, 0))).astype(BF16)
    vec = lambda t: t.reshape(1, D)
    mu8 = jnp.pad(mu, ((0, 8 - mu.shape[0]), (0, 0)))
    tm = min(TM_PROJ, S)
    nt = S // tm
    row_spec = pl.BlockSpec((tm, D), lambda i: (i, 0))
    outs = pl.pallas_call(
        functools.partial(_rw_proj_kernel, tiles_per_seq=nt),
        grid=(T // tm,),
        in_specs=[row_spec, _resident((1, D)), _resident((8, D)),
                  _resident((D, D)), _resident((D, D)), _resident((D, D)),
                  _resident((D, LANES)), _resident((LANES, D)),
                  _resident((D, LANES)), _resident((LANES, D)),
                  _resident((D, LANES)), _resident((LANES, D)),
                  _resident((1, D)), _resident((1, D)), _resident((1, D)), _resident((1, D))],
        out_specs=[row_spec] * 7,
        out_shape=[jax.ShapeDtypeStruct((T, D), F32 if i == 3 else BF16) for i in range(7)],
        scratch_shapes=[pltpu.VMEM((8, D), F32)],
        compiler_params=_params(1),
        name="rwkv_proj",
    )(x, vec(mix_g), mu8, w_r.astype(BF16), w_k.astype(BF16), w_v.astype(BF16),
      pad_cols(w1), pad_rows(w2), pad_cols(a1), pad_rows(a2), pad_cols(g1), pad_rows(g2),
      vec(w0), vec(a0), vec(k_k), vec(k_a))
    r, k, v, lw, a, kkraw, g = outs

    L = min(RW_CHUNK, S)
    nc = S // L
    W = RW_GROUP * RW_HEAD
    assert L == RW_HEAD
    chunk_spec = pl.BlockSpec((B, L, D), lambda c: (0, c, 0))
    seq = lambda t: t.reshape(B, S, D)
    yg = pl.pallas_call(
        _rw_chunk_kernel,
        grid=(nc,),
        in_specs=[chunk_spec] * 7 + [_resident((1, D))] * 3,
        out_specs=chunk_spec,
        out_shape=jax.ShapeDtypeStruct((B, S, D), BF16),
        scratch_shapes=[pltpu.VMEM((B * D // W, W, W), F32)],
        compiler_params=_params(1),
        name="rwkv_chunk",
    )(seq(r), seq(k), seq(v), seq(lw), seq(a), seq(kkraw), seq(g), vec(r_k), vec(ln_w), vec(ln_b))
    return yg.reshape(T, D), w_o.astype(BF16)


def kernel(x, positions, ffn_norm, ffn_w_gate, ffn_w_up, ffn_w_down, mix_norm, final_norm, mla_w_a, mla_q_norm, mla_w_qb, mla_kv_norm, mla_w_kvb, mla_w_o, ml_w_in, ml_b_if, ml_conv_w, ml_conv_b, ml_out_norm, ml_w_o, rw_mu, rw_w_r, rw_w_k, rw_w_v, rw_w0, rw_w1, rw_w2, rw_a0, rw_a1, rw_a2, rw_g1, rw_g2, rw_k_k, rw_k_a, rw_r_k, rw_ln_w, rw_ln_b, rw_w_o):
    B, S, D = x.shape
    depth = mix_norm.shape[0]
    n_mixers = 3
    h = x.reshape(B * S, D)

    def ffn(h, layer, half, **kw):
        return _ffn(h, ffn_norm[layer, half], ffn_w_gate[layer, half].astype(BF16),
                    ffn_w_up[layer, half].astype(BF16), ffn_w_down[layer, half].astype(BF16), **kw)

    for layer in range(depth):
        h = ffn(h, layer, 0)
        kind, j = layer % n_mixers, layer // n_mixers
        if kind == 0:
            mixer = _mla(h, positions, mix_norm[layer], mla_w_a[j], mla_q_norm[j], mla_w_qb[j],
                         mla_kv_norm[j], mla_w_kvb[j], mla_w_o[j], B, S)
        elif kind == 1:
            mixer = _mlstm(h, mix_norm[layer], ml_w_in[j], ml_b_if[j], ml_conv_w[j], ml_conv_b[j],
                           ml_out_norm[j], ml_w_o[j], B, S)
        else:
            mixer = _rwkv(h, mix_norm[layer], rw_mu[j], rw_w_r[j], rw_w_k[j], rw_w_v[j], rw_w0[j],
                          rw_w1[j], rw_w2[j], rw_a0[j], rw_a1[j], rw_a2[j], rw_g1[j], rw_g2[j],
                          rw_k_k[j], rw_k_a[j], rw_r_k[j], rw_ln_w[j], rw_ln_b[j], rw_w_o[j], B, S)
        h = ffn(h, layer, 1, mixer=mixer, final_g=final_norm if layer == depth - 1 else None)
    return h.reshape(B, S, D)
```

```python
import functools
import math

import jax
import jax.numpy as jnp
from jax import lax
from jax.experimental import pallas as pl
from jax.experimental.pallas import tpu as pltpu

F32 = jnp.float32
BF16 = jnp.bfloat16

RMS_EPS = 1e-6
GN_EPS = 64e-5
LANES = 128
MXU_DIM = 256
VMEM_LIMIT = 56 * 1024 * 1024

MLA_HEADS = 8
MLA_Q_RANK = 512
MLA_KV_RANK = 256
MLA_NOPE = 128
MLA_ROPE = 64
MLA_V = 128
ROPE_THETA = 10000.0
ML_HEADS = 8
ML_QK = 64
ML_V = 128
ML_CONV = 4
RW_HEAD = 64

TM_FFN = 512
TM_PROJ = 512
TQ_ATTN = 512
V_PAD = 16
ATTN_HEADS_PER_STEP = 4
ML_CHUNK = 256
RW_CHUNK = 64
RW_GROUP = MXU_DIM // RW_HEAD


def _params(n_axes):
    return pltpu.CompilerParams(dimension_semantics=("arbitrary",) * n_axes,
                                vmem_limit_bytes=VMEM_LIMIT)


def _resident(shape):
    nd = len(shape)
    return pl.BlockSpec(shape, lambda *_: (0,) * nd, pipeline_mode=pl.Buffered(1))


def _rms(x, g, eps=RMS_EPS):
    return x * lax.rsqrt(jnp.mean(x * x, axis=-1, keepdims=True) + eps) * g


def _dot(a, b):
    return jnp.dot(a.astype(BF16), b.astype(BF16), preferred_element_type=F32)


def _dot_nt(a, b):
    return lax.dot_general(a.astype(BF16), b.astype(BF16), (((1,), (1,)), ((), ())),
                           preferred_element_type=F32)


def _dot_tn(a, b):
    return lax.dot_general(a.astype(BF16), b.astype(BF16), (((0,), (0,)), ((), ())),
                           preferred_element_type=F32)


def _split3(x):
    hi = x.astype(BF16)
    r1 = x - hi.astype(F32)
    mid = r1.astype(BF16)
    lo = (r1 - mid.astype(F32)).astype(BF16)
    return hi, mid, lo


def _shift_rows(x, tail, shift):
    rolled = pltpu.roll(x, shift, 0)
    row = lax.broadcasted_iota(jnp.int32, tail.shape, 0)
    head = jnp.where(row < shift, pltpu.roll(tail, shift, 0), rolled[0:8])
    return jnp.concatenate([head, rolled[8:]], axis=0)


def _softplus(y):
    return jnp.maximum(y, 0.0) + jnp.log1p(jnp.exp(-jnp.abs(y)))


def _sigmoid(y):
    return 1.0 / (1.0 + jnp.exp(-y))


def _ffn_kernel(*refs, mixer, final):
    x_ref, g_ref, wg_ref, wu_ref, wd_ref = refs[:5]
    o_ref = refs[-1]
    x = x_ref[...]
    if mixer:
        a_ref, wo_ref = refs[5:7]
        x = x + jnp.dot(a_ref[...], wo_ref[...], preferred_element_type=F32)
    h = _rms(x, g_ref[...]).astype(BF16)
    gate = jnp.dot(h, wg_ref[...], preferred_element_type=F32)
    up = jnp.dot(h, wu_ref[...], preferred_element_type=F32)
    act = (gate * _sigmoid(gate) * up).astype(BF16)
    y = jnp.dot(act, wd_ref[...], preferred_element_type=F32)
    out = x + 0.5 * y
    if final:
        out = _rms(out, refs[-2][...])
    o_ref[...] = out


def _ffn(x, g, wg, wu, wd, mixer=None, final_g=None):
    T, D = x.shape
    F = wg.shape[1]
    tm = min(TM_FFN, T)
    row_spec = lambda n: pl.BlockSpec((tm, n), lambda i: (i, 0))
    args = [x, g.reshape(1, D), wg, wu, wd]
    in_specs = [row_spec(D), _resident((1, D)), _resident((D, F)), _resident((D, F)),
                _resident((F, D))]
    if mixer is not None:
        a, w_o = mixer
        args += [a, w_o]
        in_specs += [row_spec(a.shape[1]), _resident(w_o.shape)]
    if final_g is not None:
        args.append(final_g.reshape(1, D))
        in_specs.append(_resident((1, D)))
    return pl.pallas_call(
        functools.partial(_ffn_kernel, mixer=mixer is not None, final=final_g is not None),
        grid=(T // tm,),
        in_specs=in_specs,
        out_specs=row_spec(D),
        out_shape=jax.ShapeDtypeStruct((T, D), F32),
        compiler_params=_params(1),
        name="ffn",
    )(*args)


def _rope_tables(pos_col, invf):
    tm = pos_col.shape[0]
    nfreq = MLA_ROPE // 2
    nblk = LANES // nfreq
    rows = tm // nblk
    posb = jnp.broadcast_to(pos_col.astype(F32), (tm, LANES))
    lane_blk = lax.broadcasted_iota(jnp.int32, (rows, LANES), 1) // nfreq
    packed = posb[0:rows]
    for c in range(1, nblk):
        packed = jnp.where(lane_blk == c, posb[c * rows:(c + 1) * rows], packed)
    ang = packed * invf

    def spread(t):
        rolled = [t] + [pltpu.roll(t, nfreq * k, 1) for k in range(1, nblk)]
        blocks = []
        for c in range(nblk):
            out = rolled[(0 - c) % nblk]
            for dst in range(1, nblk):
                out = jnp.where(lane_blk == dst, rolled[(dst - c) % nblk], out)
            blocks.append(out)
        return jnp.concatenate(blocks, axis=0)

    return spread(jnp.cos(ang)), spread(jnp.sin(ang))


def _mla_proj_kernel(x_ref, g_ref, pos_ref, invf_ref, sgn_ref, wa_ref, qn_ref, wqb_ref,
                     kvn_ref, wkvb_ref, qt_ref, k_ref, vt_ref, *, q_scale):
    H = MLA_HEADS
    h = _rms(x_ref[...], g_ref[...]).astype(BF16)
    lat = jnp.dot(h, wa_ref[...], preferred_element_type=F32)
    cq = _rms(lat[:, :MLA_Q_RANK], qn_ref[...]).astype(BF16)
    ckv = _rms(lat[:, MLA_Q_RANK:MLA_Q_RANK + MLA_KV_RANK], kvn_ref[...]).astype(BF16)
    q = jnp.dot(cq, wqb_ref[...], preferred_element_type=F32) * q_scale
    kv = jnp.dot(ckv, wkvb_ref[...], preferred_element_type=F32)
    cos, sin = _rope_tables(pos_ref[...], invf_ref[...])
    sin = sin * sgn_ref[...]
    kpe0 = MLA_Q_RANK + MLA_KV_RANK
    kx = lat[:, kpe0:kpe0 + LANES] * cos + lat[:, kpe0 + LANES:kpe0 + 2 * LANES] * sin
    half = lax.broadcasted_iota(jnp.int32, kx.shape, 1) // MLA_ROPE
    pe0 = H * MLA_NOPE
    sw0 = pe0 + H * MLA_ROPE
    for hd in range(H):
        p, m = divmod(hd, LANES // MLA_ROPE)
        qpe = (q[:, pe0 + p * LANES:pe0 + (p + 1) * LANES] * cos
               + q[:, sw0 + p * LANES:sw0 + (p + 1) * LANES] * sin)
        sel = half == m
        qt_ref[0, hd, 0:LANES, :] = q[:, hd * MLA_NOPE:(hd + 1) * MLA_NOPE].T.astype(BF16)
        qt_ref[0, hd, LANES:2 * LANES, :] = jnp.where(sel, qpe, 0.0).T.astype(BF16)
        k0 = hd * (MLA_NOPE + MLA_V)
        k_ref[0, hd, :, 0:LANES] = kv[:, k0:k0 + MLA_NOPE].astype(BF16)
        k_ref[0, hd, :, LANES:2 * LANES] = jnp.where(sel, kx, 0.0).astype(BF16)
        vt_ref[0, hd, 0:MLA_V, :] = kv[:, k0 + MLA_NOPE:k0 + MLA_NOPE + MLA_V].T.astype(BF16)
        vt_ref[0, hd, MLA_V:MLA_V + V_PAD, :] = jnp.ones((V_PAD, kv.shape[0]), BF16)


def _attn_kernel(qt_ref, k_ref, vt_ref, o_ref, m_scr, acc_scr, *, tq):
    i = pl.program_id(2)
    heads = range(qt_ref.shape[1])
    dv = o_ref.shape[1] // qt_ref.shape[1]
    qt = [qt_ref[0, h] for h in heads]
    for h in heads:
        m_scr[h] = jnp.full(m_scr.shape[1:], -jnp.inf, F32)
        acc_scr[h] = jnp.zeros(acc_scr.shape[1:], F32)

    def step(j, diagonal):
        start = pl.multiple_of(j * tq, tq)
        s = [jnp.dot(k_ref[0, h, pl.ds(start, tq), :], qt[h], preferred_element_type=F32)
             for h in heads]
        if diagonal:
            key = lax.broadcasted_iota(jnp.int32, s[0].shape, 0)
            qry = lax.broadcasted_iota(jnp.int32, s[0].shape, 1)
            s = [jnp.where(key <= qry, x, -jnp.inf) for x in s]
        m_old = [m_scr[h, 0:1, :] for h in heads]
        m_new = [jnp.maximum(m_old[h], jnp.max(s[h], axis=0, keepdims=True)) for h in heads]
        alpha = [jnp.exp2(m_old[h] - m_new[h]) for h in heads]
        p = [jnp.exp2(s[h] - m_new[h]).astype(BF16) for h in heads]
        for h in heads:
            m_scr[h, 0:1, :] = m_new[h]
            acc_scr[h] = alpha[h] * acc_scr[h] + jnp.dot(vt_ref[0, h, :, pl.ds(start, tq)], p[h],
                                                         preferred_element_type=F32)

    def pair(jj, c):
        step(2 * jj, False)
        step(2 * jj + 1, False)
        return c

    lax.fori_loop(0, i // 2, pair, 0)

    @pl.when(i % 2 == 1)
    def _():
        step(i - 1, False)

    step(i, True)
    for h in heads:
        acc = acc_scr[h]
        o_ref[:, h * dv:(h + 1) * dv] = (acc[:dv] / acc[dv:dv + 1]).T.astype(o_ref.dtype)


def _mla(x, positions, mix_g, w_a, q_norm, w_qb, kv_norm, w_kvb, w_o, B, S):
    T, D = x.shape
    H = MLA_HEADS
    kpe = w_a[:, MLA_Q_RANK + MLA_KV_RANK:]
    kpe_sw = jnp.concatenate([kpe[:, MLA_ROPE // 2:], kpe[:, :MLA_ROPE // 2]], axis=1)
    reps = LANES // MLA_ROPE
    wa2 = jnp.concatenate([w_a[:, :MLA_Q_RANK + MLA_KV_RANK], jnp.tile(kpe, (1, reps)),
                           jnp.tile(kpe_sw, (1, reps))], axis=1).astype(BF16)
    wq3 = w_qb.reshape(MLA_Q_RANK, H, MLA_NOPE + MLA_ROPE)
    q_nope = wq3[:, :, :MLA_NOPE].reshape(MLA_Q_RANK, H * MLA_NOPE)
    q_pe = wq3[:, :, MLA_NOPE:]
    q_pe_sw = jnp.concatenate([q_pe[:, :, MLA_ROPE // 2:], q_pe[:, :, :MLA_ROPE // 2]], axis=2)
    wqb2 = jnp.concatenate([q_nope, q_pe.reshape(MLA_Q_RANK, H * MLA_ROPE),
                            q_pe_sw.reshape(MLA_Q_RANK, H * MLA_ROPE)], axis=1).astype(BF16)
    inv_freq = 1.0 / (ROPE_THETA ** (jnp.arange(0, MLA_ROPE, 2, dtype=F32) / MLA_ROPE))
    invf = jnp.tile(inv_freq, LANES // (MLA_ROPE // 2)).reshape(1, LANES)
    first_half = (jnp.arange(LANES) % MLA_ROPE) < MLA_ROPE // 2
    sgn = jnp.where(first_half, -1.0, 1.0).astype(F32).reshape(1, LANES)
    q_scale = (MLA_NOPE + MLA_ROPE) ** -0.5 * math.log2(math.e)

    tm = min(TM_PROJ, S)
    nt = S // tm
    wa_cols = wa2.shape[1]
    wq_cols = wqb2.shape[1]
    kv_cols = w_kvb.shape[1]
    qk_dim = 2 * LANES
    q, k, v = pl.pallas_call(
        functools.partial(_mla_proj_kernel, q_scale=q_scale),
        grid=(B, nt),
        in_specs=[pl.BlockSpec((tm, D), lambda b, i: (b * nt + i, 0)),
                  _resident((1, D)),
                  pl.BlockSpec((tm, 1), lambda b, i: (b * nt + i, 0)),
                  _resident((1, LANES)), _resident((1, LANES)),
                  _resident((D, wa_cols)), _resident((1, MLA_Q_RANK)),
                  _resident((MLA_Q_RANK, wq_cols)), _resident((1, MLA_KV_RANK)),
                  _resident((MLA_KV_RANK, kv_cols))],
        out_specs=[pl.BlockSpec((1, H, qk_dim, tm), lambda b, i: (b, 0, 0, i)),
                   pl.BlockSpec((1, H, tm, qk_dim), lambda b, i: (b, 0, i, 0)),
                   pl.BlockSpec((1, H, MLA_V + V_PAD, tm), lambda b, i: (b, 0, 0, i))],
        out_shape=[jax.ShapeDtypeStruct((B, H, qk_dim, S), BF16),
                   jax.ShapeDtypeStruct((B, H, S, qk_dim), BF16),
                   jax.ShapeDtypeStruct((B, H, MLA_V + V_PAD, S), BF16)],
        compiler_params=_params(2),
        name="mla_proj",
    )(x, mix_g.reshape(1, D), positions.reshape(T, 1), invf, sgn, wa2, q_norm.reshape(1, -1),
      wqb2, kv_norm.reshape(1, -1), w_kvb.astype(BF16))

    tq = min(TQ_ATTN, S)
    nq = S // tq
    hp = ATTN_HEADS_PER_STEP
    o = pl.pallas_call(
        functools.partial(_attn_kernel, tq=tq),
        grid=(B, H // hp, nq),
        in_specs=[pl.BlockSpec((1, hp, qk_dim, tq), lambda b, h, i: (b, h, 0, i)),
                  pl.BlockSpec((1, hp, S, qk_dim), lambda b, h, i: (b, h, 0, 0),
                               pipeline_mode=pl.Buffered(1)),
                  pl.BlockSpec((1, hp, MLA_V + V_PAD, S), lambda b, h, i: (b, h, 0, 0),
                               pipeline_mode=pl.Buffered(1))],
        out_specs=pl.BlockSpec((tq, hp * MLA_V), lambda b, h, i: (b * nq + i, h)),
        out_shape=jax.ShapeDtypeStruct((T, H * MLA_V), BF16),
        scratch_shapes=[pltpu.VMEM((hp, 8, tq), F32), pltpu.VMEM((hp, MLA_V + V_PAD, tq), F32)],
        compiler_params=_params(3),
        name="mla_attn",
    )(q, k, v)
    return o, w_o.astype(BF16)


def _ml_proj_kernel(x_ref, g_ref, w_ref, wg_ref, bif_ref, cw_ref, cb_ref,
                    q_ref, k_ref, v_ref, o_ref, gates_ref, pbuf, *, tiles_per_seq):
    tm = x_ref.shape[0]
    QK = ML_HEADS * ML_QK
    V = ML_HEADS * ML_V
    h = _rms(x_ref[...], g_ref[...]).astype(BF16)
    p = jnp.dot(h, w_ref[...], preferred_element_type=F32)
    gates_ref[...] = jnp.dot(h, wg_ref[...], preferred_element_type=F32) + bif_ref[...]
    v_ref[...] = p[:, 2 * QK:2 * QK + V].astype(BF16)
    o_ref[...] = p[:, 2 * QK + V:2 * QK + 2 * V]

    @pl.when(pl.program_id(0) % tiles_per_seq == 0)
    def _():
        pbuf[...] = jnp.zeros(pbuf.shape, F32)

    pqk = p[:, :2 * QK]
    tail = pbuf[...]
    acc = cb_ref[...] + cw_ref[ML_CONV - 1:ML_CONV, :] * pqk
    for kk in range(ML_CONV - 1):
        acc = acc + cw_ref[kk:kk + 1, :] * _shift_rows(pqk, tail, ML_CONV - 1 - kk)
    pbuf[...] = pqk[tm - 8:tm, :]
    qk = acc * _sigmoid(acc)
    q_ref[...] = qk[:, :QK].astype(BF16)
    k_ref[...] = (qk[:, QK:] * ML_QK ** -0.5).astype(BF16)


def _ml_chunk_kernel(q_ref, k_ref, v_ref, o_ref, g_ref, on_ref, out_ref, c_scr, m_scr):
    L = q_ref.shape[0]
    H = ML_HEADS
    pair = LANES // ML_QK

    @pl.when(pl.program_id(1) == 0)
    def _():
        c_scr[...] = jnp.zeros(c_scr.shape, F32)
        m_scr[...] = jnp.zeros(m_scr.shape, F32)

    G = g_ref[...]
    GT = G.T
    lfG = -_softplus(-G)
    lfGT = -_softplus(-GT)
    row = lax.broadcasted_iota(jnp.int32, (L, L), 0)
    col = lax.broadcasted_iota(jnp.int32, (L, L), 1)
    tril = col <= row
    tril_b = tril.astype(BF16)
    triu_b = (row <= col).astype(BF16)
    fc_cols = sum(jnp.dot(tril_b, piece, preferred_element_type=F32) for piece in _split3(lfG))
    fc_rows = sum(jnp.dot(piece, triu_b, preferred_element_type=F32) for piece in _split3(lfGT))
    m_row = m_scr[...]
    lane = lax.broadcasted_iota(jnp.int32, (L, LANES), 1)
    ones_blk = (lane == 0).astype(BF16)
    lane_half = lane // ML_QK
    srow_half = lax.broadcasted_iota(jnp.int32, (pair * ML_QK, 2 * ML_V), 0) // ML_QK
    lane1 = lax.broadcasted_iota(jnp.int32, (1, LANES), 1)
    heads = range(H)
    pr = [hd // pair for hd in heads]
    mm = [hd % pair for hd in heads]
    fc_col = [fc_cols[:, H + hd:H + hd + 1] for hd in heads]
    i_col = [G[:, hd:hd + 1] for hd in heads]
    m_prev = [m_row[:, hd:hd + 1] for hd in heads]
    dmat = [jnp.where(tril, fc_col[hd] - fc_rows[H + hd:H + hd + 1, :] + GT[hd:hd + 1, :], -jnp.inf)
            for hd in heads]
    inter = [fc_col[hd] + m_prev[hd] for hd in heads]
    m_t = [jnp.maximum(inter[hd], jnp.max(dmat[hd], axis=-1, keepdims=True)) for hd in heads]
    dw = [jnp.exp(dmat[hd] - m_t[hd]) for hd in heads]
    inter_w = [jnp.exp(inter[hd] - m_t[hd]) for hd in heads]
    qp = [q_ref[:, p * LANES:(p + 1) * LANES] for p in range(H // pair)]
    kp = [k_ref[:, p * LANES:(p + 1) * LANES] for p in range(H // pair)]
    qm = [jnp.where(lane_half == mm[hd], qp[pr[hd]], jnp.zeros_like(qp[0])) for hd in heads]
    km = [jnp.where(lane_half == mm[hd], kp[pr[hd]], jnp.zeros_like(kp[0])) for hd in heads]
    s = [_dot_nt(qm[hd], kp[pr[hd]]) * dw[hd] for hd in heads]
    vh = [jnp.concatenate([v_ref[:, hd * ML_V:(hd + 1) * ML_V], ones_blk], axis=1)
          for hd in heads]
    cp = [c_scr[p] for p in range(H // pair)]
    nd = [_dot(s[hd], vh[hd]) + inter_w[hd] * _dot(qm[hd], cp[pr[hd]]) for hd in heads]
    hid = [nd[hd][:, :ML_V] / jnp.maximum(jnp.abs(nd[hd][:, ML_V:ML_V + 1]), jnp.exp(-m_t[hd]))
           for hd in heads]
    hid = [x * lax.rsqrt(jnp.mean(x * x, axis=-1, keepdims=True) + RMS_EPS) for x in hid]
    for hd in heads:
        cols = slice(hd * ML_V, (hd + 1) * ML_V)
        out_ref[:, cols] = (hid[hd] * on_ref[:, cols] * _sigmoid(o_ref[:, cols])).astype(out_ref.dtype)
    m_new = [m_t[hd][L - 1:L, :] for hd in heads]
    fc_last = [fc_col[hd][L - 1:L, :] for hd in heads]
    decay = [jnp.exp(fc_last[hd] + m_prev[hd] - m_new[hd]) for hd in heads]
    w_col = [jnp.exp(fc_last[hd] - fc_col[hd] + i_col[hd] - m_new[hd]) for hd in heads]
    upd = [_dot_tn(km[hd], w_col[hd] * vh[hd].astype(F32)) for hd in heads]
    for p in range(H // pair):
        scale = decay[p * pair]
        for k in range(1, pair):
            scale = jnp.where(srow_half == k, decay[p * pair + k], scale)
        c_scr[p] = scale * cp[p] + sum(upd[p * pair + k] for k in range(pair))
    m_out = m_row
    for hd in heads:
        m_out = jnp.where(lane1 == hd, m_new[hd], m_out)
    m_scr[...] = m_out


def _mlstm(x, mix_g, w_in, b_if, conv_w, conv_b, out_norm, w_o, B, S):
    T, D = x.shape
    H = ML_HEADS
    QK = H * ML_QK
    V = H * ML_V
    main = 2 * QK + 2 * V
    w_main = w_in[:, :main].astype(BF16)
    w_gate = jnp.pad(w_in[:, main:], ((0, 0), (0, LANES - 2 * H))).astype(BF16)
    bif = jnp.pad(b_if, (0, LANES - 2 * H)).reshape(1, LANES)
    tm = min(TM_PROJ, S)
    nt = S // tm
    row_spec = lambda n: pl.BlockSpec((tm, n), lambda i: (i, 0))
    q, k, v, o, gates = pl.pallas_call(
        functools.partial(_ml_proj_kernel, tiles_per_seq=nt),
        grid=(T // tm,),
        in_specs=[row_spec(D), _resident((1, D)), _resident((D, main)), _resident((D, LANES)),
                  _resident((1, LANES)), _resident((ML_CONV, 2 * QK)), _resident((1, 2 * QK))],
        out_specs=[row_spec(QK), row_spec(QK), row_spec(V), row_spec(V), row_spec(LANES)],
        out_shape=[jax.ShapeDtypeStruct((T, QK), BF16), jax.ShapeDtypeStruct((T, QK), BF16),
                   jax.ShapeDtypeStruct((T, V), BF16), jax.ShapeDtypeStruct((T, V), F32),
                   jax.ShapeDtypeStruct((T, LANES), F32)],
        scratch_shapes=[pltpu.VMEM((8, 2 * QK), F32)],
        compiler_params=_params(1),
        name="mlstm_proj",
    )(x, mix_g.reshape(1, D), w_main, w_gate, bif, conv_w, conv_b.reshape(1, -1))

    L = min(ML_CHUNK, S)
    nc = S // L
    chunk_spec = lambda n: pl.BlockSpec((L, n), lambda b, c: (b * nc + c, 0))
    hid = pl.pallas_call(
        _ml_chunk_kernel,
        grid=(B, nc),
        in_specs=[chunk_spec(QK), chunk_spec(QK), chunk_spec(V), chunk_spec(V), chunk_spec(LANES),
                  _resident((1, V))],
        out_specs=chunk_spec(V),
        out_shape=jax.ShapeDtypeStruct((T, V), BF16),
        scratch_shapes=[pltpu.VMEM((H * ML_QK // LANES, LANES, 2 * ML_V), F32),
                        pltpu.VMEM((1, LANES), F32)],
        compiler_params=_params(2),
        name="mlstm_chunk",
    )(q, k, v, o, gates, out_norm.reshape(1, V))
    return hid, w_o.astype(BF16)


def _rw_proj_kernel(x_ref, g_ref, mu_ref, wr_ref, wk_ref, wv_ref, w1_ref, w2_ref, a1_ref, a2_ref,
                    g1_ref, g2_ref, w0_ref, a0_ref, kk_ref, ka_ref,
                    r_out, k_out, v_out, lw_out, a_out, kkraw_out, g_out, hbuf, *, tiles_per_seq):
    tm = x_ref.shape[0]
    D = x_ref.shape[1]
    h = _rms(x_ref[...], g_ref[...])

    @pl.when(pl.program_id(0) % tiles_per_seq == 0)
    def _():
        hbuf[...] = jnp.zeros(hbuf.shape, F32)

    xx = _shift_rows(h, hbuf[...], 1) - h
    hbuf[...] = h[tm - 8:tm, :]
    mix = lambda j: (h + xx * mu_ref[j:j + 1, :]).astype(BF16)
    r = jnp.dot(mix(0), wr_ref[...], preferred_element_type=F32)
    k = jnp.dot(mix(2), wk_ref[...], preferred_element_type=F32)
    v = jnp.dot(mix(3), wv_ref[...], preferred_element_type=F32)
    zw = w0_ref[...] + _dot(jnp.tanh(jnp.dot(mix(1), w1_ref[...], preferred_element_type=F32)),
                            w2_ref[...])
    w_log = -_softplus(-zw) - 0.5
    a = _sigmoid(a0_ref[...] + _dot(jnp.dot(mix(4), a1_ref[...], preferred_element_type=F32),
                                    a2_ref[...]))
    g = _dot(_sigmoid(jnp.dot(mix(5), g1_ref[...], preferred_element_type=F32)), g2_ref[...])
    r_out[...] = r.astype(r_out.dtype)
    v_out[...] = v.astype(v_out.dtype)
    lw_out[...] = -jnp.exp(w_log)
    a_out[...] = a.astype(a_out.dtype)
    kkraw_out[...] = (k * kk_ref[...]).astype(kkraw_out.dtype)
    k_out[...] = (k * (1.0 + (a - 1.0) * ka_ref[...])).astype(k_out.dtype)
    g_out[...] = g.astype(g_out.dtype)


def _group_sum(x, ones_bd):
    n = x.shape[0]
    hi = x.astype(BF16)
    lo = (x - hi.astype(F32)).astype(BF16)
    both = jnp.dot(jnp.concatenate([hi, lo], axis=0), ones_bd, preferred_element_type=F32)
    return both[:n] + both[n:]


def _rw_chunk_kernel(r_ref, k_ref, v_ref, lw_ref, a_ref, kk_ref, g_ref, rk_ref, lnw_ref, lnb_ref,
                     out_ref, s_scr):
    nb, L = r_ref.shape[0], r_ref.shape[1]
    W = s_scr.shape[1]
    N = RW_HEAD
    nh = W // N
    ng = r_ref.shape[2] // W

    @pl.when(pl.program_id(0) == 0)
    def _():
        s_scr[...] = jnp.zeros(s_scr.shape, F32)

    er = lax.broadcasted_iota(jnp.int32, (nh * L, W), 0)
    ec = lax.broadcasted_iota(jnp.int32, (nh * L, W), 1)
    same_head = (er // L) == (ec // N)
    sr = lax.broadcasted_iota(jnp.int32, (W, W), 0)
    sc = lax.broadcasted_iota(jnp.int32, (W, W), 1)
    state_bd = (sr // N) == (sc // N)
    ones_bd = state_bd.astype(BF16)
    lt = lax.broadcasted_iota(jnp.int32, (L, W), 0)
    ls = lax.broadcasted_iota(jnp.int32, (L, W), 1) % L
    strict = ls < lt
    incl = ls <= lt
    eye = (ls == lt).astype(F32)
    tr = lax.broadcasted_iota(jnp.int32, (L, L), 0)
    tc = lax.broadcasted_iota(jnp.int32, (L, L), 1)
    tril_b = (tc <= tr).astype(BF16)
    inv_n = 1.0 / N
    zero_b = jnp.zeros((nh * L, W), BF16)

    def expand(x):
        return jnp.where(same_head, jnp.concatenate([x.astype(BF16)] * nh, axis=0), zero_b)

    groups = range(nb * ng)
    bat = [g // ng for g in groups]
    cols = [slice((g % ng) * W, (g % ng + 1) * W) for g in groups]
    load = lambda ref: [ref[bat[g], :, cols[g]].astype(F32) for g in groups]
    r, k, v, lw, kk, a_sig = (load(ref) for ref in (r_ref, k_ref, v_ref, lw_ref, kk_ref, a_ref))
    sums = [_group_sum(jnp.concatenate([kk[g] * kk[g], r[g] * k[g] * rk_ref[:, cols[g]]], axis=0),
                       ones_bd) for g in groups]
    kk = [kk[g] / jnp.maximum(jnp.sqrt(sums[g][:L]), 1e-12) for g in groups]
    bonus = [sums[g][L:] * v[g] for g in groups]
    bv = [kk[g] * a_sig[g] for g in groups]
    cum = [sum(jnp.dot(tril_b, piece, preferred_element_type=F32) for piece in _split3(x))
           for x in lw]
    cum_last = [x[L - 1:L, :] for x in cum]
    ar = [jnp.concatenate([-jnp.exp(cum[g] - lw[g]) * kk[g], jnp.exp(cum[g]) * r[g]],
                          axis=0).astype(BF16) for g in groups]
    p_inv = [jnp.exp(-x) for x in cum]
    b_e = [expand(p_inv[g] * bv[g]) for g in groups]
    k_e = [expand(p_inv[g] * k[g]) for g in groups]
    v_e = [expand(x) for x in v]
    gb = [_dot_nt(ar[g], b_e[g]) for g in groups]
    gk = [_dot_nt(ar[g], k_e[g]) for g in groups]
    m_ab = [jnp.where(strict, x[:L], 0.0) for x in gb]
    m_rb = [jnp.where(incl, x[L:], 0.0) for x in gb]
    m_akrk = [jnp.concatenate([jnp.where(strict, x[:L], 0.0), jnp.where(incl, x[L:], 0.0)],
                              axis=0) for x in gk]

    pw = [_dot(x, expand(x)) for x in m_ab]
    tinv = [eye + x for x in m_ab]
    n_sq = int(math.log2(L)) - 1
    for it in range(n_sq):
        pw_e = [expand(x) for x in pw]
        if it + 1 < n_sq:
            both = [_dot(jnp.concatenate([pw[g], tinv[g]], axis=0), pw_e[g]) for g in groups]
            pw = [x[:L] for x in both]
            tinv = [tinv[g] + both[g][L:] for g in groups]
        else:
            tinv = [tinv[g] + _dot(tinv[g], pw_e[g]) for g in groups]

    s0 = [s_scr[g] for g in groups]
    ars = [_dot_nt(ar[g], s0[g]) for g in groups]
    mv = [_dot(m_akrk[g], v_e[g]) for g in groups]
    u = [_dot(tinv[g], expand(ars[g][:L] + mv[g][:L])) for g in groups]
    y = [ars[g][L:] + mv[g][L:] + _dot(m_rb[g], expand(u[g])) for g in groups]
    for g in groups:
        p_out = jnp.exp(cum_last[g] - cum[g])
        upd = _dot_tn(jnp.concatenate([u[g], v[g]], axis=0),
                      jnp.concatenate([p_out * bv[g], p_out * k[g]], axis=0))
        s_scr[g] = s0[g] * jnp.exp(cum_last[g]) + jnp.where(state_bd, upd, 0.0)

    mean = [_group_sum(x, ones_bd) * inv_n for x in y]
    yc = [y[g] - mean[g] for g in groups]
    var = [_group_sum(x * x, ones_bd) * inv_n for x in yc]
    for g in groups:
        c = cols[g]
        yn = yc[g] * lax.rsqrt(var[g] + GN_EPS) * lnw_ref[:, c] + lnb_ref[:, c]
        out_ref[bat[g], :, c] = ((yn + bonus[g]) * g_ref[bat[g], :, c].astype(F32)).astype(out_ref.dtype)


def _rwkv(x, mix_g, mu, w_r, w_k, w_v, w0, w1, w2, a0, a1, a2, g1, g2, k_k, k_a, r_k,
          ln_w, ln_b, w_o, B, S):
    T, D = x.shape
    pad_cols = lambda w: jnp.pad(w, ((0, 0), (0, LANES - w.shape[1]))).astype(BF16)
    pad_rows = lambda w: jnp.pad(w, ((0, LANES - w.shape[0]), (0, 0))).astype(BF16)
    vec = lambda t: t.reshape(1, D)
    mu8 = jnp.pad(mu, ((0, 8 - mu.shape[0]), (0, 0)))
    tm = min(TM_PROJ, S)
    nt = S // tm
    row_spec = pl.BlockSpec((tm, D), lambda i: (i, 0))
    outs = pl.pallas_call(
        functools.partial(_rw_proj_kernel, tiles_per_seq=nt),
        grid=(T // tm,),
        in_specs=[row_spec, _resident((1, D)), _resident((8, D)),
                  _resident((D, D)), _resident((D, D)), _resident((D, D)),
                  _resident((D, LANES)), _resident((LANES, D)),
                  _resident((D, LANES)), _resident((LANES, D)),
                  _resident((D, LANES)), _resident((LANES, D)),
                  _resident((1, D)), _resident((1, D)), _resident((1, D)), _resident((1, D))],
        out_specs=[row_spec] * 7,
        out_shape=[jax.ShapeDtypeStruct((T, D), F32 if i == 3 else BF16) for i in range(7)],
        scratch_shapes=[pltpu.VMEM((8, D), F32)],
        compiler_params=_params(1),
        name="rwkv_proj",
    )(x, vec(mix_g), mu8, w_r.astype(BF16), w_k.astype(BF16), w_v.astype(BF16),
      pad_cols(w1), pad_rows(w2), pad_cols(a1), pad_rows(a2), pad_cols(g1), pad_rows(g2),
      vec(w0), vec(a0), vec(k_k), vec(k_a))
    r, k, v, lw, a, kkraw, g = outs

    L = min(RW_CHUNK, S)
    nc = S // L
    W = RW_GROUP * RW_HEAD
    assert L == RW_HEAD
    chunk_spec = pl.BlockSpec((B, L, D), lambda c: (0, c, 0))
    seq = lambda t: t.reshape(B, S, D)
    yg = pl.pallas_call(
        _rw_chunk_kernel,
        grid=(nc,),
        in_specs=[chunk_spec] * 7 + [_resident((1, D))] * 3,
        out_specs=chunk_spec,
        out_shape=jax.ShapeDtypeStruct((B, S, D), BF16),
        scratch_shapes=[pltpu.VMEM((B * D // W, W, W), F32)],
        compiler_params=_params(1),
        name="rwkv_chunk",
    )(seq(r), seq(k), seq(v), seq(lw), seq(a), seq(kkraw), seq(g), vec(r_k), vec(ln_w), vec(ln_b))
    return yg.reshape(T, D), w_o.astype(BF16)


def kernel(x, positions, ffn_norm, ffn_w_gate, ffn_w_up, ffn_w_down, mix_norm, final_norm, mla_w_a, mla_q_norm, mla_w_qb, mla_kv_norm, mla_w_kvb, mla_w_o, ml_w_in, ml_b_if, ml_conv_w, ml_conv_b, ml_out_norm, ml_w_o, rw_mu, rw_w_r, rw_w_k, rw_w_v, rw_w0, rw_w1, rw_w2, rw_a0, rw_a1, rw_a2, rw_g1, rw_g2, rw_k_k, rw_k_a, rw_r_k, rw_ln_w, rw_ln_b, rw_w_o):
    B, S, D = x.shape
    depth = mix_norm.shape[0]
    n_mixers = 3
    h = x.reshape(B * S, D)

    def ffn(h, layer, half, **kw):
        return _ffn(h, ffn_norm[layer, half], ffn_w_gate[layer, half].astype(BF16),
                    ffn_w_up[layer, half].astype(BF16), ffn_w_down[layer, half].astype(BF16), **kw)

    for layer in range(depth):
        h = ffn(h, layer, 0)
        kind, j = layer % n_mixers, layer // n_mixers
        if kind == 0:
            mixer = _mla(h, positions, mix_norm[layer], mla_w_a[j], mla_q_norm[j], mla_w_qb[j],
                         mla_kv_norm[j], mla_w_kvb[j], mla_w_o[j], B, S)
        elif kind == 1:
            mixer = _mlstm(h, mix_norm[layer], ml_w_in[j], ml_b_if[j], ml_conv_w[j], ml_conv_b[j],
                           ml_out_norm[j], ml_w_o[j], B, S)
        else:
            mixer = _rwkv(h, mix_norm[layer], rw_mu[j], rw_w_r[j], rw_w_k[j], rw_w_v[j], rw_w0[j],
                          rw_w1[j], rw_w2[j], rw_a0[j], rw_a1[j], rw_a2[j], rw_g1[j], rw_g2[j],
                          rw_k_k[j], rw_k_a[j], rw_r_k[j], rw_ln_w[j], rw_ln_b[j], rw_w_o[j], B, S)
        h = ffn(h, layer, 1, mixer=mixer, final_g=final_norm if layer == depth - 1 else None)
    return h.reshape(B, S, D)
```

```python
import functools
import math

import jax
import jax.numpy as jnp
from jax import lax
from jax.experimental import pallas as pl
from jax.experimental.pallas import tpu as pltpu

F32 = jnp.float32
BF16 = jnp.bfloat16

RMS_EPS = 1e-6
GN_EPS = 64e-5
LANES = 128
MXU_DIM = 256
VMEM_LIMIT = 56 * 1024 * 1024

MLA_HEADS = 8
MLA_Q_RANK = 512
MLA_KV_RANK = 256
MLA_NOPE = 128
MLA_ROPE = 64
MLA_V = 128
ROPE_THETA = 10000.0
ML_HEADS = 8
ML_QK = 64
ML_V = 128
ML_CONV = 4
RW_HEAD = 64

TM_FFN = 512
TM_PROJ = 512
TQ_ATTN = 512
V_PAD = 16
ATTN_HEADS_PER_STEP = 4
ML_CHUNK = 256
RW_CHUNK = 64
RW_GROUP = MXU_DIM // RW_HEAD


def _params(n_axes):
    return pltpu.CompilerParams(dimension_semantics=("arbitrary",) * n_axes,
                                vmem_limit_bytes=VMEM_LIMIT)


def _resident(shape):
    nd = len(shape)
    return pl.BlockSpec(shape, lambda *_: (0,) * nd, pipeline_mode=pl.Buffered(1))


def _rms(x, g, eps=RMS_EPS):
    return x * lax.rsqrt(jnp.mean(x * x, axis=-1, keepdims=True) + eps) * g


def _dot(a, b):
    return jnp.dot(a.astype(BF16), b.astype(BF16), preferred_element_type=F32)


def _dot_nt(a, b):
    return lax.dot_general(a.astype(BF16), b.astype(BF16), (((1,), (1,)), ((), ())),
                           preferred_element_type=F32)


def _dot_tn(a, b):
    return lax.dot_general(a.astype(BF16), b.astype(BF16), (((0,), (0,)), ((), ())),
                           preferred_element_type=F32)


def _split3(x):
    hi = x.astype(BF16)
    r1 = x - hi.astype(F32)
    mid = r1.astype(BF16)
    lo = (r1 - mid.astype(F32)).astype(BF16)
    return hi, mid, lo


def _shift_rows(x, tail, shift):
    rolled = pltpu.roll(x, shift, 0)
    row = lax.broadcasted_iota(jnp.int32, tail.shape, 0)
    head = jnp.where(row < shift, pltpu.roll(tail, shift, 0), rolled[0:8])
    return jnp.concatenate([head, rolled[8:]], axis=0)


def _softplus(y):
    return jnp.maximum(y, 0.0) + jnp.log1p(jnp.exp(-jnp.abs(y)))


def _sigmoid(y):
    return 1.0 / (1.0 + jnp.exp(-y))


def _ffn_kernel(*refs, mixer, final):
    x_ref, g_ref, wg_ref, wu_ref, wd_ref = refs[:5]
    o_ref = refs[-1]
    x = x_ref[...]
    if mixer:
        a_ref, wo_ref = refs[5:7]
        x = x + jnp.dot(a_ref[...], wo_ref[...], preferred_element_type=F32)
    h = _rms(x, g_ref[...]).astype(BF16)
    gate = jnp.dot(h, wg_ref[...], preferred_element_type=F32)
    up = jnp.dot(h, wu_ref[...], preferred_element_type=F32)
    act = (gate * _sigmoid(gate) * up).astype(BF16)
    y = jnp.dot(act, wd_ref[...], preferred_element_type=F32)
    out = x + 0.5 * y
    if final:
        out = _rms(out, refs[-2][...])
    o_ref[...] = out


def _ffn(x, g, wg, wu, wd, index, mixer=None, final_g=None):
    T, D = x.shape
    F = wg.shape[-1]
    tm = min(TM_FFN, T)
    row_spec = lambda n: pl.BlockSpec((tm, n), lambda i: (i, 0))
    picked = lambda r, c: pl.BlockSpec((None, None, r, c), lambda i: (*index, 0, 0),
                                       pipeline_mode=pl.Buffered(1))
    args = [x, g.reshape(1, D), wg, wu, wd]
    in_specs = [row_spec(D), _resident((1, D)), picked(D, F), picked(D, F), picked(F, D)]
    if mixer is not None:
        a, w_o = mixer
        args += [a, w_o]
        in_specs += [row_spec(a.shape[1]), _resident(w_o.shape)]
    if final_g is not None:
        args.append(final_g.reshape(1, D))
        in_specs.append(_resident((1, D)))
    return pl.pallas_call(
        functools.partial(_ffn_kernel, mixer=mixer is not None, final=final_g is not None),
        grid=(T // tm,),
        in_specs=in_specs,
        out_specs=row_spec(D),
        out_shape=jax.ShapeDtypeStruct((T, D), F32),
        compiler_params=_params(1),
        name="ffn",
    )(*args)


def _rope_tables(pos_col, invf):
    tm = pos_col.shape[0]
    nfreq = MLA_ROPE // 2
    nblk = LANES // nfreq
    rows = tm // nblk
    posb = jnp.broadcast_to(pos_col.astype(F32), (tm, LANES))
    lane_blk = lax.broadcasted_iota(jnp.int32, (rows, LANES), 1) // nfreq
    packed = posb[0:rows]
    for c in range(1, nblk):
        packed = jnp.where(lane_blk == c, posb[c * rows:(c + 1) * rows], packed)
    ang = packed * invf

    def spread(t):
        rolled = [t] + [pltpu.roll(t, nfreq * k, 1) for k in range(1, nblk)]
        blocks = []
        for c in range(nblk):
            out = rolled[(0 - c) % nblk]
            for dst in range(1, nblk):
                out = jnp.where(lane_blk == dst, rolled[(dst - c) % nblk], out)
            blocks.append(out)
        return jnp.concatenate(blocks, axis=0)

    return spread(jnp.cos(ang)), spread(jnp.sin(ang))


def _mla_proj_kernel(x_ref, g_ref, pos_ref, invf_ref, sgn_ref, wa_ref, qn_ref, wqb_ref,
                     kvn_ref, wkvb_ref, qt_ref, k_ref, vt_ref, *, q_scale):
    H = MLA_HEADS
    h = _rms(x_ref[...], g_ref[...]).astype(BF16)
    lat = jnp.dot(h, wa_ref[...], preferred_element_type=F32)
    cq = _rms(lat[:, :MLA_Q_RANK], qn_ref[...]).astype(BF16)
    ckv = _rms(lat[:, MLA_Q_RANK:MLA_Q_RANK + MLA_KV_RANK], kvn_ref[...]).astype(BF16)
    q = jnp.dot(cq, wqb_ref[...], preferred_element_type=F32) * q_scale
    kv = jnp.dot(ckv, wkvb_ref[...], preferred_element_type=F32)
    cos, sin = _rope_tables(pos_ref[...], invf_ref[...])
    sin = sin * sgn_ref[...]
    kpe0 = MLA_Q_RANK + MLA_KV_RANK
    kx = lat[:, kpe0:kpe0 + LANES] * cos + lat[:, kpe0 + LANES:kpe0 + 2 * LANES] * sin
    half = lax.broadcasted_iota(jnp.int32, kx.shape, 1) // MLA_ROPE
    pe0 = H * MLA_NOPE
    sw0 = pe0 + H * MLA_ROPE
    for hd in range(H):
        p, m = divmod(hd, LANES // MLA_ROPE)
        qpe = (q[:, pe0 + p * LANES:pe0 + (p + 1) * LANES] * cos
               + q[:, sw0 + p * LANES:sw0 + (p + 1) * LANES] * sin)
        sel = half == m
        qt_ref[0, hd, 0:LANES, :] = q[:, hd * MLA_NOPE:(hd + 1) * MLA_NOPE].T.astype(BF16)
        qt_ref[0, hd, LANES:2 * LANES, :] = jnp.where(sel, qpe, 0.0).T.astype(BF16)
        k0 = hd * (MLA_NOPE + MLA_V)
        k_ref[0, hd, :, 0:LANES] = kv[:, k0:k0 + MLA_NOPE].astype(BF16)
        k_ref[0, hd, :, LANES:2 * LANES] = jnp.where(sel, kx, 0.0).astype(BF16)
        vt_ref[0, hd, 0:MLA_V, :] = kv[:, k0 + MLA_NOPE:k0 + MLA_NOPE + MLA_V].T.astype(BF16)
        vt_ref[0, hd, MLA_V:MLA_V + V_PAD, :] = jnp.ones((V_PAD, kv.shape[0]), BF16)


def _attn_kernel(qt_ref, k_ref, vt_ref, o_ref, m_scr, acc_scr, *, tq):
    i = pl.program_id(2)
    heads = range(qt_ref.shape[1])
    dv = o_ref.shape[1] // qt_ref.shape[1]
    qt = [qt_ref[0, h] for h in heads]
    for h in heads:
        m_scr[h] = jnp.full(m_scr.shape[1:], -jnp.inf, F32)
        acc_scr[h] = jnp.zeros(acc_scr.shape[1:], F32)

    def step(j, diagonal):
        start = pl.multiple_of(j * tq, tq)
        s = [jnp.dot(k_ref[0, h, pl.ds(start, tq), :], qt[h], preferred_element_type=F32)
             for h in heads]
        if diagonal:
            key = lax.broadcasted_iota(jnp.int32, s[0].shape, 0)
            qry = lax.broadcasted_iota(jnp.int32, s[0].shape, 1)
            s = [jnp.where(key <= qry, x, -jnp.inf) for x in s]
        m_old = [m_scr[h, 0:1, :] for h in heads]
        m_new = [jnp.maximum(m_old[h], jnp.max(s[h], axis=0, keepdims=True)) for h in heads]
        alpha = [jnp.exp2(m_old[h] - m_new[h]) for h in heads]
        p = [jnp.exp2(s[h] - m_new[h]).astype(BF16) for h in heads]
        for h in heads:
            m_scr[h, 0:1, :] = m_new[h]
            acc_scr[h] = alpha[h] * acc_scr[h] + jnp.dot(vt_ref[0, h, :, pl.ds(start, tq)], p[h],
                                                         preferred_element_type=F32)

    def pair(jj, c):
        step(2 * jj, False)
        step(2 * jj + 1, False)
        return c

    lax.fori_loop(0, i // 2, pair, 0)

    @pl.when(i % 2 == 1)
    def _():
        step(i - 1, False)

    step(i, True)
    for h in heads:
        acc = acc_scr[h]
        o_ref[:, h * dv:(h + 1) * dv] = (acc[:dv] / acc[dv:dv + 1]).T.astype(o_ref.dtype)


def _mla(x, positions, mix_g, w_a, q_norm, w_qb, kv_norm, w_kvb, w_o, B, S):
    T, D = x.shape
    H = MLA_HEADS
    kpe = w_a[:, MLA_Q_RANK + MLA_KV_RANK:]
    kpe_sw = jnp.concatenate([kpe[:, MLA_ROPE // 2:], kpe[:, :MLA_ROPE // 2]], axis=1)
    reps = LANES // MLA_ROPE
    wa2 = jnp.concatenate([w_a[:, :MLA_Q_RANK + MLA_KV_RANK], jnp.tile(kpe, (1, reps)),
                           jnp.tile(kpe_sw, (1, reps))], axis=1).astype(BF16)
    wq3 = w_qb.reshape(MLA_Q_RANK, H, MLA_NOPE + MLA_ROPE)
    q_nope = wq3[:, :, :MLA_NOPE].reshape(MLA_Q_RANK, H * MLA_NOPE)
    q_pe = wq3[:, :, MLA_NOPE:]
    q_pe_sw = jnp.concatenate([q_pe[:, :, MLA_ROPE // 2:], q_pe[:, :, :MLA_ROPE // 2]], axis=2)
    wqb2 = jnp.concatenate([q_nope, q_pe.reshape(MLA_Q_RANK, H * MLA_ROPE),
                            q_pe_sw.reshape(MLA_Q_RANK, H * MLA_ROPE)], axis=1).astype(BF16)
    inv_freq = 1.0 / (ROPE_THETA ** (jnp.arange(0, MLA_ROPE, 2, dtype=F32) / MLA_ROPE))
    invf = jnp.tile(inv_freq, LANES // (MLA_ROPE // 2)).reshape(1, LANES)
    first_half = (jnp.arange(LANES) % MLA_ROPE) < MLA_ROPE // 2
    sgn = jnp.where(first_half, -1.0, 1.0).astype(F32).reshape(1, LANES)
    q_scale = (MLA_NOPE + MLA_ROPE) ** -0.5 * math.log2(math.e)

    tm = min(TM_PROJ, S)
    nt = S // tm
    wa_cols = wa2.shape[1]
    wq_cols = wqb2.shape[1]
    kv_cols = w_kvb.shape[1]
    qk_dim = 2 * LANES
    q, k, v = pl.pallas_call(
        functools.partial(_mla_proj_kernel, q_scale=q_scale),
        grid=(B, nt),
        in_specs=[pl.BlockSpec((tm, D), lambda b, i: (b * nt + i, 0)),
                  _resident((1, D)),
                  pl.BlockSpec((tm, 1), lambda b, i: (b * nt + i, 0)),
                  _resident((1, LANES)), _resident((1, LANES)),
                  _resident((D, wa_cols)), _resident((1, MLA_Q_RANK)),
                  _resident((MLA_Q_RANK, wq_cols)), _resident((1, MLA_KV_RANK)),
                  _resident((MLA_KV_RANK, kv_cols))],
        out_specs=[pl.BlockSpec((1, H, qk_dim, tm), lambda b, i: (b, 0, 0, i)),
                   pl.BlockSpec((1, H, tm, qk_dim), lambda b, i: (b, 0, i, 0)),
                   pl.BlockSpec((1, H, MLA_V + V_PAD, tm), lambda b, i: (b, 0, 0, i))],
        out_shape=[jax.ShapeDtypeStruct((B, H, qk_dim, S), BF16),
                   jax.ShapeDtypeStruct((B, H, S, qk_dim), BF16),
                   jax.ShapeDtypeStruct((B, H, MLA_V + V_PAD, S), BF16)],
        compiler_params=_params(2),
        name="mla_proj",
    )(x, mix_g.reshape(1, D), positions.reshape(T, 1), invf, sgn, wa2, q_norm.reshape(1, -1),
      wqb2, kv_norm.reshape(1, -1), w_kvb.astype(BF16))

    tq = min(TQ_ATTN, S)
    nq = S // tq
    hp = ATTN_HEADS_PER_STEP
    o = pl.pallas_call(
        functools.partial(_attn_kernel, tq=tq),
        grid=(B, H // hp, nq),
        in_specs=[pl.BlockSpec((1, hp, qk_dim, tq), lambda b, h, i: (b, h, 0, i)),
                  pl.BlockSpec((1, hp, S, qk_dim), lambda b, h, i: (b, h, 0, 0),
                               pipeline_mode=pl.Buffered(1)),
                  pl.BlockSpec((1, hp, MLA_V + V_PAD, S), lambda b, h, i: (b, h, 0, 0),
                               pipeline_mode=pl.Buffered(1))],
        out_specs=pl.BlockSpec((tq, hp * MLA_V), lambda b, h, i: (b * nq + i, h)),
        out_shape=jax.ShapeDtypeStruct((T, H * MLA_V), BF16),
        scratch_shapes=[pltpu.VMEM((hp, 8, tq), F32), pltpu.VMEM((hp, MLA_V + V_PAD, tq), F32)],
        compiler_params=_params(3),
        name="mla_attn",
    )(q, k, v)
    return o, w_o.astype(BF16)


def _ml_proj_kernel(x_ref, g_ref, w_ref, wg_ref, bif_ref, cw_ref, cb_ref,
                    qt_ref, k_ref, vt_ref, o_ref, gates_ref, pbuf, *, tiles_per_seq):
    tm = x_ref.shape[0]
    QK = ML_HEADS * ML_QK
    V = ML_HEADS * ML_V
    h = _rms(x_ref[...], g_ref[...]).astype(BF16)
    p = jnp.dot(h, w_ref[...], preferred_element_type=F32)
    gates_ref[...] = jnp.dot(h, wg_ref[...], preferred_element_type=F32) + bif_ref[...]
    vt_ref[...] = p[:, 2 * QK:2 * QK + V].T.astype(BF16)
    o_ref[...] = p[:, 2 * QK + V:2 * QK + 2 * V]

    @pl.when(pl.program_id(0) % tiles_per_seq == 0)
    def _():
        pbuf[...] = jnp.zeros(pbuf.shape, F32)

    pqk = p[:, :2 * QK]
    tail = pbuf[...]
    acc = cb_ref[...] + cw_ref[ML_CONV - 1:ML_CONV, :] * pqk
    for kk in range(ML_CONV - 1):
        acc = acc + cw_ref[kk:kk + 1, :] * _shift_rows(pqk, tail, ML_CONV - 1 - kk)
    pbuf[...] = pqk[tm - 8:tm, :]
    qk = acc * _sigmoid(acc)
    qt_ref[...] = qk[:, :QK].T.astype(BF16)
    k_ref[...] = (qk[:, QK:] * ML_QK ** -0.5).astype(BF16)


def _ml_chunk_kernel(qt_ref, k_ref, vt_ref, o_ref, g_ref, on_ref, out_ref, c_scr, m_scr):
    L = k_ref.shape[0]
    H = ML_HEADS
    pair = LANES // ML_QK

    @pl.when(pl.program_id(1) == 0)
    def _():
        c_scr[...] = jnp.zeros(c_scr.shape, F32)
        m_scr[...] = jnp.zeros(m_scr.shape, F32)

    G = g_ref[...]
    GT = G.T
    lfG = -_softplus(-G)
    lfGT = -_softplus(-GT)
    row = lax.broadcasted_iota(jnp.int32, (L, L), 0)
    col = lax.broadcasted_iota(jnp.int32, (L, L), 1)
    causal = row <= col
    tril_b = (col <= row).astype(BF16)
    triu_b = causal.astype(BF16)
    fc_cols = sum(jnp.dot(tril_b, piece, preferred_element_type=F32) for piece in _split3(lfG))
    fc_rows = sum(jnp.dot(piece, triu_b, preferred_element_type=F32) for piece in _split3(lfGT))
    m_row = m_scr[...]
    ones_rows = (lax.broadcasted_iota(jnp.int32, (ML_V, L), 0) == 0).astype(BF16)
    row_half = lax.broadcasted_iota(jnp.int32, (LANES, L), 0) // ML_QK
    lane_half = lax.broadcasted_iota(jnp.int32, (L, LANES), 1) // ML_QK
    slane_half = lax.broadcasted_iota(jnp.int32, (1, LANES), 1) // ML_QK
    lane1 = lax.broadcasted_iota(jnp.int32, (1, LANES), 1)
    heads = range(H)
    pairs = range(H // pair)
    pr = [hd // pair for hd in heads]
    mm = [hd % pair for hd in heads]
    fc_row = [fc_rows[H + hd:H + hd + 1, :] for hd in heads]
    i_row = [GT[hd:hd + 1, :] for hd in heads]
    c_col = [G[:, hd:hd + 1] - fc_cols[:, H + hd:H + hd + 1] for hd in heads]
    m_prev = [m_row[:, hd:hd + 1] for hd in heads]
    dmat = [jnp.where(causal, fc_row[hd] + c_col[hd], -jnp.inf) for hd in heads]
    inter = [fc_row[hd] + m_prev[hd] for hd in heads]
    m_t = [jnp.maximum(inter[hd], jnp.max(dmat[hd], axis=0, keepdims=True)) for hd in heads]
    dw = [jnp.exp(dmat[hd] - m_t[hd]) for hd in heads]
    inter_w = [jnp.exp(inter[hd] - m_t[hd]) for hd in heads]
    qtp = [qt_ref[p * LANES:(p + 1) * LANES, :] for p in pairs]
    kp = [k_ref[:, p * LANES:(p + 1) * LANES] for p in pairs]
    qtm = [jnp.where(row_half == mm[hd], qtp[pr[hd]], jnp.zeros_like(qtp[0])) for hd in heads]
    km = [jnp.where(lane_half == mm[hd], kp[pr[hd]], jnp.zeros_like(kp[0])) for hd in heads]
    s = [_dot(kp[pr[hd]], qtm[hd]) * dw[hd] for hd in heads]
    vht = [jnp.concatenate([vt_ref[hd * ML_V:(hd + 1) * ML_V, :], ones_rows], axis=0)
           for hd in heads]
    ct = [c_scr[p] for p in pairs]
    nd = [_dot(vht[hd], s[hd]) + inter_w[hd] * _dot(ct[pr[hd]], qtm[hd]) for hd in heads]
    hid = [nd[hd][:ML_V] / jnp.maximum(jnp.abs(nd[hd][ML_V:ML_V + 1]), jnp.exp(-m_t[hd]))
           for hd in heads]
    hid = [x * lax.rsqrt(jnp.mean(x * x, axis=0, keepdims=True) + RMS_EPS) for x in hid]
    for hd in heads:
        cols = slice(hd * ML_V, (hd + 1) * ML_V)
        out_ref[:, cols] = (hid[hd].T * on_ref[:, cols] * _sigmoid(o_ref[:, cols])).astype(out_ref.dtype)
    m_new = [m_t[hd][:, L - 1:L] for hd in heads]
    fc_last = [fc_row[hd][:, L - 1:L] for hd in heads]
    decay = [jnp.exp(fc_last[hd] + m_prev[hd] - m_new[hd]) for hd in heads]
    w_row = [jnp.exp(fc_last[hd] - fc_row[hd] + i_row[hd] - m_new[hd]) for hd in heads]
    upd = [_dot(w_row[hd] * vht[hd].astype(F32), km[hd]) for hd in heads]
    for p in pairs:
        scale = decay[p * pair]
        for k in range(1, pair):
            scale = jnp.where(slane_half == k, decay[p * pair + k], scale)
        c_scr[p] = scale * ct[p] + sum(upd[p * pair + k] for k in range(pair))
    m_out = m_row
    for hd in heads:
        m_out = jnp.where(lane1 == hd, m_new[hd], m_out)
    m_scr[...] = m_out


def _mlstm(x, mix_g, w_in, b_if, conv_w, conv_b, out_norm, w_o, B, S):
    T, D = x.shape
    H = ML_HEADS
    QK = H * ML_QK
    V = H * ML_V
    main = 2 * QK + 2 * V
    w_main = w_in[:, :main].astype(BF16)
    w_gate = jnp.pad(w_in[:, main:], ((0, 0), (0, LANES - 2 * H))).astype(BF16)
    bif = jnp.pad(b_if, (0, LANES - 2 * H)).reshape(1, LANES)
    tm = min(TM_PROJ, S)
    nt = S // tm
    row_spec = lambda n: pl.BlockSpec((tm, n), lambda i: (i, 0))
    col_spec = lambda n: pl.BlockSpec((n, tm), lambda i: (0, i))
    qt, k, vt, o, gates = pl.pallas_call(
        functools.partial(_ml_proj_kernel, tiles_per_seq=nt),
        grid=(T // tm,),
        in_specs=[row_spec(D), _resident((1, D)), _resident((D, main)), _resident((D, LANES)),
                  _resident((1, LANES)), _resident((ML_CONV, 2 * QK)), _resident((1, 2 * QK))],
        out_specs=[col_spec(QK), row_spec(QK), col_spec(V), row_spec(V), row_spec(LANES)],
        out_shape=[jax.ShapeDtypeStruct((QK, T), BF16), jax.ShapeDtypeStruct((T, QK), BF16),
                   jax.ShapeDtypeStruct((V, T), BF16), jax.ShapeDtypeStruct((T, V), F32),
                   jax.ShapeDtypeStruct((T, LANES), F32)],
        scratch_shapes=[pltpu.VMEM((8, 2 * QK), F32)],
        compiler_params=_params(1),
        name="mlstm_proj",
    )(x, mix_g.reshape(1, D), w_main, w_gate, bif, conv_w, conv_b.reshape(1, -1))

    L = min(ML_CHUNK, S)
    nc = S // L
    chunk_spec = lambda n: pl.BlockSpec((L, n), lambda b, c: (b * nc + c, 0))
    chunk_t_spec = lambda n: pl.BlockSpec((n, L), lambda b, c: (0, b * nc + c))
    hid = pl.pallas_call(
        _ml_chunk_kernel,
        grid=(B, nc),
        in_specs=[chunk_t_spec(QK), chunk_spec(QK), chunk_t_spec(V), chunk_spec(V), chunk_spec(LANES),
                  _resident((1, V))],
        out_specs=chunk_spec(V),
        out_shape=jax.ShapeDtypeStruct((T, V), BF16),
        scratch_shapes=[pltpu.VMEM((H * ML_QK // LANES, 2 * ML_V, LANES), F32),
                        pltpu.VMEM((1, LANES), F32)],
        compiler_params=_params(2),
        name="mlstm_chunk",
    )(qt, k, vt, o, gates, out_norm.reshape(1, V))
    return hid, w_o.astype(BF16)


def _rw_proj_kernel(x_ref, g_ref, mu_ref, wr_ref, wk_ref, wv_ref, w1_ref, w2_ref, a1_ref, a2_ref,
                    g1_ref, g2_ref, w0_ref, a0_ref, kk_ref, ka_ref,
                    r_out, k_out, v_out, lw_out, a_out, kkraw_out, g_out, hbuf, *, tiles_per_seq):
    tm = x_ref.shape[0]
    D = x_ref.shape[1]
    h = _rms(x_ref[...], g_ref[...])

    @pl.when(pl.program_id(0) % tiles_per_seq == 0)
    def _():
        hbuf[...] = jnp.zeros(hbuf.shape, F32)

    xx = _shift_rows(h, hbuf[...], 1) - h
    hbuf[...] = h[tm - 8:tm, :]
    mix = lambda j: (h + xx * mu_ref[j:j + 1, :]).astype(BF16)
    r = jnp.dot(mix(0), wr_ref[...], preferred_element_type=F32)
    k = jnp.dot(mix(2), wk_ref[...], preferred_element_type=F32)
    v = jnp.dot(mix(3), wv_ref[...], preferred_element_type=F32)
    zw = w0_ref[...] + _dot(jnp.tanh(jnp.dot(mix(1), w1_ref[...], preferred_element_type=F32)),
                            w2_ref[...])
    log_decay = -math.exp(-0.5) * _sigmoid(zw)
    a = _sigmoid(a0_ref[...] + _dot(jnp.dot(mix(4), a1_ref[...], preferred_element_type=F32),
                                    a2_ref[...]))
    g = _dot(_sigmoid(jnp.dot(mix(5), g1_ref[...], preferred_element_type=F32)), g2_ref[...])
    r_out[...] = r.astype(r_out.dtype)
    v_out[...] = v.astype(v_out.dtype)
    lw_out[...] = log_decay
    a_out[...] = a.astype(a_out.dtype)
    kkraw_out[...] = (k * kk_ref[...]).astype(kkraw_out.dtype)
    k_out[...] = (k * (1.0 + (a - 1.0) * ka_ref[...])).astype(k_out.dtype)
    g_out[...] = g.astype(g_out.dtype)


def _group_sums(xs, ones_bd):
    n = xs[0].shape[0]
    pieces = []
    for x in xs:
        hi = x.astype(BF16)
        pieces += [hi, (x - hi.astype(F32)).astype(BF16)]
    both = jnp.dot(jnp.concatenate(pieces, axis=0), ones_bd, preferred_element_type=F32)
    return [both[2 * i * n:(2 * i + 1) * n] + both[(2 * i + 1) * n:(2 * i + 2) * n]
            for i in range(len(xs))]


def _rw_chunk_kernel(r_ref, k_ref, v_ref, lw_ref, a_ref, kk_ref, g_ref, rk_ref, lnw_ref, lnb_ref,
                     out_ref, s_scr):
    nb, L = r_ref.shape[0], r_ref.shape[1]
    W = s_scr.shape[1]
    N = RW_HEAD
    nh = W // N
    ng = r_ref.shape[2] // W

    @pl.when(pl.program_id(0) == 0)
    def _():
        s_scr[...] = jnp.zeros(s_scr.shape, F32)

    er = lax.broadcasted_iota(jnp.int32, (nh * L, W), 0)
    ec = lax.broadcasted_iota(jnp.int32, (nh * L, W), 1)
    same_head = (er // L) == (ec // N)
    sr = lax.broadcasted_iota(jnp.int32, (W, W), 0)
    sc = lax.broadcasted_iota(jnp.int32, (W, W), 1)
    state_bd = (sr // N) == (sc // N)
    ones_bd = state_bd.astype(BF16)
    lt = lax.broadcasted_iota(jnp.int32, (L, W), 0)
    ls = lax.broadcasted_iota(jnp.int32, (L, W), 1) % L
    strict = ls < lt
    incl = ls <= lt
    eye = (ls == lt).astype(F32)
    tr = lax.broadcasted_iota(jnp.int32, (L, L), 0)
    tc = lax.broadcasted_iota(jnp.int32, (L, L), 1)
    tril_b = (tc <= tr).astype(BF16)
    inv_n = 1.0 / N
    zero_b = jnp.zeros((nh * L, W), BF16)

    def expand(x):
        return jnp.where(same_head, jnp.concatenate([x.astype(BF16)] * nh, axis=0), zero_b)

    groups = range(nb * ng)
    bat = [g // ng for g in groups]
    cols = [slice((g % ng) * W, (g % ng + 1) * W) for g in groups]
    load = lambda ref: [ref[bat[g], :, cols[g]].astype(F32) for g in groups]
    r, k, v, lw, kk, a_sig = (load(ref) for ref in (r_ref, k_ref, v_ref, lw_ref, kk_ref, a_ref))
    sums = _group_sums([jnp.concatenate([kk[g] * kk[g], r[g] * k[g] * rk_ref[:, cols[g]]], axis=0)
                        for g in groups], ones_bd)
    kk = [kk[g] / jnp.maximum(jnp.sqrt(sums[g][:L]), 1e-12) for g in groups]
    bonus = [sums[g][L:] * v[g] for g in groups]
    bv = [kk[g] * a_sig[g] for g in groups]
    cum = [sum(jnp.dot(tril_b, piece, preferred_element_type=F32) for piece in _split3(x))
           for x in lw]
    cum_last = [x[L - 1:L, :] for x in cum]
    ar = [jnp.concatenate([-jnp.exp(cum[g] - lw[g]) * kk[g], jnp.exp(cum[g]) * r[g]],
                          axis=0).astype(BF16) for g in groups]
    p_inv = [jnp.exp(-x) for x in cum]
    b_e = [expand(p_inv[g] * bv[g]) for g in groups]
    k_e = [expand(p_inv[g] * k[g]) for g in groups]
    v_e = [expand(x) for x in v]
    gb = [_dot_nt(ar[g], b_e[g]) for g in groups]
    gk = [_dot_nt(ar[g], k_e[g]) for g in groups]
    m_ab = [jnp.where(strict, x[:L], 0.0) for x in gb]
    m_rb = [jnp.where(incl, x[L:], 0.0) for x in gb]
    m_akrk = [jnp.concatenate([jnp.where(strict, x[:L], 0.0), jnp.where(incl, x[L:], 0.0)],
                              axis=0) for x in gk]

    pw = [_dot(x, expand(x)) for x in m_ab]
    tinv = [eye + x for x in m_ab]
    n_sq = int(math.log2(L)) - 1
    for it in range(n_sq):
        pw_e = [expand(x) for x in pw]
        if it + 1 < n_sq:
            both = [_dot(jnp.concatenate([pw[g], tinv[g]], axis=0), pw_e[g]) for g in groups]
            pw = [x[:L] for x in both]
            tinv = [tinv[g] + both[g][L:] for g in groups]
        else:
            tinv = [tinv[g] + _dot(tinv[g], pw_e[g]) for g in groups]

    s0 = [s_scr[g] for g in groups]
    ars = [_dot_nt(ar[g], s0[g]) for g in groups]
    mv = [_dot(m_akrk[g], v_e[g]) for g in groups]
    u = [_dot(tinv[g], expand(ars[g][:L] + mv[g][:L])) for g in groups]
    y = [ars[g][L:] + mv[g][L:] + _dot(m_rb[g], expand(u[g])) for g in groups]
    for g in groups:
        p_out = jnp.exp(cum_last[g] - cum[g])
        upd = _dot_tn(jnp.concatenate([u[g], v[g]], axis=0),
                      jnp.concatenate([p_out * bv[g], p_out * k[g]], axis=0))
        s_scr[g] = s0[g] * jnp.exp(cum_last[g]) + jnp.where(state_bd, upd, 0.0)

    mean = [x * inv_n for x in _group_sums(y, ones_bd)]
    yc = [y[g] - mean[g] for g in groups]
    var = [x * inv_n for x in _group_sums([x * x for x in yc], ones_bd)]
    for g in groups:
        c = cols[g]
        yn = yc[g] * lax.rsqrt(var[g] + GN_EPS) * lnw_ref[:, c] + lnb_ref[:, c]
        out_ref[bat[g], :, c] = ((yn + bonus[g]) * g_ref[bat[g], :, c].astype(F32)).astype(out_ref.dtype)


def _rwkv(x, mix_g, mu, w_r, w_k, w_v, w0, w1, w2, a0, a1, a2, g1, g2, k_k, k_a, r_k,
          ln_w, ln_b, w_o, B, S):
    T, D = x.shape
    pad_cols = lambda w: jnp.pad(w, ((0, 0), (0, LANES - w.shape[1]))).astype(BF16)
    pad_rows = lambda w: jnp.pad(w, ((0, LANES - w.shape[0]), (0, 0))).astype(BF16)
    vec = lambda t: t.reshape(1, D)
    mu8 = jnp.pad(mu, ((0, 8 - mu.shape[0]), (0, 0)))
    tm = min(TM_PROJ, S)
    nt = S // tm
    row_spec = pl.BlockSpec((tm, D), lambda i: (i, 0))
    outs = pl.pallas_call(
        functools.partial(_rw_proj_kernel, tiles_per_seq=nt),
        grid=(T // tm,),
        in_specs=[row_spec, _resident((1, D)), _resident((8, D)),
                  _resident((D, D)), _resident((D, D)), _resident((D, D)),
                  _resident((D, LANES)), _resident((LANES, D)),
                  _resident((D, LANES)), _resident((LANES, D)),
                  _resident((D, LANES)), _resident((LANES, D)),
                  _resident((1, D)), _resident((1, D)), _resident((1, D)), _resident((1, D))],
        out_specs=[row_spec] * 7,
        out_shape=[jax.ShapeDtypeStruct((T, D), F32 if i == 3 else BF16) for i in range(7)],
        scratch_shapes=[pltpu.VMEM((8, D), F32)],
        compiler_params=_params(1),
        name="rwkv_proj",
    )(x, vec(mix_g), mu8, w_r.astype(BF16), w_k.astype(BF16), w_v.astype(BF16),
      pad_cols(w1), pad_rows(w2), pad_cols(a1), pad_rows(a2), pad_cols(g1), pad_rows(g2),
      vec(w0), vec(a0), vec(k_k), vec(k_a))
    r, k, v, lw, a, kkraw, g = outs

    L = min(RW_CHUNK, S)
    nc = S // L
    W = RW_GROUP * RW_HEAD
    assert L == RW_HEAD
    chunk_spec = pl.BlockSpec((B, L, D), lambda c: (0, c, 0))
    seq = lambda t: t.reshape(B, S, D)
    yg = pl.pallas_call(
        _rw_chunk_kernel,
        grid=(nc,),
        in_specs=[chunk_spec] * 7 + [_resident((1, D))] * 3,
        out_specs=chunk_spec,
        out_shape=jax.ShapeDtypeStruct((B, S, D), BF16),
        scratch_shapes=[pltpu.VMEM((B * D // W, W, W), F32)],
        compiler_params=_params(1),
        name="rwkv_chunk",
    )(seq(r), seq(k), seq(v), seq(lw), seq(a), seq(kkraw), seq(g), vec(r_k), vec(ln_w), vec(ln_b))
    return yg.reshape(T, D), w_o.astype(BF16)


def kernel(x, positions, ffn_norm, ffn_w_gate, ffn_w_up, ffn_w_down, mix_norm, final_norm, mla_w_a, mla_q_norm, mla_w_qb, mla_kv_norm, mla_w_kvb, mla_w_o, ml_w_in, ml_b_if, ml_conv_w, ml_conv_b, ml_out_norm, ml_w_o, rw_mu, rw_w_r, rw_w_k, rw_w_v, rw_w0, rw_w1, rw_w2, rw_a0, rw_a1, rw_a2, rw_g1, rw_g2, rw_k_k, rw_k_a, rw_r_k, rw_ln_w, rw_ln_b, rw_w_o):
    B, S, D = x.shape
    depth = mix_norm.shape[0]
    n_mixers = 3
    h = x.reshape(B * S, D)

    wg, wu, wd = (w.astype(BF16) for w in (ffn_w_gate, ffn_w_up, ffn_w_down))

    def ffn(h, layer, half, **kw):
        return _ffn(h, ffn_norm[layer, half], wg, wu, wd, (layer, half), **kw)

    for layer in range(depth):
        h = ffn(h, layer, 0)
        kind, j = layer % n_mixers, layer // n_mixers
        if kind == 0:
            mixer = _mla(h, positions, mix_norm[layer], mla_w_a[j], mla_q_norm[j], mla_w_qb[j],
                         mla_kv_norm[j], mla_w_kvb[j], mla_w_o[j], B, S)
        elif kind == 1:
            mixer = _mlstm(h, mix_norm[layer], ml_w_in[j], ml_b_if[j], ml_conv_w[j], ml_conv_b[j],
                           ml_out_norm[j], ml_w_o[j], B, S)
        else:
            mixer = _rwkv(h, mix_norm[layer], rw_mu[j], rw_w_r[j], rw_w_k[j], rw_w_v[j], rw_w0[j],
                          rw_w1[j], rw_w2[j], rw_a0[j], rw_a1[j], rw_a2[j], rw_g1[j], rw_g2[j],
                          rw_k_k[j], rw_k_a[j], rw_r_k[j], rw_ln_w[j], rw_ln_b[j], rw_w_o[j], B, S)
        h = ffn(h, layer, 1, mixer=mixer, final_g=final_norm if layer == depth - 1 else None)
    return h.reshape(B, S, D)
```

```python
import functools
import math

import jax
import jax.numpy as jnp
from jax import lax
from jax.experimental import pallas as pl
from jax.experimental.pallas import tpu as pltpu

F32 = jnp.float32
BF16 = jnp.bfloat16

RMS_EPS = 1e-6
GN_EPS = 64e-5
LANES = 128
MXU_DIM = 256
V7X_VMEM_BYTES = 64 * 1024 * 1024
VMEM_LIMIT = V7X_VMEM_BYTES * 7 // 8

MLA_HEADS = 8
MLA_Q_RANK = 512
MLA_KV_RANK = 256
MLA_NOPE = 128
MLA_ROPE = 64
MLA_V = 128
ROPE_THETA = 10000.0
ML_HEADS = 8
ML_QK = 64
ML_V = 128
ML_CONV = 4
RW_HEAD = 64

TM_FFN = 512
TM_PROJ = 512
TQ_ATTN = 512
V_PAD = 16
ATTN_HEADS_PER_STEP = 4
ML_CHUNK = 256
RW_CHUNK = 64
RW_GROUP = MXU_DIM // RW_HEAD


def _params(n_axes):
    return pltpu.CompilerParams(dimension_semantics=("arbitrary",) * n_axes,
                                vmem_limit_bytes=VMEM_LIMIT)


def _resident(shape):
    nd = len(shape)
    return pl.BlockSpec(shape, lambda *_: (0,) * nd, pipeline_mode=pl.Buffered(1))


def _rms(x, g, eps=RMS_EPS):
    return x * lax.rsqrt(jnp.mean(x * x, axis=-1, keepdims=True) + eps) * g


def _dot(a, b):
    return jnp.dot(a.astype(BF16), b.astype(BF16), preferred_element_type=F32)


def _dot_nt(a, b):
    return lax.dot_general(a.astype(BF16), b.astype(BF16), (((1,), (1,)), ((), ())),
                           preferred_element_type=F32)


def _dot_tn(a, b):
    return lax.dot_general(a.astype(BF16), b.astype(BF16), (((0,), (0,)), ((), ())),
                           preferred_element_type=F32)


def _split3(x):
    hi = x.astype(BF16)
    r1 = x - hi.astype(F32)
    mid = r1.astype(BF16)
    lo = (r1 - mid.astype(F32)).astype(BF16)
    return hi, mid, lo


def _shift_rows(x, tail, shift):
    rolled = pltpu.roll(x, shift, 0)
    row = lax.broadcasted_iota(jnp.int32, tail.shape, 0)
    head = jnp.where(row < shift, pltpu.roll(tail, shift, 0), rolled[0:8])
    return jnp.concatenate([head, rolled[8:]], axis=0)


def _softplus(y):
    return jnp.maximum(y, 0.0) + jnp.log1p(jnp.exp(-jnp.abs(y)))


def _sigmoid(y):
    return 1.0 / (1.0 + jnp.exp(-y))


def _ffn_kernel(*refs, mixer, final):
    x_ref, g_ref, wg_ref, wu_ref, wd_ref = refs[:5]
    o_ref = refs[-1]
    x = x_ref[...]
    if mixer:
        a_ref, wo_ref = refs[5:7]
        x = x + jnp.dot(a_ref[...], wo_ref[...], preferred_element_type=F32)
    h = _rms(x, g_ref[...]).astype(BF16)
    gate = jnp.dot(h, wg_ref[...], preferred_element_type=F32)
    up = jnp.dot(h, wu_ref[...], preferred_element_type=F32)
    act = (gate * _sigmoid(gate) * up).astype(BF16)
    y = jnp.dot(act, wd_ref[...], preferred_element_type=F32)
    out = x + 0.5 * y
    if final:
        out = _rms(out, refs[-2][...])
    o_ref[...] = out


def _ffn(x, g, wg, wu, wd, index, mixer=None, final_g=None):
    T, D = x.shape
    F = wg.shape[-1]
    tm = min(TM_FFN, T)
    row_spec = lambda n: pl.BlockSpec((tm, n), lambda i: (i, 0))
    picked = lambda r, c: pl.BlockSpec((None, None, r, c), lambda i: (*index, 0, 0),
                                       pipeline_mode=pl.Buffered(1))
    args = [x, g.reshape(1, D), wg, wu, wd]
    in_specs = [row_spec(D), _resident((1, D)), picked(D, F), picked(D, F), picked(F, D)]
    if mixer is not None:
        a, w_o = mixer
        args += [a, w_o]
        in_specs += [row_spec(a.shape[1]), _resident(w_o.shape)]
    if final_g is not None:
        args.append(final_g.reshape(1, D))
        in_specs.append(_resident((1, D)))
    return pl.pallas_call(
        functools.partial(_ffn_kernel, mixer=mixer is not None, final=final_g is not None),
        grid=(T // tm,),
        in_specs=in_specs,
        out_specs=row_spec(D),
        out_shape=jax.ShapeDtypeStruct((T, D), F32),
        compiler_params=_params(1),
        name="ffn",
    )(*args)


def _rope_tables(pos_col, invf):
    tm = pos_col.shape[0]
    nfreq = MLA_ROPE // 2
    nblk = LANES // nfreq
    rows = tm // nblk
    posb = jnp.broadcast_to(pos_col.astype(F32), (tm, LANES))
    lane_blk = lax.broadcasted_iota(jnp.int32, (rows, LANES), 1) // nfreq
    packed = posb[0:rows]
    for c in range(1, nblk):
        packed = jnp.where(lane_blk == c, posb[c * rows:(c + 1) * rows], packed)
    ang = packed * invf

    def spread(t):
        rolled = [t] + [pltpu.roll(t, nfreq * k, 1) for k in range(1, nblk)]
        blocks = []
        for c in range(nblk):
            out = rolled[(0 - c) % nblk]
            for dst in range(1, nblk):
                out = jnp.where(lane_blk == dst, rolled[(dst - c) % nblk], out)
            blocks.append(out)
        return jnp.concatenate(blocks, axis=0)

    return spread(jnp.cos(ang)), spread(jnp.sin(ang))


def _mla_proj_kernel(x_ref, g_ref, pos_ref, invf_ref, sgn_ref, wa_ref, qn_ref, wqb_ref,
                     kvn_ref, wkvb_ref, qt_ref, k_ref, vt_ref, *, q_scale):
    H = MLA_HEADS
    h = _rms(x_ref[...], g_ref[...]).astype(BF16)
    lat = jnp.dot(h, wa_ref[...], preferred_element_type=F32)
    cq = _rms(lat[:, :MLA_Q_RANK], qn_ref[...]).astype(BF16)
    ckv = _rms(lat[:, MLA_Q_RANK:MLA_Q_RANK + MLA_KV_RANK], kvn_ref[...]).astype(BF16)
    q = jnp.dot(cq, wqb_ref[...], preferred_element_type=F32) * q_scale
    kv = jnp.dot(ckv, wkvb_ref[...], preferred_element_type=F32)
    cos, sin = _rope_tables(pos_ref[...], invf_ref[...])
    sin = sin * sgn_ref[...]
    kpe0 = MLA_Q_RANK + MLA_KV_RANK
    kx = lat[:, kpe0:kpe0 + LANES] * cos + lat[:, kpe0 + LANES:kpe0 + 2 * LANES] * sin
    half = lax.broadcasted_iota(jnp.int32, kx.shape, 1) // MLA_ROPE
    pe0 = H * MLA_NOPE
    sw0 = pe0 + H * MLA_ROPE
    for hd in range(H):
        p, m = divmod(hd, LANES // MLA_ROPE)
        qpe = (q[:, pe0 + p * LANES:pe0 + (p + 1) * LANES] * cos
               + q[:, sw0 + p * LANES:sw0 + (p + 1) * LANES] * sin)
        sel = half == m
        qt_ref[0, hd, 0:LANES, :] = q[:, hd * MLA_NOPE:(hd + 1) * MLA_NOPE].T.astype(BF16)
        qt_ref[0, hd, LANES:2 * LANES, :] = jnp.where(sel, qpe, 0.0).T.astype(BF16)
        k0 = hd * (MLA_NOPE + MLA_V)
        k_ref[0, hd, :, 0:LANES] = kv[:, k0:k0 + MLA_NOPE].astype(BF16)
        k_ref[0, hd, :, LANES:2 * LANES] = jnp.where(sel, kx, 0.0).astype(BF16)
        vt_ref[0, hd, 0:MLA_V, :] = kv[:, k0 + MLA_NOPE:k0 + MLA_NOPE + MLA_V].T.astype(BF16)
        vt_ref[0, hd, MLA_V:MLA_V + V_PAD, :] = jnp.ones((V_PAD, kv.shape[0]), BF16)


def _attn_kernel(qt_ref, k_ref, vt_ref, o_ref, m_scr, acc_scr, *, tq):
    i = pl.program_id(2)
    heads = range(qt_ref.shape[1])
    dv = o_ref.shape[1] // qt_ref.shape[1]
    qt = [qt_ref[0, h] for h in heads]
    for h in heads:
        m_scr[h] = jnp.full(m_scr.shape[1:], -jnp.inf, F32)
        acc_scr[h] = jnp.zeros(acc_scr.shape[1:], F32)

    def update(h, m_new, pv):
        m_old = m_scr[h, 0:1, :]
        m_scr[h, 0:1, :] = m_new
        acc_scr[h] = jnp.exp2(m_old - m_new) * acc_scr[h] + pv

    def step(j):
        start = pl.multiple_of(j * tq, tq)
        s = [jnp.dot(k_ref[0, h, pl.ds(start, tq), :], qt[h], preferred_element_type=F32)
             for h in heads]
        m_new = [jnp.maximum(m_scr[h, 0:1, :], jnp.max(s[h], axis=0, keepdims=True)) for h in heads]
        p = [jnp.exp2(s[h] - m_new[h]).astype(BF16) for h in heads]
        for h in heads:
            update(h, m_new[h], jnp.dot(vt_ref[0, h, :, pl.ds(start, tq)], p[h],
                                        preferred_element_type=F32))

    def diagonal_step(j):
        half = tq // 2
        start = pl.multiple_of(j * tq, tq)
        mid = pl.multiple_of(j * tq + half, half)
        key = lax.broadcasted_iota(jnp.int32, (half, tq), 0)
        qry = lax.broadcasted_iota(jnp.int32, (half, tq), 1)
        keep_top = key <= qry
        keep_bot = keep_top[:, :half]
        s_top = [jnp.where(keep_top, jnp.dot(k_ref[0, h, pl.ds(start, half), :], qt[h],
                                             preferred_element_type=F32), -jnp.inf) for h in heads]
        s_bot = [jnp.where(keep_bot, jnp.dot(k_ref[0, h, pl.ds(mid, half), :], qt[h][:, half:],
                                             preferred_element_type=F32), -jnp.inf) for h in heads]
        m_top = [jnp.max(x, axis=0, keepdims=True) for x in s_top]
        m_bot = [jnp.max(x, axis=0, keepdims=True) for x in s_bot]
        m_new = [jnp.maximum(m_scr[h, 0:1, :], jnp.concatenate(
            [m_top[h][:, :half], jnp.maximum(m_top[h][:, half:], m_bot[h])], axis=1)) for h in heads]
        p_top = [jnp.exp2(s_top[h] - m_new[h]).astype(BF16) for h in heads]
        p_bot = [jnp.exp2(s_bot[h] - m_new[h][:, half:]).astype(BF16) for h in heads]
        for h in heads:
            pv_top = jnp.dot(vt_ref[0, h, :, pl.ds(start, half)], p_top[h], preferred_element_type=F32)
            pv_bot = jnp.dot(vt_ref[0, h, :, pl.ds(mid, half)], p_bot[h], preferred_element_type=F32)
            update(h, m_new[h], pv_top + jnp.concatenate([jnp.zeros_like(pv_bot), pv_bot], axis=1))

    def pair(jj, c):
        step(2 * jj)
        step(2 * jj + 1)
        return c

    lax.fori_loop(0, i // 2, pair, 0)

    @pl.when(i % 2 == 1)
    def _():
        step(i - 1)

    diagonal_step(i)
    for h in heads:
        acc = acc_scr[h]
        o_ref[:, h * dv:(h + 1) * dv] = (acc[:dv] / acc[dv:dv + 1]).T.astype(o_ref.dtype)


def _mla(x, positions, mix_g, w_a, q_norm, w_qb, kv_norm, w_kvb, w_o, B, S):
    T, D = x.shape
    H = MLA_HEADS
    kpe = w_a[:, MLA_Q_RANK + MLA_KV_RANK:]
    kpe_sw = jnp.concatenate([kpe[:, MLA_ROPE // 2:], kpe[:, :MLA_ROPE // 2]], axis=1)
    reps = LANES // MLA_ROPE
    wa2 = jnp.concatenate([w_a[:, :MLA_Q_RANK + MLA_KV_RANK], jnp.tile(kpe, (1, reps)),
                           jnp.tile(kpe_sw, (1, reps))], axis=1).astype(BF16)
    wq3 = w_qb.reshape(MLA_Q_RANK, H, MLA_NOPE + MLA_ROPE)
    q_nope = wq3[:, :, :MLA_NOPE].reshape(MLA_Q_RANK, H * MLA_NOPE)
    q_pe = wq3[:, :, MLA_NOPE:]
    q_pe_sw = jnp.concatenate([q_pe[:, :, MLA_ROPE // 2:], q_pe[:, :, :MLA_ROPE // 2]], axis=2)
    wqb2 = jnp.concatenate([q_nope, q_pe.reshape(MLA_Q_RANK, H * MLA_ROPE),
                            q_pe_sw.reshape(MLA_Q_RANK, H * MLA_ROPE)], axis=1).astype(BF16)
    inv_freq = 1.0 / (ROPE_THETA ** (jnp.arange(0, MLA_ROPE, 2, dtype=F32) / MLA_ROPE))
    invf = jnp.tile(inv_freq, LANES // (MLA_ROPE // 2)).reshape(1, LANES)
    first_half = (jnp.arange(LANES) % MLA_ROPE) < MLA_ROPE // 2
    sgn = jnp.where(first_half, -1.0, 1.0).astype(F32).reshape(1, LANES)
    q_scale = (MLA_NOPE + MLA_ROPE) ** -0.5 * math.log2(math.e)

    tm = min(TM_PROJ, S)
    nt = S // tm
    wa_cols = wa2.shape[1]
    wq_cols = wqb2.shape[1]
    kv_cols = w_kvb.shape[1]
    qk_dim = 2 * LANES
    q, k, v = pl.pallas_call(
        functools.partial(_mla_proj_kernel, q_scale=q_scale),
        grid=(B, nt),
        in_specs=[pl.BlockSpec((tm, D), lambda b, i: (b * nt + i, 0)),
                  _resident((1, D)),
                  pl.BlockSpec((tm, 1), lambda b, i: (b * nt + i, 0)),
                  _resident((1, LANES)), _resident((1, LANES)),
                  _resident((D, wa_cols)), _resident((1, MLA_Q_RANK)),
                  _resident((MLA_Q_RANK, wq_cols)), _resident((1, MLA_KV_RANK)),
                  _resident((MLA_KV_RANK, kv_cols))],
        out_specs=[pl.BlockSpec((1, H, qk_dim, tm), lambda b, i: (b, 0, 0, i)),
                   pl.BlockSpec((1, H, tm, qk_dim), lambda b, i: (b, 0, i, 0)),
                   pl.BlockSpec((1, H, MLA_V + V_PAD, tm), lambda b, i: (b, 0, 0, i))],
        out_shape=[jax.ShapeDtypeStruct((B, H, qk_dim, S), BF16),
                   jax.ShapeDtypeStruct((B, H, S, qk_dim), BF16),
                   jax.ShapeDtypeStruct((B, H, MLA_V + V_PAD, S), BF16)],
        compiler_params=_params(2),
        name="mla_proj",
    )(x, mix_g.reshape(1, D), positions.reshape(T, 1), invf, sgn, wa2, q_norm.reshape(1, -1),
      wqb2, kv_norm.reshape(1, -1), w_kvb.astype(BF16))

    tq = min(TQ_ATTN, S)
    nq = S // tq
    hp = ATTN_HEADS_PER_STEP
    o = pl.pallas_call(
        functools.partial(_attn_kernel, tq=tq),
        grid=(B, H // hp, nq),
        in_specs=[pl.BlockSpec((1, hp, qk_dim, tq), lambda b, h, i: (b, h, 0, i)),
                  pl.BlockSpec((1, hp, S, qk_dim), lambda b, h, i: (b, h, 0, 0),
                               pipeline_mode=pl.Buffered(1)),
                  pl.BlockSpec((1, hp, MLA_V + V_PAD, S), lambda b, h, i: (b, h, 0, 0),
                               pipeline_mode=pl.Buffered(1))],
        out_specs=pl.BlockSpec((tq, hp * MLA_V), lambda b, h, i: (b * nq + i, h)),
        out_shape=jax.ShapeDtypeStruct((T, H * MLA_V), BF16),
        scratch_shapes=[pltpu.VMEM((hp, 8, tq), F32), pltpu.VMEM((hp, MLA_V + V_PAD, tq), F32)],
        compiler_params=_params(3),
        name="mla_attn",
    )(q, k, v)
    return o, w_o.astype(BF16)


def _ml_proj_kernel(x_ref, g_ref, w_ref, wg_ref, bif_ref, cw_ref, cb_ref,
                    qt_ref, k_ref, vt_ref, o_ref, gates_ref, pbuf, *, tiles_per_seq):
    tm = x_ref.shape[0]
    QK = ML_HEADS * ML_QK
    V = ML_HEADS * ML_V

    @pl.when(pl.program_id(0) % tiles_per_seq == 0)
    def _():
        pbuf[...] = jnp.zeros(pbuf.shape, F32)

    h = _rms(x_ref[...], g_ref[...]).astype(BF16)
    w = w_ref[...]
    p_vo = jnp.dot(h, w[:, 2 * QK:], preferred_element_type=F32)
    gates_ref[...] = jnp.dot(h, wg_ref[...], preferred_element_type=F32) + bif_ref[...]
    vt_ref[...] = p_vo[:, :V].T.astype(BF16)
    o_ref[...] = p_vo[:, V:]
    pqk = jnp.dot(h, w[:, :2 * QK], preferred_element_type=F32)
    tail = pbuf[...]
    acc = cb_ref[...] + cw_ref[ML_CONV - 1:ML_CONV, :] * pqk
    for kk in range(ML_CONV - 1):
        acc = acc + cw_ref[kk:kk + 1, :] * _shift_rows(pqk, tail, ML_CONV - 1 - kk)
    pbuf[...] = pqk[tm - 8:tm, :]
    qk = acc * _sigmoid(acc)
    qt_ref[...] = qk[:, :QK].T.astype(BF16)
    k_ref[...] = (qk[:, QK:] * ML_QK ** -0.5).astype(BF16)


def _ml_chunk_kernel(qt_ref, k_ref, vt_ref, o_ref, g_ref, on_ref, out_ref, c_scr, m_scr):
    L = k_ref.shape[0]
    H = ML_HEADS
    pair = LANES // ML_QK

    @pl.when(pl.program_id(1) == 0)
    def _():
        c_scr[...] = jnp.zeros(c_scr.shape, F32)
        m_scr[...] = jnp.zeros(m_scr.shape, F32)

    G = g_ref[...]
    GT = G.T
    lfG = -_softplus(-G)
    lfGT = -_softplus(-GT)
    row = lax.broadcasted_iota(jnp.int32, (L, L), 0)
    col = lax.broadcasted_iota(jnp.int32, (L, L), 1)
    causal = row <= col
    tril_b = (col <= row).astype(BF16)
    triu_b = causal.astype(BF16)
    fc_cols = sum(jnp.dot(tril_b, piece, preferred_element_type=F32) for piece in _split3(lfG))
    fc_rows = sum(jnp.dot(piece, triu_b, preferred_element_type=F32) for piece in _split3(lfGT))
    m_row = m_scr[...]
    ones_rows = (lax.broadcasted_iota(jnp.int32, (ML_V, L), 0) == 0).astype(BF16)
    row_half = lax.broadcasted_iota(jnp.int32, (LANES, L), 0) // ML_QK
    lane_half = lax.broadcasted_iota(jnp.int32, (L, LANES), 1) // ML_QK
    slane_half = lax.broadcasted_iota(jnp.int32, (1, LANES), 1) // ML_QK
    lane1 = lax.broadcasted_iota(jnp.int32, (1, LANES), 1)
    heads = range(H)
    pairs = range(H // pair)
    pr = [hd // pair for hd in heads]
    mm = [hd % pair for hd in heads]
    fc_row = [fc_rows[H + hd:H + hd + 1, :] for hd in heads]
    i_row = [GT[hd:hd + 1, :] for hd in heads]
    c_col = [G[:, hd:hd + 1] - fc_cols[:, H + hd:H + hd + 1] for hd in heads]
    m_prev = [m_row[:, hd:hd + 1] for hd in heads]
    dmat = [jnp.where(causal, fc_row[hd] + c_col[hd], -jnp.inf) for hd in heads]
    inter = [fc_row[hd] + m_prev[hd] for hd in heads]
    m_t = [jnp.maximum(inter[hd], jnp.max(dmat[hd], axis=0, keepdims=True)) for hd in heads]
    dw = [jnp.exp(dmat[hd] - m_t[hd]) for hd in heads]
    inter_w = [jnp.exp(inter[hd] - m_t[hd]) for hd in heads]
    qtp = [qt_ref[p * LANES:(p + 1) * LANES, :] for p in pairs]
    kp = [k_ref[:, p * LANES:(p + 1) * LANES] for p in pairs]
    qtm = [jnp.where(row_half == mm[hd], qtp[pr[hd]], jnp.zeros_like(qtp[0])) for hd in heads]
    km = [jnp.where(lane_half == mm[hd], kp[pr[hd]], jnp.zeros_like(kp[0])) for hd in heads]
    s = [_dot(kp[pr[hd]], qtm[hd]) * dw[hd] for hd in heads]
    vht = [jnp.concatenate([vt_ref[hd * ML_V:(hd + 1) * ML_V, :], ones_rows], axis=0)
           for hd in heads]
    ct = [c_scr[p] for p in pairs]
    nd = [_dot(vht[hd], s[hd]) + inter_w[hd] * _dot(ct[pr[hd]], qtm[hd]) for hd in heads]
    hid = [nd[hd][:ML_V] / jnp.maximum(jnp.abs(nd[hd][ML_V:ML_V + 1]), jnp.exp(-m_t[hd]))
           for hd in heads]
    hid = [x * lax.rsqrt(jnp.mean(x * x, axis=0, keepdims=True) + RMS_EPS) for x in hid]
    for hd in heads:
        cols = slice(hd * ML_V, (hd + 1) * ML_V)
        out_ref[:, cols] = (hid[hd].T * on_ref[:, cols] * _sigmoid(o_ref[:, cols])).astype(out_ref.dtype)
    m_new = [m_t[hd][:, L - 1:L] for hd in heads]
    fc_last = [fc_row[hd][:, L - 1:L] for hd in heads]
    decay = [jnp.exp(fc_last[hd] + m_prev[hd] - m_new[hd]) for hd in heads]
    w_row = [jnp.exp(fc_last[hd] - fc_row[hd] + i_row[hd] - m_new[hd]) for hd in heads]
    upd = [_dot(w_row[hd] * vht[hd].astype(F32), km[hd]) for hd in heads]
    for p in pairs:
        scale = decay[p * pair]
        for k in range(1, pair):
            scale = jnp.where(slane_half == k, decay[p * pair + k], scale)
        c_scr[p] = scale * ct[p] + sum(upd[p * pair + k] for k in range(pair))
    m_out = m_row
    for hd in heads:
        m_out = jnp.where(lane1 == hd, m_new[hd], m_out)
    m_scr[...] = m_out


def _mlstm(x, mix_g, w_in, b_if, conv_w, conv_b, out_norm, w_o, B, S):
    T, D = x.shape
    H = ML_HEADS
    QK = H * ML_QK
    V = H * ML_V
    main = 2 * QK + 2 * V
    w_main = w_in[:, :main].astype(BF16)
    w_gate = jnp.pad(w_in[:, main:], ((0, 0), (0, LANES - 2 * H))).astype(BF16)
    bif = jnp.pad(b_if, (0, LANES - 2 * H)).reshape(1, LANES)
    tm = min(TM_PROJ, S)
    nt = S // tm
    row_spec = lambda n: pl.BlockSpec((tm, n), lambda i: (i, 0))
    col_spec = lambda n: pl.BlockSpec((n, tm), lambda i: (0, i))
    qt, k, vt, o, gates = pl.pallas_call(
        functools.partial(_ml_proj_kernel, tiles_per_seq=nt),
        grid=(T // tm,),
        in_specs=[row_spec(D), _resident((1, D)), _resident((D, main)), _resident((D, LANES)),
                  _resident((1, LANES)), _resident((ML_CONV, 2 * QK)), _resident((1, 2 * QK))],
        out_specs=[col_spec(QK), row_spec(QK), col_spec(V), row_spec(V), row_spec(LANES)],
        out_shape=[jax.ShapeDtypeStruct((QK, T), BF16), jax.ShapeDtypeStruct((T, QK), BF16),
                   jax.ShapeDtypeStruct((V, T), BF16), jax.ShapeDtypeStruct((T, V), F32),
                   jax.ShapeDtypeStruct((T, LANES), F32)],
        scratch_shapes=[pltpu.VMEM((8, 2 * QK), F32)],
        compiler_params=_params(1),
        name="mlstm_proj",
    )(x, mix_g.reshape(1, D), w_main, w_gate, bif, conv_w, conv_b.reshape(1, -1))

    L = min(ML_CHUNK, S)
    nc = S // L
    chunk_spec = lambda n: pl.BlockSpec((L, n), lambda b, c: (b * nc + c, 0))
    chunk_t_spec = lambda n: pl.BlockSpec((n, L), lambda b, c: (0, b * nc + c))
    hid = pl.pallas_call(
        _ml_chunk_kernel,
        grid=(B, nc),
        in_specs=[chunk_t_spec(QK), chunk_spec(QK), chunk_t_spec(V), chunk_spec(V), chunk_spec(LANES),
                  _resident((1, V))],
        out_specs=chunk_spec(V),
        out_shape=jax.ShapeDtypeStruct((T, V), BF16),
        scratch_shapes=[pltpu.VMEM((H * ML_QK // LANES, 2 * ML_V, LANES), F32),
                        pltpu.VMEM((1, LANES), F32)],
        compiler_params=_params(2),
        name="mlstm_chunk",
    )(qt, k, vt, o, gates, out_norm.reshape(1, V))
    return hid, w_o.astype(BF16)


def _rw_proj_kernel(x_ref, g_ref, mu_ref, wr_ref, wk_ref, wv_ref, w1_ref, w2_ref, a1_ref, a2_ref,
                    g1_ref, g2_ref, w0_ref, a0_ref, kk_ref, ka_ref,
                    r_out, k_out, v_out, lw_out, a_out, kkraw_out, g_out, hbuf, *, tiles_per_seq):
    tm = x_ref.shape[0]
    D = x_ref.shape[1]

    @pl.when(pl.program_id(0) % tiles_per_seq == 0)
    def _():
        hbuf[...] = jnp.zeros(hbuf.shape, F32)

    h = _rms(x_ref[...], g_ref[...])
    xx = _shift_rows(h, hbuf[...], 1) - h
    hbuf[...] = h[tm - 8:tm, :]
    mix = lambda j: (h + xx * mu_ref[j:j + 1, :]).astype(BF16)
    r = jnp.dot(mix(0), wr_ref[...], preferred_element_type=F32)
    k = jnp.dot(mix(2), wk_ref[...], preferred_element_type=F32)
    v = jnp.dot(mix(3), wv_ref[...], preferred_element_type=F32)
    zw = w0_ref[...] + _dot(jnp.tanh(jnp.dot(mix(1), w1_ref[...], preferred_element_type=F32)),
                            w2_ref[...])
    log_decay = -math.exp(-0.5) * _sigmoid(zw)
    a = _sigmoid(a0_ref[...] + _dot(jnp.dot(mix(4), a1_ref[...], preferred_element_type=F32),
                                    a2_ref[...]))
    g = _dot(_sigmoid(jnp.dot(mix(5), g1_ref[...], preferred_element_type=F32)), g2_ref[...])
    r_out[...] = r.astype(r_out.dtype)
    v_out[...] = v.astype(v_out.dtype)
    lw_out[...] = log_decay
    a_out[...] = a.astype(a_out.dtype)
    kkraw_out[...] = (k * kk_ref[...]).astype(kkraw_out.dtype)
    k_out[...] = (k * (1.0 + (a - 1.0) * ka_ref[...])).astype(k_out.dtype)
    g_out[...] = g.astype(g_out.dtype)


def _group_sums(xs, ones_bd):
    n = xs[0].shape[0]
    pieces = []
    for x in xs:
        hi = x.astype(BF16)
        pieces += [hi, (x - hi.astype(F32)).astype(BF16)]
    both = jnp.dot(jnp.concatenate(pieces, axis=0), ones_bd, preferred_element_type=F32)
    return [both[2 * i * n:(2 * i + 1) * n] + both[(2 * i + 1) * n:(2 * i + 2) * n]
            for i in range(len(xs))]


def _rw_chunk_kernel(r_ref, k_ref, v_ref, lw_ref, a_ref, kk_ref, g_ref, rk_ref, lnw_ref, lnb_ref,
                     out_ref, s_scr):
    nb, L = r_ref.shape[0], r_ref.shape[1]
    W = s_scr.shape[1]
    N = RW_HEAD
    nh = W // N
    ng = r_ref.shape[2] // W

    @pl.when(pl.program_id(0) == 0)
    def _():
        s_scr[...] = jnp.zeros(s_scr.shape, F32)

    er = lax.broadcasted_iota(jnp.int32, (nh * L, W), 0)
    ec = lax.broadcasted_iota(jnp.int32, (nh * L, W), 1)
    same_head = (er // L) == (ec // N)
    sr = lax.broadcasted_iota(jnp.int32, (W, W), 0)
    sc = lax.broadcasted_iota(jnp.int32, (W, W), 1)
    state_bd = (sr // N) == (sc // N)
    ones_bd = state_bd.astype(BF16)
    lt = lax.broadcasted_iota(jnp.int32, (L, W), 0)
    ls = lax.broadcasted_iota(jnp.int32, (L, W), 1) % L
    strict = ls < lt
    incl = ls <= lt
    eye = (ls == lt).astype(F32)
    tr = lax.broadcasted_iota(jnp.int32, (L, L), 0)
    tc = lax.broadcasted_iota(jnp.int32, (L, L), 1)
    tril_b = (tc <= tr).astype(BF16)
    inv_n = 1.0 / N
    zero_b = jnp.zeros((nh * L, W), BF16)

    def expand(x):
        return jnp.where(same_head, jnp.concatenate([x.astype(BF16)] * nh, axis=0), zero_b)

    groups = range(nb * ng)
    bat = [g // ng for g in groups]
    cols = [slice((g % ng) * W, (g % ng + 1) * W) for g in groups]
    load = lambda ref: [ref[bat[g], :, cols[g]].astype(F32) for g in groups]
    r, k, v, lw, kk, a_sig = (load(ref) for ref in (r_ref, k_ref, v_ref, lw_ref, kk_ref, a_ref))
    sums = _group_sums([jnp.concatenate([kk[g] * kk[g], r[g] * k[g] * rk_ref[:, cols[g]]], axis=0)
                        for g in groups], ones_bd)
    kk = [kk[g] / jnp.maximum(jnp.sqrt(sums[g][:L]), 1e-12) for g in groups]
    bonus = [sums[g][L:] * v[g] for g in groups]
    bv = [kk[g] * a_sig[g] for g in groups]
    cum = [sum(jnp.dot(tril_b, piece, preferred_element_type=F32) for piece in _split3(x))
           for x in lw]
    cum_last = [x[L - 1:L, :] for x in cum]
    ar = [jnp.concatenate([-jnp.exp(cum[g] - lw[g]) * kk[g], jnp.exp(cum[g]) * r[g]],
                          axis=0).astype(BF16) for g in groups]
    p_inv = [jnp.exp(-x) for x in cum]
    b_e = [expand(p_inv[g] * bv[g]) for g in groups]
    k_e = [expand(p_inv[g] * k[g]) for g in groups]
    v_e = [expand(x) for x in v]
    gb = [_dot_nt(ar[g], b_e[g]) for g in groups]
    gk = [_dot_nt(ar[g], k_e[g]) for g in groups]
    m_ab = [jnp.where(strict, x[:L], 0.0) for x in gb]
    m_rb = [jnp.where(incl, x[L:], 0.0) for x in gb]
    m_akrk = [jnp.concatenate([jnp.where(strict, x[:L], 0.0), jnp.where(incl, x[L:], 0.0)],
                              axis=0) for x in gk]

    pw = [_dot(x, expand(x)) for x in m_ab]
    tinv = [eye + x for x in m_ab]
    n_sq = int(math.log2(L)) - 1
    for it in range(n_sq):
        pw_e = [expand(x) for x in pw]
        if it + 1 < n_sq:
            both = [_dot(jnp.concatenate([pw[g], tinv[g]], axis=0), pw_e[g]) for g in groups]
            pw = [x[:L] for x in both]
            tinv = [tinv[g] + both[g][L:] for g in groups]
        else:
            tinv = [tinv[g] + _dot(tinv[g], pw_e[g]) for g in groups]

    s0 = [s_scr[g] for g in groups]
    ars = [_dot_nt(ar[g], s0[g]) for g in groups]
    mv = [_dot(m_akrk[g], v_e[g]) for g in groups]
    u = [_dot(tinv[g], expand(ars[g][:L] + mv[g][:L])) for g in groups]
    y = [ars[g][L:] + mv[g][L:] + _dot(m_rb[g], expand(u[g])) for g in groups]
    for g in groups:
        p_out = jnp.exp(cum_last[g] - cum[g])
        upd = _dot_tn(jnp.concatenate([u[g], v[g]], axis=0),
                      jnp.concatenate([p_out * bv[g], p_out * k[g]], axis=0))
        s_scr[g] = s0[g] * jnp.exp(cum_last[g]) + jnp.where(state_bd, upd, 0.0)

    mean = [x * inv_n for x in _group_sums(y, ones_bd)]
    yc = [y[g] - mean[g] for g in groups]
    var = [x * inv_n for x in _group_sums([x * x for x in yc], ones_bd)]
    for g in groups:
        c = cols[g]
        yn = yc[g] * lax.rsqrt(var[g] + GN_EPS) * lnw_ref[:, c] + lnb_ref[:, c]
        out_ref[bat[g], :, c] = ((yn + bonus[g]) * g_ref[bat[g], :, c].astype(F32)).astype(out_ref.dtype)


def _rwkv(x, mix_g, mu, w_r, w_k, w_v, w0, w1, w2, a0, a1, a2, g1, g2, k_k, k_a, r_k,
          ln_w, ln_b, w_o, B, S):
    T, D = x.shape
    pad_cols = lambda w: jnp.pad(w, ((0, 0), (0, LANES - w.shape[1]))).astype(BF16)
    pad_rows = lambda w: jnp.pad(w, ((0, LANES - w.shape[0]), (0, 0))).astype(BF16)
    vec = lambda t: t.reshape(1, D)
    mu8 = jnp.pad(mu, ((0, 8 - mu.shape[0]), (0, 0)))
    tm = min(TM_PROJ, S)
    nt = S // tm
    row_spec = pl.BlockSpec((tm, D), lambda i: (i, 0))
    outs = pl.pallas_call(
        functools.partial(_rw_proj_kernel, tiles_per_seq=nt),
        grid=(T // tm,),
        in_specs=[row_spec, _resident((1, D)), _resident((8, D)),
                  _resident((D, D)), _resident((D, D)), _resident((D, D)),
                  _resident((D, LANES)), _resident((LANES, D)),
                  _resident((D, LANES)), _resident((LANES, D)),
                  _resident((D, LANES)), _resident((LANES, D)),
                  _resident((1, D)), _resident((1, D)), _resident((1, D)), _resident((1, D))],
        out_specs=[row_spec] * 7,
        out_shape=[jax.ShapeDtypeStruct((T, D), F32 if i == 3 else BF16) for i in range(7)],
        scratch_shapes=[pltpu.VMEM((8, D), F32)],
        compiler_params=_params(1),
        name="rwkv_proj",
    )(x, vec(mix_g), mu8, w_r.astype(BF16), w_k.astype(BF16), w_v.astype(BF16),
      pad_cols(w1), pad_rows(w2), pad_cols(a1), pad_rows(a2), pad_cols(g1), pad_rows(g2),
      vec(w0), vec(a0), vec(k_k), vec(k_a))
    r, k, v, lw, a, kkraw, g = outs

    L = min(RW_CHUNK, S)
    nc = S // L
    W = RW_GROUP * RW_HEAD
    assert L == RW_HEAD
    chunk_spec = pl.BlockSpec((B, L, D), lambda c: (0, c, 0))
    seq = lambda t: t.reshape(B, S, D)
    yg = pl.pallas_call(
        _rw_chunk_kernel,
        grid=(nc,),
        in_specs=[chunk_spec] * 7 + [_resident((1, D))] * 3,
        out_specs=chunk_spec,
        out_shape=jax.ShapeDtypeStruct((B, S, D), BF16),
        scratch_shapes=[pltpu.VMEM((B * D // W, W, W), F32)],
        compiler_params=_params(1),
        name="rwkv_chunk",
    )(seq(r), seq(k), seq(v), seq(lw), seq(a), seq(kkraw), seq(g), vec(r_k), vec(ln_w), vec(ln_b))
    return yg.reshape(T, D), w_o.astype(BF16)


def kernel(x, positions, ffn_norm, ffn_w_gate, ffn_w_up, ffn_w_down, mix_norm, final_norm, mla_w_a, mla_q_norm, mla_w_qb, mla_kv_norm, mla_w_kvb, mla_w_o, ml_w_in, ml_b_if, ml_conv_w, ml_conv_b, ml_out_norm, ml_w_o, rw_mu, rw_w_r, rw_w_k, rw_w_v, rw_w0, rw_w1, rw_w2, rw_a0, rw_a1, rw_a2, rw_g1, rw_g2, rw_k_k, rw_k_a, rw_r_k, rw_ln_w, rw_ln_b, rw_w_o):
    B, S, D = x.shape
    depth = mix_norm.shape[0]
    n_mixers = 3
    h = x.reshape(B * S, D)

    wg, wu, wd = (w.astype(BF16) for w in (ffn_w_gate, ffn_w_up, ffn_w_down))

    def ffn(h, layer, half, **kw):
        return _ffn(h, ffn_norm[layer, half], wg, wu, wd, (layer, half), **kw)

    for layer in range(depth):
        h = ffn(h, layer, 0)
        kind, j = layer % n_mixers, layer // n_mixers
        if kind == 0:
            mixer = _mla(h, positions, mix_norm[layer], mla_w_a[j], mla_q_norm[j], mla_w_qb[j],
                         mla_kv_norm[j], mla_w_kvb[j], mla_w_o[j], B, S)
        elif kind == 1:
            mixer = _mlstm(h, mix_norm[layer], ml_w_in[j], ml_b_if[j], ml_conv_w[j], ml_conv_b[j],
                           ml_out_norm[j], ml_w_o[j], B, S)
        else:
            mixer = _rwkv(h, mix_norm[layer], rw_mu[j], rw_w_r[j], rw_w_k[j], rw_w_v[j], rw_w0[j],
                          rw_w1[j], rw_w2[j], rw_a0[j], rw_a1[j], rw_a2[j], rw_g1[j], rw_g2[j],
                          rw_k_k[j], rw_k_a[j], rw_r_k[j], rw_ln_w[j], rw_ln_b[j], rw_w_o[j], B, S)
        h = ffn(h, layer, 1, mixer=mixer, final_g=final_norm if layer == depth - 1 else None)
    return h.reshape(B, S, D)
```

```python
import functools
import math

import jax
import jax.numpy as jnp
from jax import lax
from jax.experimental import pallas as pl
from jax.experimental.pallas import tpu as pltpu

F32 = jnp.float32
BF16 = jnp.bfloat16

RMS_EPS = 1e-6
GN_EPS = 64e-5
LANES = 128
MXU_DIM = 256
V7X_VMEM_BYTES = 64 * 1024 * 1024
VMEM_LIMIT = V7X_VMEM_BYTES * 7 // 8

MLA_HEADS = 8
MLA_Q_RANK = 512
MLA_KV_RANK = 256
MLA_NOPE = 128
MLA_ROPE = 64
MLA_V = 128
ROPE_THETA = 10000.0
ML_HEADS = 8
ML_QK = 64
ML_V = 128
ML_CONV = 4
RW_HEAD = 64

TM_FFN = 512
FFN_SUBTILES = 2
TM_PROJ = 512
TQ_ATTN = 512
V_PAD = 16
ATTN_HEADS_PER_STEP = 4
ML_CHUNK = 256
RW_CHUNK = 64
RW_GROUP = MXU_DIM // RW_HEAD


def _params(n_axes):
    return pltpu.CompilerParams(dimension_semantics=("arbitrary",) * n_axes,
                                vmem_limit_bytes=VMEM_LIMIT)


def _resident(shape):
    nd = len(shape)
    return pl.BlockSpec(shape, lambda *_: (0,) * nd, pipeline_mode=pl.Buffered(1))


def _rms(x, g, eps=RMS_EPS):
    return x * lax.rsqrt(jnp.mean(x * x, axis=-1, keepdims=True) + eps) * g


def _dot(a, b):
    return jnp.dot(a.astype(BF16), b.astype(BF16), preferred_element_type=F32)


def _dot_nt(a, b):
    return lax.dot_general(a.astype(BF16), b.astype(BF16), (((1,), (1,)), ((), ())),
                           preferred_element_type=F32)


def _dot_tn(a, b):
    return lax.dot_general(a.astype(BF16), b.astype(BF16), (((0,), (0,)), ((), ())),
                           preferred_element_type=F32)


def _split3(x):
    hi = x.astype(BF16)
    r1 = x - hi.astype(F32)
    mid = r1.astype(BF16)
    lo = (r1 - mid.astype(F32)).astype(BF16)
    return hi, mid, lo


def _shift_rows(x, tail, shift):
    rolled = pltpu.roll(x, shift, 0)
    row = lax.broadcasted_iota(jnp.int32, tail.shape, 0)
    head = jnp.where(row < shift, pltpu.roll(tail, shift, 0), rolled[0:8])
    return jnp.concatenate([head, rolled[8:]], axis=0)


def _softplus(y):
    return jnp.maximum(y, 0.0) + jnp.log1p(jnp.exp(-jnp.abs(y)))


def _sigmoid(y):
    return 1.0 / (1.0 + jnp.exp(-y))


def _ffn_kernel(*refs, mixer, final):
    x_ref, g_ref, wg_ref, wu_ref, wd_ref = refs[:5]
    o_ref = refs[-1]
    tm = x_ref.shape[0]
    subs = [slice(k * (tm // FFN_SUBTILES), (k + 1) * (tm // FFN_SUBTILES)) for k in range(FFN_SUBTILES)]
    x = [x_ref[r, :] for r in subs]
    if mixer:
        a_ref, wo_ref = refs[5:7]
        x = [x[k] + jnp.dot(a_ref[r, :], wo_ref[...], preferred_element_type=F32)
             for k, r in enumerate(subs)]
    h = [_rms(t, g_ref[...]).astype(BF16) for t in x]
    gate = [jnp.dot(t, wg_ref[...], preferred_element_type=F32) for t in h]
    up = [jnp.dot(t, wu_ref[...], preferred_element_type=F32) for t in h]
    act = [(g * _sigmoid(g) * u).astype(BF16) for g, u in zip(gate, up)]
    y = [jnp.dot(t, wd_ref[...], preferred_element_type=F32) for t in act]
    for k, r in enumerate(subs):
        out = x[k] + 0.5 * y[k]
        if final:
            out = _rms(out, refs[-2][...])
        o_ref[r, :] = out


def _ffn(x, g, wg, wu, wd, index, mixer=None, final_g=None):
    T, D = x.shape
    F = wg.shape[-1]
    tm = min(TM_FFN, T)
    row_spec = lambda n: pl.BlockSpec((tm, n), lambda i: (i, 0))
    picked = lambda r, c: pl.BlockSpec((None, None, r, c), lambda i: (*index, 0, 0),
                                       pipeline_mode=pl.Buffered(1))
    args = [x, g.reshape(1, D), wg, wu, wd]
    in_specs = [row_spec(D), _resident((1, D)), picked(D, F), picked(D, F), picked(F, D)]
    if mixer is not None:
        a, w_o = mixer
        args += [a, w_o]
        in_specs += [row_spec(a.shape[1]), _resident(w_o.shape)]
    if final_g is not None:
        args.append(final_g.reshape(1, D))
        in_specs.append(_resident((1, D)))
    return pl.pallas_call(
        functools.partial(_ffn_kernel, mixer=mixer is not None, final=final_g is not None),
        grid=(T // tm,),
        in_specs=in_specs,
        out_specs=row_spec(D),
        out_shape=jax.ShapeDtypeStruct((T, D), F32),
        compiler_params=_params(1),
        name="ffn",
    )(*args)


def _rope_tables(pos_col, invf):
    tm = pos_col.shape[0]
    nfreq = MLA_ROPE // 2
    nblk = LANES // nfreq
    rows = tm // nblk
    posb = jnp.broadcast_to(pos_col.astype(F32), (tm, LANES))
    lane_blk = lax.broadcasted_iota(jnp.int32, (rows, LANES), 1) // nfreq
    packed = posb[0:rows]
    for c in range(1, nblk):
        packed = jnp.where(lane_blk == c, posb[c * rows:(c + 1) * rows], packed)
    ang = packed * invf

    def spread(t):
        rolled = [t] + [pltpu.roll(t, nfreq * k, 1) for k in range(1, nblk)]
        blocks = []
        for c in range(nblk):
            out = rolled[(0 - c) % nblk]
            for dst in range(1, nblk):
                out = jnp.where(lane_blk == dst, rolled[(dst - c) % nblk], out)
            blocks.append(out)
        return jnp.concatenate(blocks, axis=0)

    return spread(jnp.cos(ang)), spread(jnp.sin(ang))


def _mla_proj_kernel(x_ref, g_ref, pos_ref, invf_ref, sgn_ref, wa_ref, qn_ref, wqb_ref,
                     kvn_ref, wkvb_ref, qt_ref, kn_ref, kx_ref, vt_ref, *, q_scale):
    H = MLA_HEADS
    h = _rms(x_ref[...], g_ref[...]).astype(BF16)
    lat = jnp.dot(h, wa_ref[...], preferred_element_type=F32)
    cq = _rms(lat[:, :MLA_Q_RANK], qn_ref[...]).astype(BF16)
    ckv = _rms(lat[:, MLA_Q_RANK:MLA_Q_RANK + MLA_KV_RANK], kvn_ref[...]).astype(BF16)
    q = jnp.dot(cq, wqb_ref[...], preferred_element_type=F32) * q_scale
    kv = jnp.dot(ckv, wkvb_ref[...], preferred_element_type=F32)
    cos, sin = _rope_tables(pos_ref[...], invf_ref[...])
    sin = sin * sgn_ref[...]
    kpe0 = MLA_Q_RANK + MLA_KV_RANK
    kx = lat[:, kpe0:kpe0 + LANES] * cos + lat[:, kpe0 + LANES:kpe0 + 2 * LANES] * sin
    kx_ref[0] = kx.astype(BF16)
    half = lax.broadcasted_iota(jnp.int32, kx.shape, 1) // MLA_ROPE
    pe0 = H * MLA_NOPE
    sw0 = pe0 + H * MLA_ROPE
    for hd in range(H):
        p, m = divmod(hd, LANES // MLA_ROPE)
        qpe = (q[:, pe0 + p * LANES:pe0 + (p + 1) * LANES] * cos
               + q[:, sw0 + p * LANES:sw0 + (p + 1) * LANES] * sin)
        sel = half == m
        qt_ref[0, hd, 0:LANES, :] = q[:, hd * MLA_NOPE:(hd + 1) * MLA_NOPE].T.astype(BF16)
        qt_ref[0, hd, LANES:2 * LANES, :] = jnp.where(sel, qpe, 0.0).T.astype(BF16)
        k0 = hd * (MLA_NOPE + MLA_V)
        kn_ref[0, hd] = kv[:, k0:k0 + MLA_NOPE].astype(BF16)
        vt_ref[0, hd, 0:MLA_V, :] = kv[:, k0 + MLA_NOPE:k0 + MLA_NOPE + MLA_V].T.astype(BF16)
        vt_ref[0, hd, MLA_V:MLA_V + V_PAD, :] = jnp.ones((V_PAD, kv.shape[0]), BF16)


def _attn_kernel(qt_ref, kn_ref, kx_ref, vt_ref, o_ref, m_scr, acc_scr, *, tq):
    i = pl.program_id(2)
    heads = range(qt_ref.shape[1])
    dv = o_ref.shape[1] // qt_ref.shape[1]
    qt = [qt_ref[0, h] for h in heads]
    for h in heads:
        m_scr[h] = jnp.full(m_scr.shape[1:], -jnp.inf, F32)
        acc_scr[h] = jnp.zeros(acc_scr.shape[1:], F32)

    def keys(h, start, n):
        return jnp.concatenate([kn_ref[0, h, pl.ds(start, n), :], kx_ref[0, pl.ds(start, n), :]],
                               axis=1)

    def update(h, m_new, pv):
        m_old = m_scr[h, 0:1, :]
        m_scr[h, 0:1, :] = m_new
        acc_scr[h] = jnp.exp2(m_old - m_new) * acc_scr[h] + pv

    def step(j):
        start = pl.multiple_of(j * tq, tq)
        s = [jnp.dot(keys(h, start, tq), qt[h], preferred_element_type=F32)
             for h in heads]
        m_new = [jnp.maximum(m_scr[h, 0:1, :], jnp.max(s[h], axis=0, keepdims=True)) for h in heads]
        p = [jnp.exp2(s[h] - m_new[h]).astype(BF16) for h in heads]
        for h in heads:
            update(h, m_new[h], jnp.dot(vt_ref[0, h, :, pl.ds(start, tq)], p[h],
                                        preferred_element_type=F32))

    def diagonal_step(j):
        half = tq // 2
        start = pl.multiple_of(j * tq, tq)
        mid = pl.multiple_of(j * tq + half, half)
        key = lax.broadcasted_iota(jnp.int32, (half, tq), 0)
        qry = lax.broadcasted_iota(jnp.int32, (half, tq), 1)
        keep_top = key <= qry
        keep_bot = keep_top[:, :half]
        s_top = [jnp.where(keep_top, jnp.dot(keys(h, start, half), qt[h],
                                             preferred_element_type=F32), -jnp.inf) for h in heads]
        s_bot = [jnp.where(keep_bot, jnp.dot(keys(h, mid, half), qt[h][:, half:],
                                             preferred_element_type=F32), -jnp.inf) for h in heads]
        m_top = [jnp.max(x, axis=0, keepdims=True) for x in s_top]
        m_bot = [jnp.max(x, axis=0, keepdims=True) for x in s_bot]
        m_new = [jnp.maximum(m_scr[h, 0:1, :], jnp.concatenate(
            [m_top[h][:, :half], jnp.maximum(m_top[h][:, half:], m_bot[h])], axis=1)) for h in heads]
        p_top = [jnp.exp2(s_top[h] - m_new[h]).astype(BF16) for h in heads]
        p_bot = [jnp.exp2(s_bot[h] - m_new[h][:, half:]).astype(BF16) for h in heads]
        for h in heads:
            pv_top = jnp.dot(vt_ref[0, h, :, pl.ds(start, half)], p_top[h], preferred_element_type=F32)
            pv_bot = jnp.dot(vt_ref[0, h, :, pl.ds(mid, half)], p_bot[h], preferred_element_type=F32)
            update(h, m_new[h], pv_top + jnp.concatenate([jnp.zeros_like(pv_bot), pv_bot], axis=1))

    def pair(jj, c):
        step(2 * jj)
        step(2 * jj + 1)
        return c

    lax.fori_loop(0, i // 2, pair, 0)

    @pl.when(i % 2 == 1)
    def _():
        step(i - 1)

    diagonal_step(i)
    for h in heads:
        acc = acc_scr[h]
        o_ref[:, h * dv:(h + 1) * dv] = (acc[:dv] / acc[dv:dv + 1]).T.astype(o_ref.dtype)


def _mla(x, positions, mix_g, w_a, q_norm, w_qb, kv_norm, w_kvb, w_o, B, S):
    T, D = x.shape
    H = MLA_HEADS
    kpe = w_a[:, MLA_Q_RANK + MLA_KV_RANK:]
    kpe_sw = jnp.concatenate([kpe[:, MLA_ROPE // 2:], kpe[:, :MLA_ROPE // 2]], axis=1)
    reps = LANES // MLA_ROPE
    wa2 = jnp.concatenate([w_a[:, :MLA_Q_RANK + MLA_KV_RANK], jnp.tile(kpe, (1, reps)),
                           jnp.tile(kpe_sw, (1, reps))], axis=1).astype(BF16)
    wq3 = w_qb.reshape(MLA_Q_RANK, H, MLA_NOPE + MLA_ROPE)
    q_nope = wq3[:, :, :MLA_NOPE].reshape(MLA_Q_RANK, H * MLA_NOPE)
    q_pe = wq3[:, :, MLA_NOPE:]
    q_pe_sw = jnp.concatenate([q_pe[:, :, MLA_ROPE // 2:], q_pe[:, :, :MLA_ROPE // 2]], axis=2)
    wqb2 = jnp.concatenate([q_nope, q_pe.reshape(MLA_Q_RANK, H * MLA_ROPE),
                            q_pe_sw.reshape(MLA_Q_RANK, H * MLA_ROPE)], axis=1).astype(BF16)
    inv_freq = 1.0 / (ROPE_THETA ** (jnp.arange(0, MLA_ROPE, 2, dtype=F32) / MLA_ROPE))
    invf = jnp.tile(inv_freq, LANES // (MLA_ROPE // 2)).reshape(1, LANES)
    first_half = (jnp.arange(LANES) % MLA_ROPE) < MLA_ROPE // 2
    sgn = jnp.where(first_half, -1.0, 1.0).astype(F32).reshape(1, LANES)
    q_scale = (MLA_NOPE + MLA_ROPE) ** -0.5 * math.log2(math.e)

    tm = min(TM_PROJ, S)
    nt = S // tm
    wa_cols = wa2.shape[1]
    wq_cols = wqb2.shape[1]
    kv_cols = w_kvb.shape[1]
    qk_dim = 2 * LANES
    q, kn, kx, v = pl.pallas_call(
        functools.partial(_mla_proj_kernel, q_scale=q_scale),
        grid=(B, nt),
        in_specs=[pl.BlockSpec((tm, D), lambda b, i: (b * nt + i, 0)),
                  _resident((1, D)),
                  pl.BlockSpec((tm, 1), lambda b, i: (b * nt + i, 0)),
                  _resident((1, LANES)), _resident((1, LANES)),
                  _resident((D, wa_cols)), _resident((1, MLA_Q_RANK)),
                  _resident((MLA_Q_RANK, wq_cols)), _resident((1, MLA_KV_RANK)),
                  _resident((MLA_KV_RANK, kv_cols))],
        out_specs=[pl.BlockSpec((1, H, qk_dim, tm), lambda b, i: (b, 0, 0, i)),
                   pl.BlockSpec((1, H, tm, MLA_NOPE), lambda b, i: (b, 0, i, 0)),
                   pl.BlockSpec((1, tm, LANES), lambda b, i: (b, i, 0)),
                   pl.BlockSpec((1, H, MLA_V + V_PAD, tm), lambda b, i: (b, 0, 0, i))],
        out_shape=[jax.ShapeDtypeStruct((B, H, qk_dim, S), BF16),
                   jax.ShapeDtypeStruct((B, H, S, MLA_NOPE), BF16),
                   jax.ShapeDtypeStruct((B, S, LANES), BF16),
                   jax.ShapeDtypeStruct((B, H, MLA_V + V_PAD, S), BF16)],
        compiler_params=_params(2),
        name="mla_proj",
    )(x, mix_g.reshape(1, D), positions.reshape(T, 1), invf, sgn, wa2, q_norm.reshape(1, -1),
      wqb2, kv_norm.reshape(1, -1), w_kvb.astype(BF16))

    tq = min(TQ_ATTN, S)
    nq = S // tq
    hp = ATTN_HEADS_PER_STEP
    o = pl.pallas_call(
        functools.partial(_attn_kernel, tq=tq),
        grid=(B, H // hp, nq),
        in_specs=[pl.BlockSpec((1, hp, qk_dim, tq), lambda b, h, i: (b, h, 0, i)),
                  pl.BlockSpec((1, hp, S, MLA_NOPE), lambda b, h, i: (b, h, 0, 0),
                               pipeline_mode=pl.Buffered(1)),
                  pl.BlockSpec((1, S, LANES), lambda b, h, i: (b, 0, 0),
                               pipeline_mode=pl.Buffered(1)),
                  pl.BlockSpec((1, hp, MLA_V + V_PAD, S), lambda b, h, i: (b, h, 0, 0),
                               pipeline_mode=pl.Buffered(1))],
        out_specs=pl.BlockSpec((tq, hp * MLA_V), lambda b, h, i: (b * nq + i, h)),
        out_shape=jax.ShapeDtypeStruct((T, H * MLA_V), BF16),
        scratch_shapes=[pltpu.VMEM((hp, 8, tq), F32), pltpu.VMEM((hp, MLA_V + V_PAD, tq), F32)],
        compiler_params=_params(3),
        name="mla_attn",
    )(q, kn, kx, v)
    return o, w_o.astype(BF16)


def _ml_proj_kernel(x_ref, g_ref, w_ref, wg_ref, bif_ref, cw_ref, cb_ref,
                    qt_ref, k_ref, vt_ref, o_ref, gates_ref, pbuf, *, tiles_per_seq):
    tm = x_ref.shape[0]
    QK = ML_HEADS * ML_QK
    V = ML_HEADS * ML_V

    @pl.when(pl.program_id(0) % tiles_per_seq == 0)
    def _():
        pbuf[...] = jnp.zeros(pbuf.shape, F32)

    h = _rms(x_ref[...], g_ref[...]).astype(BF16)
    w = w_ref[...]
    p_vo = jnp.dot(h, w[:, 2 * QK:], preferred_element_type=F32)
    gates_ref[...] = jnp.dot(h, wg_ref[...], preferred_element_type=F32) + bif_ref[...]
    vt_ref[...] = p_vo[:, :V].T.astype(BF16)
    o_ref[...] = p_vo[:, V:]
    pqk = jnp.dot(h, w[:, :2 * QK], preferred_element_type=F32)
    tail = pbuf[...]
    acc = cb_ref[...] + cw_ref[ML_CONV - 1:ML_CONV, :] * pqk
    for kk in range(ML_CONV - 1):
        acc = acc + cw_ref[kk:kk + 1, :] * _shift_rows(pqk, tail, ML_CONV - 1 - kk)
    pbuf[...] = pqk[tm - 8:tm, :]
    qk = acc * _sigmoid(acc)
    qt_ref[...] = qk[:, :QK].T.astype(BF16)
    k_ref[...] = (qk[:, QK:] * ML_QK ** -0.5).astype(BF16)


def _ml_chunk_kernel(qt_ref, k_ref, vt_ref, o_ref, g_ref, on_ref, out_ref, c_scr, m_scr):
    L = k_ref.shape[0]
    H = ML_HEADS
    pair = LANES // ML_QK

    @pl.when(pl.program_id(1) == 0)
    def _():
        c_scr[...] = jnp.zeros(c_scr.shape, F32)
        m_scr[...] = jnp.zeros(m_scr.shape, F32)

    G = g_ref[...]
    GT = G.T
    lfG = -_softplus(-G)
    lfGT = -_softplus(-GT)
    row = lax.broadcasted_iota(jnp.int32, (L, L), 0)
    col = lax.broadcasted_iota(jnp.int32, (L, L), 1)
    causal = row <= col
    tril_b = (col <= row).astype(BF16)
    triu_b = causal.astype(BF16)
    fc_cols = sum(jnp.dot(tril_b, piece, preferred_element_type=F32) for piece in _split3(lfG))
    fc_rows = sum(jnp.dot(piece, triu_b, preferred_element_type=F32) for piece in _split3(lfGT))
    m_row = m_scr[...]
    ones_rows = (lax.broadcasted_iota(jnp.int32, (ML_V, L), 0) == 0).astype(BF16)
    row_half = lax.broadcasted_iota(jnp.int32, (LANES, L), 0) // ML_QK
    lane_half = lax.broadcasted_iota(jnp.int32, (L, LANES), 1) // ML_QK
    slane_half = lax.broadcasted_iota(jnp.int32, (1, LANES), 1) // ML_QK
    lane1 = lax.broadcasted_iota(jnp.int32, (1, LANES), 1)
    heads = range(H)
    pairs = range(H // pair)
    pr = [hd // pair for hd in heads]
    mm = [hd % pair for hd in heads]
    fc_row = [fc_rows[H + hd:H + hd + 1, :] for hd in heads]
    i_row = [GT[hd:hd + 1, :] for hd in heads]
    c_col = [G[:, hd:hd + 1] - fc_cols[:, H + hd:H + hd + 1] for hd in heads]
    m_prev = [m_row[:, hd:hd + 1] for hd in heads]
    dmat = [jnp.where(causal, fc_row[hd] + c_col[hd], -jnp.inf) for hd in heads]
    inter = [fc_row[hd] + m_prev[hd] for hd in heads]
    m_t = [jnp.maximum(inter[hd], jnp.max(dmat[hd], axis=0, keepdims=True)) for hd in heads]
    dw = [jnp.exp(dmat[hd] - m_t[hd]) for hd in heads]
    inter_w = [jnp.exp(inter[hd] - m_t[hd]) for hd in heads]
    qtp = [qt_ref[p * LANES:(p + 1) * LANES, :] for p in pairs]
    kp = [k_ref[:, p * LANES:(p + 1) * LANES] for p in pairs]
    qtm = [jnp.where(row_half == mm[hd], qtp[pr[hd]], jnp.zeros_like(qtp[0])) for hd in heads]
    km = [jnp.where(lane_half == mm[hd], kp[pr[hd]], jnp.zeros_like(kp[0])) for hd in heads]
    s = [_dot(kp[pr[hd]], qtm[hd]) * dw[hd] for hd in heads]
    vht = [jnp.concatenate([vt_ref[hd * ML_V:(hd + 1) * ML_V, :], ones_rows], axis=0)
           for hd in heads]
    ct = [c_scr[p] for p in pairs]
    nd = [_dot(vht[hd], s[hd]) + inter_w[hd] * _dot(ct[pr[hd]], qtm[hd]) for hd in heads]
    hid = [nd[hd][:ML_V] / jnp.maximum(jnp.abs(nd[hd][ML_V:ML_V + 1]), jnp.exp(-m_t[hd]))
           for hd in heads]
    hid = [x * lax.rsqrt(jnp.mean(x * x, axis=0, keepdims=True) + RMS_EPS) for x in hid]
    for hd in heads:
        cols = slice(hd * ML_V, (hd + 1) * ML_V)
        out_ref[:, cols] = (hid[hd].T * on_ref[:, cols] * _sigmoid(o_ref[:, cols])).astype(out_ref.dtype)
    m_new = [m_t[hd][:, L - 1:L] for hd in heads]
    fc_last = [fc_row[hd][:, L - 1:L] for hd in heads]
    decay = [jnp.exp(fc_last[hd] + m_prev[hd] - m_new[hd]) for hd in heads]
    w_row = [jnp.exp(fc_last[hd] - fc_row[hd] + i_row[hd] - m_new[hd]) for hd in heads]
    upd = [_dot(w_row[hd] * vht[hd].astype(F32), km[hd]) for hd in heads]
    for p in pairs:
        scale = decay[p * pair]
        for k in range(1, pair):
            scale = jnp.where(slane_half == k, decay[p * pair + k], scale)
        c_scr[p] = scale * ct[p] + sum(upd[p * pair + k] for k in range(pair))
    m_out = m_row
    for hd in heads:
        m_out = jnp.where(lane1 == hd, m_new[hd], m_out)
    m_scr[...] = m_out


def _mlstm(x, mix_g, w_in, b_if, conv_w, conv_b, out_norm, w_o, B, S):
    T, D = x.shape
    H = ML_HEADS
    QK = H * ML_QK
    V = H * ML_V
    main = 2 * QK + 2 * V
    w_main = w_in[:, :main].astype(BF16)
    w_gate = jnp.pad(w_in[:, main:], ((0, 0), (0, LANES - 2 * H))).astype(BF16)
    bif = jnp.pad(b_if, (0, LANES - 2 * H)).reshape(1, LANES)
    tm = min(TM_PROJ, S)
    nt = S // tm
    row_spec = lambda n: pl.BlockSpec((tm, n), lambda i: (i, 0))
    col_spec = lambda n: pl.BlockSpec((n, tm), lambda i: (0, i))
    qt, k, vt, o, gates = pl.pallas_call(
        functools.partial(_ml_proj_kernel, tiles_per_seq=nt),
        grid=(T // tm,),
        in_specs=[row_spec(D), _resident((1, D)), _resident((D, main)), _resident((D, LANES)),
                  _resident((1, LANES)), _resident((ML_CONV, 2 * QK)), _resident((1, 2 * QK))],
        out_specs=[col_spec(QK), row_spec(QK), col_spec(V), row_spec(V), row_spec(LANES)],
        out_shape=[jax.ShapeDtypeStruct((QK, T), BF16), jax.ShapeDtypeStruct((T, QK), BF16),
                   jax.ShapeDtypeStruct((V, T), BF16), jax.ShapeDtypeStruct((T, V), F32),
                   jax.ShapeDtypeStruct((T, LANES), F32)],
        scratch_shapes=[pltpu.VMEM((8, 2 * QK), F32)],
        compiler_params=_params(1),
        name="mlstm_proj",
    )(x, mix_g.reshape(1, D), w_main, w_gate, bif, conv_w, conv_b.reshape(1, -1))

    L = min(ML_CHUNK, S)
    nc = S // L
    chunk_spec = lambda n: pl.BlockSpec((L, n), lambda b, c: (b * nc + c, 0))
    chunk_t_spec = lambda n: pl.BlockSpec((n, L), lambda b, c: (0, b * nc + c))
    hid = pl.pallas_call(
        _ml_chunk_kernel,
        grid=(B, nc),
        in_specs=[chunk_t_spec(QK), chunk_spec(QK), chunk_t_spec(V), chunk_spec(V), chunk_spec(LANES),
                  _resident((1, V))],
        out_specs=chunk_spec(V),
        out_shape=jax.ShapeDtypeStruct((T, V), BF16),
        scratch_shapes=[pltpu.VMEM((H * ML_QK // LANES, 2 * ML_V, LANES), F32),
                        pltpu.VMEM((1, LANES), F32)],
        compiler_params=_params(2),
        name="mlstm_chunk",
    )(qt, k, vt, o, gates, out_norm.reshape(1, V))
    return hid, w_o.astype(BF16)


def _rw_proj_kernel(x_ref, g_ref, mu_ref, wr_ref, wk_ref, wv_ref, w1_ref, w2_ref, a1_ref, a2_ref,
                    g1_ref, g2_ref, w0_ref, a0_ref, kk_ref, ka_ref,
                    r_out, k_out, v_out, lw_out, a_out, kkraw_out, g_out, hbuf, *, tiles_per_seq):
    tm = x_ref.shape[0]
    D = x_ref.shape[1]

    @pl.when(pl.program_id(0) % tiles_per_seq == 0)
    def _():
        hbuf[...] = jnp.zeros(hbuf.shape, F32)

    h = _rms(x_ref[...], g_ref[...])
    xx = _shift_rows(h, hbuf[...], 1) - h
    hbuf[...] = h[tm - 8:tm, :]
    mix = lambda j: (h + xx * mu_ref[j:j + 1, :]).astype(BF16)
    r = jnp.dot(mix(0), wr_ref[...], preferred_element_type=F32)
    k = jnp.dot(mix(2), wk_ref[...], preferred_element_type=F32)
    v = jnp.dot(mix(3), wv_ref[...], preferred_element_type=F32)
    zw = w0_ref[...] + _dot(jnp.tanh(jnp.dot(mix(1), w1_ref[...], preferred_element_type=F32)),
                            w2_ref[...])
    log_decay = -math.exp(-0.5) * _sigmoid(zw)
    a = _sigmoid(a0_ref[...] + _dot(jnp.dot(mix(4), a1_ref[...], preferred_element_type=F32),
                                    a2_ref[...]))
    g = _dot(_sigmoid(jnp.dot(mix(5), g1_ref[...], preferred_element_type=F32)), g2_ref[...])
    r_out[...] = r.astype(r_out.dtype)
    v_out[...] = v.astype(v_out.dtype)
    lw_out[...] = log_decay
    a_out[...] = a.astype(a_out.dtype)
    kkraw_out[...] = (k * kk_ref[...]).astype(kkraw_out.dtype)
    k_out[...] = (k * (1.0 + (a - 1.0) * ka_ref[...])).astype(k_out.dtype)
    g_out[...] = g.astype(g_out.dtype)


def _group_sums(xs, ones_bd):
    n = xs[0].shape[0]
    pieces = []
    for x in xs:
        hi = x.astype(BF16)
        pieces += [hi, (x - hi.astype(F32)).astype(BF16)]
    both = jnp.dot(jnp.concatenate(pieces, axis=0), ones_bd, preferred_element_type=F32)
    return [both[2 * i * n:(2 * i + 1) * n] + both[(2 * i + 1) * n:(2 * i + 2) * n]
            for i in range(len(xs))]


def _rw_chunk_kernel(r_ref, k_ref, v_ref, lw_ref, a_ref, kk_ref, g_ref, rk_ref, lnw_ref, lnb_ref,
                     out_ref, s_scr):
    nb, L = r_ref.shape[0], r_ref.shape[1]
    W = s_scr.shape[1]
    N = RW_HEAD
    nh = W // N
    ng = r_ref.shape[2] // W

    @pl.when(pl.program_id(0) == 0)
    def _():
        s_scr[...] = jnp.zeros(s_scr.shape, F32)

    er = lax.broadcasted_iota(jnp.int32, (nh * L, W), 0)
    ec = lax.broadcasted_iota(jnp.int32, (nh * L, W), 1)
    same_head = (er // L) == (ec // N)
    sr = lax.broadcasted_iota(jnp.int32, (W, W), 0)
    sc = lax.broadcasted_iota(jnp.int32, (W, W), 1)
    state_bd = (sr // N) == (sc // N)
    ones_bd = state_bd.astype(BF16)
    lt = lax.broadcasted_iota(jnp.int32, (L, W), 0)
    ls = lax.broadcasted_iota(jnp.int32, (L, W), 1) % L
    strict = ls < lt
    incl = ls <= lt
    eye = (ls == lt).astype(F32)
    tr = lax.broadcasted_iota(jnp.int32, (L, L), 0)
    tc = lax.broadcasted_iota(jnp.int32, (L, L), 1)
    tril_b = (tc <= tr).astype(BF16)
    inv_n = 1.0 / N
    zero_b = jnp.zeros((nh * L, W), BF16)

    def expand(x):
        return jnp.where(same_head, jnp.concatenate([x.astype(BF16)] * nh, axis=0), zero_b)

    groups = range(nb * ng)
    bat = [g // ng for g in groups]
    cols = [slice((g % ng) * W, (g % ng + 1) * W) for g in groups]
    load = lambda ref: [ref[bat[g], :, cols[g]].astype(F32) for g in groups]
    r, k, v, lw, kk, a_sig = (load(ref) for ref in (r_ref, k_ref, v_ref, lw_ref, kk_ref, a_ref))
    sums = _group_sums([jnp.concatenate([kk[g] * kk[g], r[g] * k[g] * rk_ref[:, cols[g]]], axis=0)
                        for g in groups], ones_bd)
    kk = [kk[g] / jnp.maximum(jnp.sqrt(sums[g][:L]), 1e-12) for g in groups]
    bonus = [sums[g][L:] * v[g] for g in groups]
    bv = [kk[g] * a_sig[g] for g in groups]
    cum = [sum(jnp.dot(tril_b, piece, preferred_element_type=F32) for piece in _split3(x))
           for x in lw]
    cum_last = [x[L - 1:L, :] for x in cum]
    ar = [jnp.concatenate([-jnp.exp(cum[g] - lw[g]) * kk[g], jnp.exp(cum[g]) * r[g]],
                          axis=0).astype(BF16) for g in groups]
    p_inv = [jnp.exp(-x) for x in cum]
    b_e = [expand(p_inv[g] * bv[g]) for g in groups]
    k_e = [expand(p_inv[g] * k[g]) for g in groups]
    v_e = [expand(x) for x in v]
    gb = [_dot_nt(ar[g], b_e[g]) for g in groups]
    gk = [_dot_nt(ar[g], k_e[g]) for g in groups]
    m_ab = [jnp.where(strict, x[:L], 0.0) for x in gb]
    m_rb = [jnp.where(incl, x[L:], 0.0) for x in gb]
    m_akrk = [jnp.concatenate([jnp.where(strict, x[:L], 0.0), jnp.where(incl, x[L:], 0.0)],
                              axis=0) for x in gk]

    pw = [_dot(x, expand(x)) for x in m_ab]
    tinv = [eye + x for x in m_ab]
    n_sq = int(math.log2(L)) - 1
    for it in range(n_sq):
        pw_e = [expand(x) for x in pw]
        if it + 1 < n_sq:
            both = [_dot(jnp.concatenate([pw[g], tinv[g]], axis=0), pw_e[g]) for g in groups]
            pw = [x[:L] for x in both]
            tinv = [tinv[g] + both[g][L:] for g in groups]
        else:
            tinv = [tinv[g] + _dot(tinv[g], pw_e[g]) for g in groups]

    s0 = [s_scr[g] for g in groups]
    ars = [_dot_nt(ar[g], s0[g]) for g in groups]
    mv = [_dot(m_akrk[g], v_e[g]) for g in groups]
    u = [_dot(tinv[g], expand(ars[g][:L] + mv[g][:L])) for g in groups]
    y = [ars[g][L:] + mv[g][L:] + _dot(m_rb[g], expand(u[g])) for g in groups]
    for g in groups:
        p_out = jnp.exp(cum_last[g] - cum[g])
        upd = _dot_tn(jnp.concatenate([u[g], v[g]], axis=0),
                      jnp.concatenate([p_out * bv[g], p_out * k[g]], axis=0))
        s_scr[g] = s0[g] * jnp.exp(cum_last[g]) + jnp.where(state_bd, upd, 0.0)

    mean = [x * inv_n for x in _group_sums(y, ones_bd)]
    yc = [y[g] - mean[g] for g in groups]
    var = [x * inv_n for x in _group_sums([x * x for x in yc], ones_bd)]
    for g in groups:
        c = cols[g]
        yn = yc[g] * lax.rsqrt(var[g] + GN_EPS) * lnw_ref[:, c] + lnb_ref[:, c]
        out_ref[bat[g], :, c] = ((yn + bonus[g]) * g_ref[bat[g], :, c].astype(F32)).astype(out_ref.dtype)


def _rwkv(x, mix_g, mu, w_r, w_k, w_v, w0, w1, w2, a0, a1, a2, g1, g2, k_k, k_a, r_k,
          ln_w, ln_b, w_o, B, S):
    T, D = x.shape
    pad_cols = lambda w: jnp.pad(w, ((0, 0), (0, LANES - w.shape[1]))).astype(BF16)
    pad_rows = lambda w: jnp.pad(w, ((0, LANES - w.shape[0]), (0, 0))).astype(BF16)
    vec = lambda t: t.reshape(1, D)
    mu8 = jnp.pad(mu, ((0, 8 - mu.shape[0]), (0, 0)))
    tm = min(TM_PROJ, S)
    nt = S // tm
    row_spec = pl.BlockSpec((tm, D), lambda i: (i, 0))
    outs = pl.pallas_call(
        functools.partial(_rw_proj_kernel, tiles_per_seq=nt),
        grid=(T // tm,),
        in_specs=[row_spec, _resident((1, D)), _resident((8, D)),
                  _resident((D, D)), _resident((D, D)), _resident((D, D)),
                  _resident((D, LANES)), _resident((LANES, D)),
                  _resident((D, LANES)), _resident((LANES, D)),
                  _resident((D, LANES)), _resident((LANES, D)),
                  _resident((1, D)), _resident((1, D)), _resident((1, D)), _resident((1, D))],
        out_specs=[row_spec] * 7,
        out_shape=[jax.ShapeDtypeStruct((T, D), F32 if i == 3 else BF16) for i in range(7)],
        scratch_shapes=[pltpu.VMEM((8, D), F32)],
        compiler_params=_params(1),
        name="rwkv_proj",
    )(x, vec(mix_g), mu8, w_r.astype(BF16), w_k.astype(BF16), w_v.astype(BF16),
      pad_cols(w1), pad_rows(w2), pad_cols(a1), pad_rows(a2), pad_cols(g1), pad_rows(g2),
      vec(w0), vec(a0), vec(k_k), vec(k_a))
    r, k, v, lw, a, kkraw, g = outs

    L = min(RW_CHUNK, S)
    nc = S // L
    W = RW_GROUP * RW_HEAD
    assert L == RW_HEAD
    chunk_spec = pl.BlockSpec((B, L, D), lambda c: (0, c, 0))
    seq = lambda t: t.reshape(B, S, D)
    yg = pl.pallas_call(
        _rw_chunk_kernel,
        grid=(nc,),
        in_specs=[chunk_spec] * 7 + [_resident((1, D))] * 3,
        out_specs=chunk_spec,
        out_shape=jax.ShapeDtypeStruct((B, S, D), BF16),
        scratch_shapes=[pltpu.VMEM((B * D // W, W, W), F32)],
        compiler_params=_params(1),
        name="rwkv_chunk",
    )(seq(r), seq(k), seq(v), seq(lw), seq(a), seq(kkraw), seq(g), vec(r_k), vec(ln_w), vec(ln_b))
    return yg.reshape(T, D), w_o.astype(BF16)


def kernel(x, positions, ffn_norm, ffn_w_gate, ffn_w_up, ffn_w_down, mix_norm, final_norm, mla_w_a, mla_q_norm, mla_w_qb, mla_kv_norm, mla_w_kvb, mla_w_o, ml_w_in, ml_b_if, ml_conv_w, ml_conv_b, ml_out_norm, ml_w_o, rw_mu, rw_w_r, rw_w_k, rw_w_v, rw_w0, rw_w1, rw_w2, rw_a0, rw_a1, rw_a2, rw_g1, rw_g2, rw_k_k, rw_k_a, rw_r_k, rw_ln_w, rw_ln_b, rw_w_o):
    B, S, D = x.shape
    depth = mix_norm.shape[0]
    n_mixers = 3
    h = x.reshape(B * S, D)

    wg, wu, wd = (w.astype(BF16) for w in (ffn_w_gate, ffn_w_up, ffn_w_down))

    def ffn(h, layer, half, **kw):
        return _ffn(h, ffn_norm[layer, half], wg, wu, wd, (layer, half), **kw)

    for layer in range(depth):
        h = ffn(h, layer, 0)
        kind, j = layer % n_mixers, layer // n_mixers
        if kind == 0:
            mixer = _mla(h, positions, mix_norm[layer], mla_w_a[j], mla_q_norm[j], mla_w_qb[j],
                         mla_kv_norm[j], mla_w_kvb[j], mla_w_o[j], B, S)
        elif kind == 1:
            mixer = _mlstm(h, mix_norm[layer], ml_w_in[j], ml_b_if[j], ml_conv_w[j], ml_conv_b[j],
                           ml_out_norm[j], ml_w_o[j], B, S)
        else:
            mixer = _rwkv(h, mix_norm[layer], rw_mu[j], rw_w_r[j], rw_w_k[j], rw_w_v[j], rw_w0[j],
                          rw_w1[j], rw_w2[j], rw_a0[j], rw_a1[j], rw_a2[j], rw_g1[j], rw_g2[j],
                          rw_k_k[j], rw_k_a[j], rw_r_k[j], rw_ln_w[j], rw_ln_b[j], rw_w_o[j], B, S)
        h = ffn(h, layer, 1, mixer=mixer, final_g=final_norm if layer == depth - 1 else None)
    return h.reshape(B, S, D)
```

```python
import functools
import math

import jax
import jax.numpy as jnp
from jax import lax
from jax.experimental import pallas as pl
from jax.experimental.pallas import tpu as pltpu

F32 = jnp.float32
BF16 = jnp.bfloat16

RMS_EPS = 1e-6
GN_EPS = 64e-5
LANES = 128
MXU_DIM = 256
V7X_VMEM_BYTES = 64 * 1024 * 1024
VMEM_LIMIT = V7X_VMEM_BYTES * 7 // 8

MLA_HEADS = 8
MLA_Q_RANK = 512
MLA_KV_RANK = 256
MLA_NOPE = 128
MLA_ROPE = 64
MLA_V = 128
ROPE_THETA = 10000.0
ML_HEADS = 8
ML_QK = 64
ML_V = 128
ML_CONV = 4
RW_HEAD = 64

TM_FFN = 512
FFN_SUBTILES = 2
TM_PROJ = 512
TQ_ATTN = 512
V_PAD = 16
ATTN_HEADS_PER_STEP = 4
ML_CHUNK = 256
RW_CHUNK = 64
RW_CHUNKS_PER_STEP = 4
RW_GROUP = MXU_DIM // RW_HEAD


def _params(n_axes):
    return pltpu.CompilerParams(dimension_semantics=("arbitrary",) * n_axes,
                                vmem_limit_bytes=VMEM_LIMIT)


def _resident(shape):
    nd = len(shape)
    return pl.BlockSpec(shape, lambda *_: (0,) * nd, pipeline_mode=pl.Buffered(1))


def _rms(x, g, eps=RMS_EPS):
    return x * lax.rsqrt(jnp.mean(x * x, axis=-1, keepdims=True) + eps) * g


def _dot(a, b):
    return jnp.dot(a.astype(BF16), b.astype(BF16), preferred_element_type=F32)


def _dot_nt(a, b):
    return lax.dot_general(a.astype(BF16), b.astype(BF16), (((1,), (1,)), ((), ())),
                           preferred_element_type=F32)


def _dot_tn(a, b):
    return lax.dot_general(a.astype(BF16), b.astype(BF16), (((0,), (0,)), ((), ())),
                           preferred_element_type=F32)


def _split3(x):
    hi = x.astype(BF16)
    r1 = x - hi.astype(F32)
    mid = r1.astype(BF16)
    lo = (r1 - mid.astype(F32)).astype(BF16)
    return hi, mid, lo


def _shift_rows(x, tail, shift):
    rolled = pltpu.roll(x, shift, 0)
    row = lax.broadcasted_iota(jnp.int32, tail.shape, 0)
    head = jnp.where(row < shift, pltpu.roll(tail, shift, 0), rolled[0:8])
    return jnp.concatenate([head, rolled[8:]], axis=0)


def _softplus(y):
    return jnp.maximum(y, 0.0) + jnp.log1p(jnp.exp(-jnp.abs(y)))


def _sigmoid(y):
    return 1.0 / (1.0 + jnp.exp(-y))


def _ffn_kernel(*refs, mixer, final):
    x_ref, g_ref, wg_ref, wu_ref, wd_ref = refs[:5]
    o_ref = refs[-1]
    tm = x_ref.shape[0]
    subs = [slice(k * (tm // FFN_SUBTILES), (k + 1) * (tm // FFN_SUBTILES)) for k in range(FFN_SUBTILES)]
    x = [x_ref[r, :] for r in subs]
    if mixer:
        a_ref, wo_ref = refs[5:7]
        x = [x[k] + jnp.dot(a_ref[r, :], wo_ref[...], preferred_element_type=F32)
             for k, r in enumerate(subs)]
    h = [_rms(t, g_ref[...]).astype(BF16) for t in x]
    gate = [jnp.dot(t, wg_ref[...], preferred_element_type=F32) for t in h]
    up = [jnp.dot(t, wu_ref[...], preferred_element_type=F32) for t in h]
    act = [(g * _sigmoid(g) * u).astype(BF16) for g, u in zip(gate, up)]
    y = [jnp.dot(t, wd_ref[...], preferred_element_type=F32) for t in act]
    for k, r in enumerate(subs):
        out = x[k] + 0.5 * y[k]
        if final:
            out = _rms(out, refs[-2][...])
        o_ref[r, :] = out


def _ffn(x, g, wg, wu, wd, index, mixer=None, final_g=None):
    T, D = x.shape
    F = wg.shape[-1]
    tm = min(TM_FFN, T)
    row_spec = lambda n: pl.BlockSpec((tm, n), lambda i: (i, 0))
    picked = lambda r, c: pl.BlockSpec((None, None, r, c), lambda i: (*index, 0, 0),
                                       pipeline_mode=pl.Buffered(1))
    args = [x, g.reshape(1, D), wg, wu, wd]
    in_specs = [row_spec(D), _resident((1, D)), picked(D, F), picked(D, F), picked(F, D)]
    if mixer is not None:
        a, w_o = mixer
        args += [a, w_o]
        in_specs += [row_spec(a.shape[1]), _resident(w_o.shape)]
    if final_g is not None:
        args.append(final_g.reshape(1, D))
        in_specs.append(_resident((1, D)))
    return pl.pallas_call(
        functools.partial(_ffn_kernel, mixer=mixer is not None, final=final_g is not None),
        grid=(T // tm,),
        in_specs=in_specs,
        out_specs=row_spec(D),
        out_shape=jax.ShapeDtypeStruct((T, D), F32),
        compiler_params=_params(1),
        name="ffn",
    )(*args)


def _rope_tables(pos_col, invf):
    tm = pos_col.shape[0]
    nfreq = MLA_ROPE // 2
    nblk = LANES // nfreq
    rows = tm // nblk
    posb = jnp.broadcast_to(pos_col.astype(F32), (tm, LANES))
    lane_blk = lax.broadcasted_iota(jnp.int32, (rows, LANES), 1) // nfreq
    packed = posb[0:rows]
    for c in range(1, nblk):
        packed = jnp.where(lane_blk == c, posb[c * rows:(c + 1) * rows], packed)
    ang = packed * invf

    def spread(t):
        rolled = [t] + [pltpu.roll(t, nfreq * k, 1) for k in range(1, nblk)]
        blocks = []
        for c in range(nblk):
            out = rolled[(0 - c) % nblk]
            for dst in range(1, nblk):
                out = jnp.where(lane_blk == dst, rolled[(dst - c) % nblk], out)
            blocks.append(out)
        return jnp.concatenate(blocks, axis=0)

    return spread(jnp.cos(ang)), spread(jnp.sin(ang))


def _mla_proj_kernel(x_ref, g_ref, pos_ref, invf_ref, sgn_ref, wa_ref, qn_ref, wqb_ref,
                     kvn_ref, wkvb_ref, qt_ref, kn_ref, kx_ref, vt_ref, *, q_scale):
    H = MLA_HEADS
    h = _rms(x_ref[...], g_ref[...]).astype(BF16)
    lat = jnp.dot(h, wa_ref[...], preferred_element_type=F32)
    cq = _rms(lat[:, :MLA_Q_RANK], qn_ref[...]).astype(BF16)
    ckv = _rms(lat[:, MLA_Q_RANK:MLA_Q_RANK + MLA_KV_RANK], kvn_ref[...]).astype(BF16)
    q = jnp.dot(cq, wqb_ref[...], preferred_element_type=F32) * q_scale
    kv = jnp.dot(ckv, wkvb_ref[...], preferred_element_type=F32)
    cos, sin = _rope_tables(pos_ref[...], invf_ref[...])
    sin = sin * sgn_ref[...]
    kpe0 = MLA_Q_RANK + MLA_KV_RANK
    kx = lat[:, kpe0:kpe0 + LANES] * cos + lat[:, kpe0 + LANES:kpe0 + 2 * LANES] * sin
    kx_ref[0] = kx.astype(BF16)
    half = lax.broadcasted_iota(jnp.int32, kx.shape, 1) // MLA_ROPE
    pe0 = H * MLA_NOPE
    sw0 = pe0 + H * MLA_ROPE
    for hd in range(H):
        p, m = divmod(hd, LANES // MLA_ROPE)
        qpe = (q[:, pe0 + p * LANES:pe0 + (p + 1) * LANES] * cos
               + q[:, sw0 + p * LANES:sw0 + (p + 1) * LANES] * sin)
        sel = half == m
        qt_ref[0, hd, 0:LANES, :] = q[:, hd * MLA_NOPE:(hd + 1) * MLA_NOPE].T.astype(BF16)
        qt_ref[0, hd, LANES:2 * LANES, :] = jnp.where(sel, qpe, 0.0).T.astype(BF16)
        k0 = hd * (MLA_NOPE + MLA_V)
        kn_ref[0, hd] = kv[:, k0:k0 + MLA_NOPE].astype(BF16)
        vt_ref[0, hd, 0:MLA_V, :] = kv[:, k0 + MLA_NOPE:k0 + MLA_NOPE + MLA_V].T.astype(BF16)
        vt_ref[0, hd, MLA_V:MLA_V + V_PAD, :] = jnp.ones((V_PAD, kv.shape[0]), BF16)


def _attn_kernel(qt_ref, kn_ref, kx_ref, vt_ref, o_ref, m_scr, acc_scr, *, tq):
    i = pl.program_id(2)
    heads = range(qt_ref.shape[1])
    dv = o_ref.shape[1] // qt_ref.shape[1]
    qt = [qt_ref[0, h] for h in heads]
    for h in heads:
        m_scr[h] = jnp.full(m_scr.shape[1:], -jnp.inf, F32)
        acc_scr[h] = jnp.zeros(acc_scr.shape[1:], F32)

    def keys(h, start, n):
        return jnp.concatenate([kn_ref[0, h, pl.ds(start, n), :], kx_ref[0, pl.ds(start, n), :]],
                               axis=1)

    def update(h, m_new, pv):
        m_old = m_scr[h, 0:1, :]
        m_scr[h, 0:1, :] = m_new
        acc_scr[h] = jnp.exp2(m_old - m_new) * acc_scr[h] + pv

    def step(j):
        start = pl.multiple_of(j * tq, tq)
        s = [jnp.dot(keys(h, start, tq), qt[h], preferred_element_type=F32)
             for h in heads]
        m_new = [jnp.maximum(m_scr[h, 0:1, :], jnp.max(s[h], axis=0, keepdims=True)) for h in heads]
        p = [jnp.exp2(s[h] - m_new[h]).astype(BF16) for h in heads]
        for h in heads:
            update(h, m_new[h], jnp.dot(vt_ref[0, h, :, pl.ds(start, tq)], p[h],
                                        preferred_element_type=F32))

    def diagonal_step(j):
        half = tq // 2
        start = pl.multiple_of(j * tq, tq)
        mid = pl.multiple_of(j * tq + half, half)
        key = lax.broadcasted_iota(jnp.int32, (half, tq), 0)
        qry = lax.broadcasted_iota(jnp.int32, (half, tq), 1)
        keep_top = key <= qry
        keep_bot = keep_top[:, :half]
        s_top = [jnp.where(keep_top, jnp.dot(keys(h, start, half), qt[h],
                                             preferred_element_type=F32), -jnp.inf) for h in heads]
        s_bot = [jnp.where(keep_bot, jnp.dot(keys(h, mid, half), qt[h][:, half:],
                                             preferred_element_type=F32), -jnp.inf) for h in heads]
        m_top = [jnp.max(x, axis=0, keepdims=True) for x in s_top]
        m_bot = [jnp.max(x, axis=0, keepdims=True) for x in s_bot]
        m_new = [jnp.maximum(m_scr[h, 0:1, :], jnp.concatenate(
            [m_top[h][:, :half], jnp.maximum(m_top[h][:, half:], m_bot[h])], axis=1)) for h in heads]
        p_top = [jnp.exp2(s_top[h] - m_new[h]).astype(BF16) for h in heads]
        p_bot = [jnp.exp2(s_bot[h] - m_new[h][:, half:]).astype(BF16) for h in heads]
        for h in heads:
            pv_top = jnp.dot(vt_ref[0, h, :, pl.ds(start, half)], p_top[h], preferred_element_type=F32)
            pv_bot = jnp.dot(vt_ref[0, h, :, pl.ds(mid, half)], p_bot[h], preferred_element_type=F32)
            update(h, m_new[h], pv_top + jnp.concatenate([jnp.zeros_like(pv_bot), pv_bot], axis=1))

    def pair(jj, c):
        step(2 * jj)
        step(2 * jj + 1)
        return c

    lax.fori_loop(0, i // 2, pair, 0)

    @pl.when(i % 2 == 1)
    def _():
        step(i - 1)

    diagonal_step(i)
    for h in heads:
        acc = acc_scr[h]
        o_ref[:, h * dv:(h + 1) * dv] = (acc[:dv] / acc[dv:dv + 1]).T.astype(o_ref.dtype)


def _mla(x, positions, mix_g, w_a, q_norm, w_qb, kv_norm, w_kvb, w_o, B, S):
    T, D = x.shape
    H = MLA_HEADS
    kpe = w_a[:, MLA_Q_RANK + MLA_KV_RANK:]
    kpe_sw = jnp.concatenate([kpe[:, MLA_ROPE // 2:], kpe[:, :MLA_ROPE // 2]], axis=1)
    reps = LANES // MLA_ROPE
    wa2 = jnp.concatenate([w_a[:, :MLA_Q_RANK + MLA_KV_RANK], jnp.tile(kpe, (1, reps)),
                           jnp.tile(kpe_sw, (1, reps))], axis=1).astype(BF16)
    wq3 = w_qb.reshape(MLA_Q_RANK, H, MLA_NOPE + MLA_ROPE)
    q_nope = wq3[:, :, :MLA_NOPE].reshape(MLA_Q_RANK, H * MLA_NOPE)
    q_pe = wq3[:, :, MLA_NOPE:]
    q_pe_sw = jnp.concatenate([q_pe[:, :, MLA_ROPE // 2:], q_pe[:, :, :MLA_ROPE // 2]], axis=2)
    wqb2 = jnp.concatenate([q_nope, q_pe.reshape(MLA_Q_RANK, H * MLA_ROPE),
                            q_pe_sw.reshape(MLA_Q_RANK, H * MLA_ROPE)], axis=1).astype(BF16)
    inv_freq = 1.0 / (ROPE_THETA ** (jnp.arange(0, MLA_ROPE, 2, dtype=F32) / MLA_ROPE))
    invf = jnp.tile(inv_freq, LANES // (MLA_ROPE // 2)).reshape(1, LANES)
    first_half = (jnp.arange(LANES) % MLA_ROPE) < MLA_ROPE // 2
    sgn = jnp.where(first_half, -1.0, 1.0).astype(F32).reshape(1, LANES)
    q_scale = (MLA_NOPE + MLA_ROPE) ** -0.5 * math.log2(math.e)

    tm = min(TM_PROJ, S)
    nt = S // tm
    wa_cols = wa2.shape[1]
    wq_cols = wqb2.shape[1]
    kv_cols = w_kvb.shape[1]
    qk_dim = 2 * LANES
    q, kn, kx, v = pl.pallas_call(
        functools.partial(_mla_proj_kernel, q_scale=q_scale),
        grid=(B, nt),
        in_specs=[pl.BlockSpec((tm, D), lambda b, i: (b * nt + i, 0)),
                  _resident((1, D)),
                  pl.BlockSpec((tm, 1), lambda b, i: (b * nt + i, 0)),
                  _resident((1, LANES)), _resident((1, LANES)),
                  _resident((D, wa_cols)), _resident((1, MLA_Q_RANK)),
                  _resident((MLA_Q_RANK, wq_cols)), _resident((1, MLA_KV_RANK)),
                  _resident((MLA_KV_RANK, kv_cols))],
        out_specs=[pl.BlockSpec((1, H, qk_dim, tm), lambda b, i: (b, 0, 0, i)),
                   pl.BlockSpec((1, H, tm, MLA_NOPE), lambda b, i: (b, 0, i, 0)),
                   pl.BlockSpec((1, tm, LANES), lambda b, i: (b, i, 0)),
                   pl.BlockSpec((1, H, MLA_V + V_PAD, tm), lambda b, i: (b, 0, 0, i))],
        out_shape=[jax.ShapeDtypeStruct((B, H, qk_dim, S), BF16),
                   jax.ShapeDtypeStruct((B, H, S, MLA_NOPE), BF16),
                   jax.ShapeDtypeStruct((B, S, LANES), BF16),
                   jax.ShapeDtypeStruct((B, H, MLA_V + V_PAD, S), BF16)],
        compiler_params=_params(2),
        name="mla_proj",
    )(x, mix_g.reshape(1, D), positions.reshape(T, 1), invf, sgn, wa2, q_norm.reshape(1, -1),
      wqb2, kv_norm.reshape(1, -1), w_kvb.astype(BF16))

    tq = min(TQ_ATTN, S)
    nq = S // tq
    hp = ATTN_HEADS_PER_STEP
    o = pl.pallas_call(
        functools.partial(_attn_kernel, tq=tq),
        grid=(B, H // hp, nq),
        in_specs=[pl.BlockSpec((1, hp, qk_dim, tq), lambda b, h, i: (b, h, 0, i)),
                  pl.BlockSpec((1, hp, S, MLA_NOPE), lambda b, h, i: (b, h, 0, 0),
                               pipeline_mode=pl.Buffered(1)),
                  pl.BlockSpec((1, S, LANES), lambda b, h, i: (b, 0, 0),
                               pipeline_mode=pl.Buffered(1)),
                  pl.BlockSpec((1, hp, MLA_V + V_PAD, S), lambda b, h, i: (b, h, 0, 0),
                               pipeline_mode=pl.Buffered(1))],
        out_specs=pl.BlockSpec((tq, hp * MLA_V), lambda b, h, i: (b * nq + i, h)),
        out_shape=jax.ShapeDtypeStruct((T, H * MLA_V), BF16),
        scratch_shapes=[pltpu.VMEM((hp, 8, tq), F32), pltpu.VMEM((hp, MLA_V + V_PAD, tq), F32)],
        compiler_params=_params(3),
        name="mla_attn",
    )(q, kn, kx, v)
    return o, w_o.astype(BF16)


def _ml_proj_kernel(x_ref, g_ref, w_ref, wg_ref, bif_ref, cw_ref, cb_ref,
                    qt_ref, k_ref, vt_ref, o_ref, gates_ref, pbuf, *, tiles_per_seq):
    tm = x_ref.shape[0]
    QK = ML_HEADS * ML_QK
    V = ML_HEADS * ML_V

    @pl.when(pl.program_id(0) % tiles_per_seq == 0)
    def _():
        pbuf[...] = jnp.zeros(pbuf.shape, F32)

    h = _rms(x_ref[...], g_ref[...]).astype(BF16)
    w = w_ref[...]
    p_vo = jnp.dot(h, w[:, 2 * QK:], preferred_element_type=F32)
    gates_ref[...] = jnp.dot(h, wg_ref[...], preferred_element_type=F32) + bif_ref[...]
    vt_ref[...] = p_vo[:, :V].T.astype(BF16)
    o_ref[...] = p_vo[:, V:]
    pqk = jnp.dot(h, w[:, :2 * QK], preferred_element_type=F32)
    tail = pbuf[...]
    acc = cb_ref[...] + cw_ref[ML_CONV - 1:ML_CONV, :] * pqk
    for kk in range(ML_CONV - 1):
        acc = acc + cw_ref[kk:kk + 1, :] * _shift_rows(pqk, tail, ML_CONV - 1 - kk)
    pbuf[...] = pqk[tm - 8:tm, :]
    qk = acc * _sigmoid(acc)
    qt_ref[...] = qk[:, :QK].T.astype(BF16)
    k_ref[...] = (qk[:, QK:] * ML_QK ** -0.5).astype(BF16)


def _ml_chunk_kernel(qt_ref, k_ref, vt_ref, o_ref, g_ref, on_ref, out_ref, c_scr, m_scr):
    L = k_ref.shape[0]
    H = ML_HEADS
    pair = LANES // ML_QK

    @pl.when(pl.program_id(1) == 0)
    def _():
        c_scr[...] = jnp.zeros(c_scr.shape, F32)
        m_scr[...] = jnp.zeros(m_scr.shape, F32)

    G = g_ref[...]
    GT = G.T
    lfG = -_softplus(-G)
    lfGT = -_softplus(-GT)
    row = lax.broadcasted_iota(jnp.int32, (L, L), 0)
    col = lax.broadcasted_iota(jnp.int32, (L, L), 1)
    causal = row <= col
    tril_b = (col <= row).astype(BF16)
    triu_b = causal.astype(BF16)
    fc_cols = sum(jnp.dot(tril_b, piece, preferred_element_type=F32) for piece in _split3(lfG))
    fc_rows = sum(jnp.dot(piece, triu_b, preferred_element_type=F32) for piece in _split3(lfGT))
    m_row = m_scr[...]
    ones_rows = (lax.broadcasted_iota(jnp.int32, (ML_V, L), 0) == 0).astype(BF16)
    row_half = lax.broadcasted_iota(jnp.int32, (LANES, L), 0) // ML_QK
    lane_half = lax.broadcasted_iota(jnp.int32, (L, LANES), 1) // ML_QK
    slane_half = lax.broadcasted_iota(jnp.int32, (1, LANES), 1) // ML_QK
    lane1 = lax.broadcasted_iota(jnp.int32, (1, LANES), 1)
    heads = range(H)
    pairs = range(H // pair)
    pr = [hd // pair for hd in heads]
    mm = [hd % pair for hd in heads]
    fc_row = [fc_rows[H + hd:H + hd + 1, :] for hd in heads]
    i_row = [GT[hd:hd + 1, :] for hd in heads]
    c_col = [G[:, hd:hd + 1] - fc_cols[:, H + hd:H + hd + 1] for hd in heads]
    m_prev = [m_row[:, hd:hd + 1] for hd in heads]
    dmat = [jnp.where(causal, fc_row[hd] + c_col[hd], -jnp.inf) for hd in heads]
    inter = [fc_row[hd] + m_prev[hd] for hd in heads]
    m_t = [jnp.maximum(inter[hd], jnp.max(dmat[hd], axis=0, keepdims=True)) for hd in heads]
    dw = [jnp.exp(dmat[hd] - m_t[hd]) for hd in heads]
    inter_w = [jnp.exp(inter[hd] - m_t[hd]) for hd in heads]
    qtp = [qt_ref[p * LANES:(p + 1) * LANES, :] for p in pairs]
    kp = [k_ref[:, p * LANES:(p + 1) * LANES] for p in pairs]
    qtm = [jnp.where(row_half == mm[hd], qtp[pr[hd]], jnp.zeros_like(qtp[0])) for hd in heads]
    km = [jnp.where(lane_half == mm[hd], kp[pr[hd]], jnp.zeros_like(kp[0])) for hd in heads]
    s = [_dot(kp[pr[hd]], qtm[hd]) * dw[hd] for hd in heads]
    vht = [jnp.concatenate([vt_ref[hd * ML_V:(hd + 1) * ML_V, :], ones_rows], axis=0)
           for hd in heads]
    ct = [c_scr[p] for p in pairs]
    nd = [_dot(vht[hd], s[hd]) + inter_w[hd] * _dot(ct[pr[hd]], qtm[hd]) for hd in heads]
    hid = [nd[hd][:ML_V] / jnp.maximum(jnp.abs(nd[hd][ML_V:ML_V + 1]), jnp.exp(-m_t[hd]))
           for hd in heads]
    hid = [x * lax.rsqrt(jnp.mean(x * x, axis=0, keepdims=True) + RMS_EPS) for x in hid]
    for hd in heads:
        cols = slice(hd * ML_V, (hd + 1) * ML_V)
        out_ref[:, cols] = (hid[hd].T * on_ref[:, cols] * _sigmoid(o_ref[:, cols])).astype(out_ref.dtype)
    m_new = [m_t[hd][:, L - 1:L] for hd in heads]
    fc_last = [fc_row[hd][:, L - 1:L] for hd in heads]
    decay = [jnp.exp(fc_last[hd] + m_prev[hd] - m_new[hd]) for hd in heads]
    w_row = [jnp.exp(fc_last[hd] - fc_row[hd] + i_row[hd] - m_new[hd]) for hd in heads]
    upd = [_dot(w_row[hd] * vht[hd].astype(F32), km[hd]) for hd in heads]
    for p in pairs:
        scale = decay[p * pair]
        for k in range(1, pair):
            scale = jnp.where(slane_half == k, decay[p * pair + k], scale)
        c_scr[p] = scale * ct[p] + sum(upd[p * pair + k] for k in range(pair))
    m_out = m_row
    for hd in heads:
        m_out = jnp.where(lane1 == hd, m_new[hd], m_out)
    m_scr[...] = m_out


def _mlstm(x, mix_g, w_in, b_if, conv_w, conv_b, out_norm, w_o, B, S):
    T, D = x.shape
    H = ML_HEADS
    QK = H * ML_QK
    V = H * ML_V
    main = 2 * QK + 2 * V
    w_main = w_in[:, :main].astype(BF16)
    w_gate = jnp.pad(w_in[:, main:], ((0, 0), (0, LANES - 2 * H))).astype(BF16)
    bif = jnp.pad(b_if, (0, LANES - 2 * H)).reshape(1, LANES)
    tm = min(TM_PROJ, S)
    nt = S // tm
    row_spec = lambda n: pl.BlockSpec((tm, n), lambda i: (i, 0))
    col_spec = lambda n: pl.BlockSpec((n, tm), lambda i: (0, i))
    qt, k, vt, o, gates = pl.pallas_call(
        functools.partial(_ml_proj_kernel, tiles_per_seq=nt),
        grid=(T // tm,),
        in_specs=[row_spec(D), _resident((1, D)), _resident((D, main)), _resident((D, LANES)),
                  _resident((1, LANES)), _resident((ML_CONV, 2 * QK)), _resident((1, 2 * QK))],
        out_specs=[col_spec(QK), row_spec(QK), col_spec(V), row_spec(V), row_spec(LANES)],
        out_shape=[jax.ShapeDtypeStruct((QK, T), BF16), jax.ShapeDtypeStruct((T, QK), BF16),
                   jax.ShapeDtypeStruct((V, T), BF16), jax.ShapeDtypeStruct((T, V), F32),
                   jax.ShapeDtypeStruct((T, LANES), F32)],
        scratch_shapes=[pltpu.VMEM((8, 2 * QK), F32)],
        compiler_params=_params(1),
        name="mlstm_proj",
    )(x, mix_g.reshape(1, D), w_main, w_gate, bif, conv_w, conv_b.reshape(1, -1))

    L = min(ML_CHUNK, S)
    nc = S // L
    chunk_spec = lambda n: pl.BlockSpec((L, n), lambda b, c: (b * nc + c, 0))
    chunk_t_spec = lambda n: pl.BlockSpec((n, L), lambda b, c: (0, b * nc + c))
    hid = pl.pallas_call(
        _ml_chunk_kernel,
        grid=(B, nc),
        in_specs=[chunk_t_spec(QK), chunk_spec(QK), chunk_t_spec(V), chunk_spec(V), chunk_spec(LANES),
                  _resident((1, V))],
        out_specs=chunk_spec(V),
        out_shape=jax.ShapeDtypeStruct((T, V), BF16),
        scratch_shapes=[pltpu.VMEM((H * ML_QK // LANES, 2 * ML_V, LANES), F32),
                        pltpu.VMEM((1, LANES), F32)],
        compiler_params=_params(2),
        name="mlstm_chunk",
    )(qt, k, vt, o, gates, out_norm.reshape(1, V))
    return hid, w_o.astype(BF16)


def _rw_proj_kernel(x_ref, g_ref, mu_ref, wr_ref, wk_ref, wv_ref, w1_ref, w2_ref, a1_ref, a2_ref,
                    g1_ref, g2_ref, w0_ref, a0_ref, kk_ref, ka_ref,
                    r_out, k_out, v_out, lw_out, a_out, kkraw_out, g_out, hbuf, *, tiles_per_seq):
    tm = x_ref.shape[0]
    D = x_ref.shape[1]

    @pl.when(pl.program_id(0) % tiles_per_seq == 0)
    def _():
        hbuf[...] = jnp.zeros(hbuf.shape, F32)

    h = _rms(x_ref[...], g_ref[...])
    xx = _shift_rows(h, hbuf[...], 1) - h
    hbuf[...] = h[tm - 8:tm, :]
    mix = lambda j: (h + xx * mu_ref[j:j + 1, :]).astype(BF16)
    r = jnp.dot(mix(0), wr_ref[...], preferred_element_type=F32)
    k = jnp.dot(mix(2), wk_ref[...], preferred_element_type=F32)
    v = jnp.dot(mix(3), wv_ref[...], preferred_element_type=F32)
    zw = w0_ref[...] + _dot(jnp.tanh(jnp.dot(mix(1), w1_ref[...], preferred_element_type=F32)),
                            w2_ref[...])
    log_decay = -math.exp(-0.5) * _sigmoid(zw)
    a = _sigmoid(a0_ref[...] + _dot(jnp.dot(mix(4), a1_ref[...], preferred_element_type=F32),
                                    a2_ref[...]))
    g = _dot(_sigmoid(jnp.dot(mix(5), g1_ref[...], preferred_element_type=F32)), g2_ref[...])
    r_out[...] = r.astype(r_out.dtype)
    v_out[...] = v.astype(v_out.dtype)
    lw_out[...] = log_decay
    a_out[...] = a.astype(a_out.dtype)
    kkraw_out[...] = (k * kk_ref[...]).astype(kkraw_out.dtype)
    k_out[...] = (k * (1.0 + (a - 1.0) * ka_ref[...])).astype(k_out.dtype)
    g_out[...] = g.astype(g_out.dtype)


def _group_sums(xs, ones_bd):
    n = xs[0].shape[0]
    pieces = []
    for x in xs:
        hi = x.astype(BF16)
        pieces += [hi, (x - hi.astype(F32)).astype(BF16)]
    both = jnp.dot(jnp.concatenate(pieces, axis=0), ones_bd, preferred_element_type=F32)
    return [both[2 * i * n:(2 * i + 1) * n] + both[(2 * i + 1) * n:(2 * i + 2) * n]
            for i in range(len(xs))]


def _rw_chunk_kernel(r_ref, k_ref, v_ref, lw_ref, a_ref, kk_ref, g_ref, rk_ref, lnw_ref, lnb_ref,
                     out_ref, s_scr):
    nb, L = r_ref.shape[0], RW_HEAD
    n_sub = r_ref.shape[1] // L
    W = s_scr.shape[1]
    N = RW_HEAD
    nh = W // N
    ng = r_ref.shape[2] // W

    @pl.when(pl.program_id(0) == 0)
    def _():
        s_scr[...] = jnp.zeros(s_scr.shape, F32)

    er = lax.broadcasted_iota(jnp.int32, (nh * L, W), 0)
    ec = lax.broadcasted_iota(jnp.int32, (nh * L, W), 1)
    same_head = (er // L) == (ec // N)
    sr = lax.broadcasted_iota(jnp.int32, (W, W), 0)
    sc = lax.broadcasted_iota(jnp.int32, (W, W), 1)
    state_bd = (sr // N) == (sc // N)
    ones_bd = state_bd.astype(BF16)
    lt = lax.broadcasted_iota(jnp.int32, (L, W), 0)
    ls = lax.broadcasted_iota(jnp.int32, (L, W), 1) % L
    strict = ls < lt
    incl = ls <= lt
    eye = (ls == lt).astype(F32)
    tr = lax.broadcasted_iota(jnp.int32, (L, L), 0)
    tc = lax.broadcasted_iota(jnp.int32, (L, L), 1)
    tril_b = (tc <= tr).astype(BF16)
    inv_n = 1.0 / N
    zero_b = jnp.zeros((nh * L, W), BF16)

    def expand(x):
        return jnp.where(same_head, jnp.concatenate([x.astype(BF16)] * nh, axis=0), zero_b)

    groups = range(nb * ng)
    bat = [g // ng for g in groups]
    cols = [slice((g % ng) * W, (g % ng + 1) * W) for g in groups]

    def chunk(sub, s0):
        rows = slice(sub * L, (sub + 1) * L)
        load = lambda ref: [ref[bat[g], rows, cols[g]].astype(F32) for g in groups]
        r, k, v, lw, kk, a_sig = (load(ref) for ref in (r_ref, k_ref, v_ref, lw_ref, kk_ref, a_ref))
        sums = _group_sums([jnp.concatenate([kk[g] * kk[g], r[g] * k[g] * rk_ref[:, cols[g]]], axis=0)
                            for g in groups], ones_bd)
        kk = [kk[g] / jnp.maximum(jnp.sqrt(sums[g][:L]), 1e-12) for g in groups]
        bonus = [sums[g][L:] * v[g] for g in groups]
        bv = [kk[g] * a_sig[g] for g in groups]
        cum = [sum(jnp.dot(tril_b, piece, preferred_element_type=F32) for piece in _split3(x))
               for x in lw]
        cum_last = [x[L - 1:L, :] for x in cum]
        ar = [jnp.concatenate([-jnp.exp(cum[g] - lw[g]) * kk[g], jnp.exp(cum[g]) * r[g]],
                              axis=0).astype(BF16) for g in groups]
        p_inv = [jnp.exp(-x) for x in cum]
        b_e = [expand(p_inv[g] * bv[g]) for g in groups]
        k_e = [expand(p_inv[g] * k[g]) for g in groups]
        v_e = [expand(x) for x in v]
        gb = [_dot_nt(ar[g], b_e[g]) for g in groups]
        gk = [_dot_nt(ar[g], k_e[g]) for g in groups]
        m_ab = [jnp.where(strict, x[:L], 0.0) for x in gb]
        m_rb = [jnp.where(incl, x[L:], 0.0) for x in gb]
        m_akrk = [jnp.concatenate([jnp.where(strict, x[:L], 0.0), jnp.where(incl, x[L:], 0.0)],
                                  axis=0) for x in gk]

        pw = [_dot(x, expand(x)) for x in m_ab]
        tinv = [eye + x for x in m_ab]
        n_sq = int(math.log2(L)) - 1
        for it in range(n_sq):
            pw_e = [expand(x) for x in pw]
            if it + 1 < n_sq:
                both = [_dot(jnp.concatenate([pw[g], tinv[g]], axis=0), pw_e[g]) for g in groups]
                pw = [x[:L] for x in both]
                tinv = [tinv[g] + both[g][L:] for g in groups]
            else:
                tinv = [tinv[g] + _dot(tinv[g], pw_e[g]) for g in groups]

        ars = [_dot_nt(ar[g], s0[g]) for g in groups]
        mv = [_dot(m_akrk[g], v_e[g]) for g in groups]
        u = [_dot(tinv[g], expand(ars[g][:L] + mv[g][:L])) for g in groups]
        y = [ars[g][L:] + mv[g][L:] + _dot(m_rb[g], expand(u[g])) for g in groups]
        s_new = []
        for g in groups:
            p_out = jnp.exp(cum_last[g] - cum[g])
            upd = _dot_tn(jnp.concatenate([u[g], v[g]], axis=0),
                          jnp.concatenate([p_out * bv[g], p_out * k[g]], axis=0))
            s_new.append(s0[g] * jnp.exp(cum_last[g]) + jnp.where(state_bd, upd, 0.0))

        mean = [x * inv_n for x in _group_sums(y, ones_bd)]
        yc = [y[g] - mean[g] for g in groups]
        var = [x * inv_n for x in _group_sums([x * x for x in yc], ones_bd)]
        for g in groups:
            c = cols[g]
            yn = yc[g] * lax.rsqrt(var[g] + GN_EPS) * lnw_ref[:, c] + lnb_ref[:, c]
            out_ref[bat[g], rows, c] = ((yn + bonus[g]) * g_ref[bat[g], rows, c].astype(F32)).astype(out_ref.dtype)
        return s_new

    state = [s_scr[g] for g in groups]
    for sub in range(n_sub):
        state = chunk(sub, state)
    for g in groups:
        s_scr[g] = state[g]


def _rwkv(x, mix_g, mu, w_r, w_k, w_v, w0, w1, w2, a0, a1, a2, g1, g2, k_k, k_a, r_k,
          ln_w, ln_b, w_o, B, S):
    T, D = x.shape
    pad_cols = lambda w: jnp.pad(w, ((0, 0), (0, LANES - w.shape[1]))).astype(BF16)
    pad_rows = lambda w: jnp.pad(w, ((0, LANES - w.shape[0]), (0, 0))).astype(BF16)
    vec = lambda t: t.reshape(1, D)
    mu8 = jnp.pad(mu, ((0, 8 - mu.shape[0]), (0, 0)))
    tm = min(TM_PROJ, S)
    nt = S // tm
    row_spec = pl.BlockSpec((tm, D), lambda i: (i, 0))
    outs = pl.pallas_call(
        functools.partial(_rw_proj_kernel, tiles_per_seq=nt),
        grid=(T // tm,),
        in_specs=[row_spec, _resident((1, D)), _resident((8, D)),
                  _resident((D, D)), _resident((D, D)), _resident((D, D)),
                  _resident((D, LANES)), _resident((LANES, D)),
                  _resident((D, LANES)), _resident((LANES, D)),
                  _resident((D, LANES)), _resident((LANES, D)),
                  _resident((1, D)), _resident((1, D)), _resident((1, D)), _resident((1, D))],
        out_specs=[row_spec] * 7,
        out_shape=[jax.ShapeDtypeStruct((T, D), F32 if i == 3 else BF16) for i in range(7)],
        scratch_shapes=[pltpu.VMEM((8, D), F32)],
        compiler_params=_params(1),
        name="rwkv_proj",
    )(x, vec(mix_g), mu8, w_r.astype(BF16), w_k.astype(BF16), w_v.astype(BF16),
      pad_cols(w1), pad_rows(w2), pad_cols(a1), pad_rows(a2), pad_cols(g1), pad_rows(g2),
      vec(w0), vec(a0), vec(k_k), vec(k_a))
    r, k, v, lw, a, kkraw, g = outs

    L = min(RW_CHUNK * RW_CHUNKS_PER_STEP, S)
    nc = S // L
    W = RW_GROUP * RW_HEAD
    assert RW_CHUNK == RW_HEAD
    chunk_spec = pl.BlockSpec((B, L, D), lambda c: (0, c, 0))
    seq = lambda t: t.reshape(B, S, D)
    yg = pl.pallas_call(
        _rw_chunk_kernel,
        grid=(nc,),
        in_specs=[chunk_spec] * 7 + [_resident((1, D))] * 3,
        out_specs=chunk_spec,
        out_shape=jax.ShapeDtypeStruct((B, S, D), BF16),
        scratch_shapes=[pltpu.VMEM((B * D // W, W, W), F32)],
        compiler_params=_params(1),
        name="rwkv_chunk",
    )(seq(r), seq(k), seq(v), seq(lw), seq(a), seq(kkraw), seq(g), vec(r_k), vec(ln_w), vec(ln_b))
    return yg.reshape(T, D), w_o.astype(BF16)


def kernel(x, positions, ffn_norm, ffn_w_gate, ffn_w_up, ffn_w_down, mix_norm, final_norm, mla_w_a, mla_q_norm, mla_w_qb, mla_kv_norm, mla_w_kvb, mla_w_o, ml_w_in, ml_b_if, ml_conv_w, ml_conv_b, ml_out_norm, ml_w_o, rw_mu, rw_w_r, rw_w_k, rw_w_v, rw_w0, rw_w1, rw_w2, rw_a0, rw_a1, rw_a2, rw_g1, rw_g2, rw_k_k, rw_k_a, rw_r_k, rw_ln_w, rw_ln_b, rw_w_o):
    B, S, D = x.shape
    depth = mix_norm.shape[0]
    n_mixers = 3
    h = x.reshape(B * S, D)

    wg, wu, wd = (w.astype(BF16) for w in (ffn_w_gate, ffn_w_up, ffn_w_down))

    def ffn(h, layer, half, **kw):
        return _ffn(h, ffn_norm[layer, half], wg, wu, wd, (layer, half), **kw)

    for layer in range(depth):
        h = ffn(h, layer, 0)
        kind, j = layer % n_mixers, layer // n_mixers
        if kind == 0:
            mixer = _mla(h, positions, mix_norm[layer], mla_w_a[j], mla_q_norm[j], mla_w_qb[j],
                         mla_kv_norm[j], mla_w_kvb[j], mla_w_o[j], B, S)
        elif kind == 1:
            mixer = _mlstm(h, mix_norm[layer], ml_w_in[j], ml_b_if[j], ml_conv_w[j], ml_conv_b[j],
                           ml_out_norm[j], ml_w_o[j], B, S)
        else:
            mixer = _rwkv(h, mix_norm[layer], rw_mu[j], rw_w_r[j], rw_w_k[j], rw_w_v[j], rw_w0[j],
                          rw_w1[j], rw_w2[j], rw_a0[j], rw_a1[j], rw_a2[j], rw_g1[j], rw_g2[j],
                          rw_k_k[j], rw_k_a[j], rw_r_k[j], rw_ln_w[j], rw_ln_b[j], rw_w_o[j], B, S)
        h = ffn(h, layer, 1, mixer=mixer, final_g=final_norm if layer == depth - 1 else None)
    return h.reshape(B, S, D)
```

```python
import functools
import math

import jax
import jax.numpy as jnp
from jax import lax
from jax.experimental import pallas as pl
from jax.experimental.pallas import tpu as pltpu

F32 = jnp.float32
BF16 = jnp.bfloat16

RMS_EPS = 1e-6
GN_EPS = 64e-5
LANES = 128
MXU_DIM = 256
V7X_VMEM_BYTES = 64 * 1024 * 1024
VMEM_LIMIT = V7X_VMEM_BYTES * 7 // 8

MLA_HEADS = 8
MLA_Q_RANK = 512
MLA_KV_RANK = 256
MLA_NOPE = 128
MLA_ROPE = 64
MLA_V = 128
ROPE_THETA = 10000.0
ML_HEADS = 8
ML_QK = 64
ML_V = 128
ML_CONV = 4
RW_HEAD = 64

TM_FFN = 512
FFN_SUBTILES = 2
TM_PROJ = 512
TQ_ATTN = 512
V_PAD = 16
ATTN_HEADS_PER_STEP = 4
ML_CHUNK = 256
RW_CHUNK = 64
RW_CHUNKS_PER_STEP = 4
RW_GROUP = MXU_DIM // RW_HEAD


def _params(n_axes):
    return pltpu.CompilerParams(dimension_semantics=("arbitrary",) * n_axes,
                                vmem_limit_bytes=VMEM_LIMIT)


def _resident(shape):
    nd = len(shape)
    return pl.BlockSpec(shape, lambda *_: (0,) * nd, pipeline_mode=pl.Buffered(1))


def _rms(x, g, eps=RMS_EPS):
    return x * lax.rsqrt(jnp.mean(x * x, axis=-1, keepdims=True) + eps) * g


def _dot(a, b):
    return jnp.dot(a.astype(BF16), b.astype(BF16), preferred_element_type=F32)


def _dot_nt(a, b):
    return lax.dot_general(a.astype(BF16), b.astype(BF16), (((1,), (1,)), ((), ())),
                           preferred_element_type=F32)


def _dot_tn(a, b):
    return lax.dot_general(a.astype(BF16), b.astype(BF16), (((0,), (0,)), ((), ())),
                           preferred_element_type=F32)


def _split3(x):
    hi = x.astype(BF16)
    r1 = x - hi.astype(F32)
    mid = r1.astype(BF16)
    lo = (r1 - mid.astype(F32)).astype(BF16)
    return hi, mid, lo


def _shift_rows(x, tail, shift):
    rolled = pltpu.roll(x, shift, 0)
    row = lax.broadcasted_iota(jnp.int32, tail.shape, 0)
    head = jnp.where(row < shift, pltpu.roll(tail, shift, 0), rolled[0:8])
    return jnp.concatenate([head, rolled[8:]], axis=0)


def _softplus(y):
    return jnp.maximum(y, 0.0) + jnp.log1p(jnp.exp(-jnp.abs(y)))


def _sigmoid(y):
    return 1.0 / (1.0 + jnp.exp(-y))


def _ffn_kernel(*refs, mixer, final):
    x_ref, g_ref, wg_ref, wu_ref, wd_ref = refs[:5]
    o_ref = refs[-1]
    tm = x_ref.shape[0]
    subs = [slice(k * (tm // FFN_SUBTILES), (k + 1) * (tm // FFN_SUBTILES)) for k in range(FFN_SUBTILES)]
    x = [x_ref[r, :] for r in subs]
    if mixer:
        a_ref, wo_ref = refs[5:7]
        x = [x[k] + jnp.dot(a_ref[r, :], wo_ref[...], preferred_element_type=F32)
             for k, r in enumerate(subs)]
    h = [_rms(t, g_ref[...]).astype(BF16) for t in x]
    gate = [jnp.dot(t, wg_ref[...], preferred_element_type=F32) for t in h]
    up = [jnp.dot(t, wu_ref[...], preferred_element_type=F32) for t in h]
    act = [(g * _sigmoid(g) * u).astype(BF16) for g, u in zip(gate, up)]
    y = [jnp.dot(t, wd_ref[...], preferred_element_type=F32) for t in act]
    for k, r in enumerate(subs):
        out = x[k] + 0.5 * y[k]
        if final:
            out = _rms(out, refs[-2][...])
        o_ref[r, :] = out


def _ffn(x, g, wg, wu, wd, index, mixer=None, final_g=None):
    T, D = x.shape
    F = wg.shape[-1]
    tm = min(TM_FFN, T)
    row_spec = lambda n: pl.BlockSpec((tm, n), lambda i: (i, 0))
    picked = lambda r, c: pl.BlockSpec((None, None, r, c), lambda i: (*index, 0, 0),
                                       pipeline_mode=pl.Buffered(1))
    args = [x, g.reshape(1, D), wg, wu, wd]
    in_specs = [row_spec(D), _resident((1, D)), picked(D, F), picked(D, F), picked(F, D)]
    if mixer is not None:
        a, w_o = mixer
        args += [a, w_o]
        in_specs += [row_spec(a.shape[1]), _resident(w_o.shape)]
    if final_g is not None:
        args.append(final_g.reshape(1, D))
        in_specs.append(_resident((1, D)))
    return pl.pallas_call(
        functools.partial(_ffn_kernel, mixer=mixer is not None, final=final_g is not None),
        grid=(T // tm,),
        in_specs=in_specs,
        out_specs=row_spec(D),
        out_shape=jax.ShapeDtypeStruct((T, D), F32),
        compiler_params=_params(1),
        name="ffn",
    )(*args)


def _rope_tables(pos_col, invf):
    tm = pos_col.shape[0]
    nfreq = MLA_ROPE // 2
    nblk = LANES // nfreq
    rows = tm // nblk
    posb = jnp.broadcast_to(pos_col.astype(F32), (tm, LANES))
    lane_blk = lax.broadcasted_iota(jnp.int32, (rows, LANES), 1) // nfreq
    packed = posb[0:rows]
    for c in range(1, nblk):
        packed = jnp.where(lane_blk == c, posb[c * rows:(c + 1) * rows], packed)
    ang = packed * invf

    def spread(t):
        rolled = [t] + [pltpu.roll(t, nfreq * k, 1) for k in range(1, nblk)]
        blocks = []
        for c in range(nblk):
            out = rolled[(0 - c) % nblk]
            for dst in range(1, nblk):
                out = jnp.where(lane_blk == dst, rolled[(dst - c) % nblk], out)
            blocks.append(out)
        return jnp.concatenate(blocks, axis=0)

    return spread(jnp.cos(ang)), spread(jnp.sin(ang))


def _mla_proj_kernel(x_ref, g_ref, pos_ref, invf_ref, sgn_ref, wa_ref, qn_ref, wqb_ref,
                     kvn_ref, wkvb_ref, qt_ref, kn_ref, kx_ref, vt_ref, *, q_scale):
    H = MLA_HEADS
    h = _rms(x_ref[...], g_ref[...]).astype(BF16)
    lat = jnp.dot(h, wa_ref[...], preferred_element_type=F32)
    cq = _rms(lat[:, :MLA_Q_RANK], qn_ref[...]).astype(BF16)
    ckv = _rms(lat[:, MLA_Q_RANK:MLA_Q_RANK + MLA_KV_RANK], kvn_ref[...]).astype(BF16)
    q = jnp.dot(cq, wqb_ref[...], preferred_element_type=F32) * q_scale
    kv = jnp.dot(ckv, wkvb_ref[...], preferred_element_type=F32)
    cos, sin = _rope_tables(pos_ref[...], invf_ref[...])
    sin = sin * sgn_ref[...]
    kpe0 = MLA_Q_RANK + MLA_KV_RANK
    kx = lat[:, kpe0:kpe0 + LANES] * cos + lat[:, kpe0 + LANES:kpe0 + 2 * LANES] * sin
    kx_ref[0] = kx.astype(BF16)
    half = lax.broadcasted_iota(jnp.int32, kx.shape, 1) // MLA_ROPE
    pe0 = H * MLA_NOPE
    sw0 = pe0 + H * MLA_ROPE
    for hd in range(H):
        p, m = divmod(hd, LANES // MLA_ROPE)
        qpe = (q[:, pe0 + p * LANES:pe0 + (p + 1) * LANES] * cos
               + q[:, sw0 + p * LANES:sw0 + (p + 1) * LANES] * sin)
        sel = half == m
        qt_ref[0, hd, 0:LANES, :] = q[:, hd * MLA_NOPE:(hd + 1) * MLA_NOPE].T.astype(BF16)
        qt_ref[0, hd, LANES:2 * LANES, :] = jnp.where(sel, qpe, 0.0).T.astype(BF16)
        k0 = hd * (MLA_NOPE + MLA_V)
        kn_ref[0, hd] = kv[:, k0:k0 + MLA_NOPE].astype(BF16)
        vt_ref[0, hd, 0:MLA_V, :] = kv[:, k0 + MLA_NOPE:k0 + MLA_NOPE + MLA_V].T.astype(BF16)
        vt_ref[0, hd, MLA_V:MLA_V + V_PAD, :] = jnp.ones((V_PAD, kv.shape[0]), BF16)


def _attn_kernel(qt_ref, kn_ref, kx_ref, vt_ref, o_ref, m_scr, acc_scr, *, tq):
    i = pl.program_id(2)
    heads = range(qt_ref.shape[1])
    dv = o_ref.shape[1] // qt_ref.shape[1]
    qt = [qt_ref[0, h] for h in heads]
    for h in heads:
        m_scr[h] = jnp.full(m_scr.shape[1:], -jnp.inf, F32)
        acc_scr[h] = jnp.zeros(acc_scr.shape[1:], F32)

    def keys(h, start, n):
        return jnp.concatenate([kn_ref[0, h, pl.ds(start, n), :], kx_ref[0, pl.ds(start, n), :]],
                               axis=1)

    def update(h, m_new, pv):
        m_old = m_scr[h, 0:1, :]
        m_scr[h, 0:1, :] = m_new
        acc_scr[h] = jnp.exp2(m_old - m_new) * acc_scr[h] + pv

    def step(j):
        start = pl.multiple_of(j * tq, tq)
        s = [jnp.dot(keys(h, start, tq), qt[h], preferred_element_type=F32)
             for h in heads]
        m_new = [jnp.maximum(m_scr[h, 0:1, :], jnp.max(s[h], axis=0, keepdims=True)) for h in heads]
        p = [jnp.exp2(s[h] - m_new[h]).astype(BF16) for h in heads]
        for h in heads:
            update(h, m_new[h], jnp.dot(vt_ref[0, h, :, pl.ds(start, tq)], p[h],
                                        preferred_element_type=F32))

    def diagonal_step(j):
        half = tq // 2
        start = pl.multiple_of(j * tq, tq)
        mid = pl.multiple_of(j * tq + half, half)
        key = lax.broadcasted_iota(jnp.int32, (half, tq), 0)
        qry = lax.broadcasted_iota(jnp.int32, (half, tq), 1)
        keep_top = key <= qry
        keep_bot = keep_top[:, :half]
        s_top = [jnp.where(keep_top, jnp.dot(keys(h, start, half), qt[h],
                                             preferred_element_type=F32), -jnp.inf) for h in heads]
        s_bot = [jnp.where(keep_bot, jnp.dot(keys(h, mid, half), qt[h][:, half:],
                                             preferred_element_type=F32), -jnp.inf) for h in heads]
        m_top = [jnp.max(x, axis=0, keepdims=True) for x in s_top]
        m_bot = [jnp.max(x, axis=0, keepdims=True) for x in s_bot]
        m_new = [jnp.maximum(m_scr[h, 0:1, :], jnp.concatenate(
            [m_top[h][:, :half], jnp.maximum(m_top[h][:, half:], m_bot[h])], axis=1)) for h in heads]
        p_top = [jnp.exp2(s_top[h] - m_new[h]).astype(BF16) for h in heads]
        p_bot = [jnp.exp2(s_bot[h] - m_new[h][:, half:]).astype(BF16) for h in heads]
        for h in heads:
            pv_top = jnp.dot(vt_ref[0, h, :, pl.ds(start, half)], p_top[h], preferred_element_type=F32)
            pv_bot = jnp.dot(vt_ref[0, h, :, pl.ds(mid, half)], p_bot[h], preferred_element_type=F32)
            update(h, m_new[h], pv_top + jnp.concatenate([jnp.zeros_like(pv_bot), pv_bot], axis=1))

    def pair(jj, c):
        step(2 * jj)
        step(2 * jj + 1)
        return c

    lax.fori_loop(0, i // 2, pair, 0)

    @pl.when(i % 2 == 1)
    def _():
        step(i - 1)

    diagonal_step(i)
    for h in heads:
        acc = acc_scr[h]
        o_ref[:, h * dv:(h + 1) * dv] = (acc[:dv] / acc[dv:dv + 1]).T.astype(o_ref.dtype)


def _mla(x, positions, mix_g, w_a, q_norm, w_qb, kv_norm, w_kvb, w_o, B, S):
    T, D = x.shape
    H = MLA_HEADS
    kpe = w_a[:, MLA_Q_RANK + MLA_KV_RANK:]
    kpe_sw = jnp.concatenate([kpe[:, MLA_ROPE // 2:], kpe[:, :MLA_ROPE // 2]], axis=1)
    reps = LANES // MLA_ROPE
    wa2 = jnp.concatenate([w_a[:, :MLA_Q_RANK + MLA_KV_RANK], jnp.tile(kpe, (1, reps)),
                           jnp.tile(kpe_sw, (1, reps))], axis=1).astype(BF16)
    wq3 = w_qb.reshape(MLA_Q_RANK, H, MLA_NOPE + MLA_ROPE)
    q_nope = wq3[:, :, :MLA_NOPE].reshape(MLA_Q_RANK, H * MLA_NOPE)
    q_pe = wq3[:, :, MLA_NOPE:]
    q_pe_sw = jnp.concatenate([q_pe[:, :, MLA_ROPE // 2:], q_pe[:, :, :MLA_ROPE // 2]], axis=2)
    wqb2 = jnp.concatenate([q_nope, q_pe.reshape(MLA_Q_RANK, H * MLA_ROPE),
                            q_pe_sw.reshape(MLA_Q_RANK, H * MLA_ROPE)], axis=1).astype(BF16)
    inv_freq = 1.0 / (ROPE_THETA ** (jnp.arange(0, MLA_ROPE, 2, dtype=F32) / MLA_ROPE))
    invf = jnp.tile(inv_freq, LANES // (MLA_ROPE // 2)).reshape(1, LANES)
    first_half = (jnp.arange(LANES) % MLA_ROPE) < MLA_ROPE // 2
    sgn = jnp.where(first_half, -1.0, 1.0).astype(F32).reshape(1, LANES)
    q_scale = (MLA_NOPE + MLA_ROPE) ** -0.5 * math.log2(math.e)

    tm = min(TM_PROJ, S)
    nt = S // tm
    wa_cols = wa2.shape[1]
    wq_cols = wqb2.shape[1]
    kv_cols = w_kvb.shape[1]
    qk_dim = 2 * LANES
    q, kn, kx, v = pl.pallas_call(
        functools.partial(_mla_proj_kernel, q_scale=q_scale),
        grid=(B, nt),
        in_specs=[pl.BlockSpec((tm, D), lambda b, i: (b * nt + i, 0)),
                  _resident((1, D)),
                  pl.BlockSpec((tm, 1), lambda b, i: (b * nt + i, 0)),
                  _resident((1, LANES)), _resident((1, LANES)),
                  _resident((D, wa_cols)), _resident((1, MLA_Q_RANK)),
                  _resident((MLA_Q_RANK, wq_cols)), _resident((1, MLA_KV_RANK)),
                  _resident((MLA_KV_RANK, kv_cols))],
        out_specs=[pl.BlockSpec((1, H, qk_dim, tm), lambda b, i: (b, 0, 0, i)),
                   pl.BlockSpec((1, H, tm, MLA_NOPE), lambda b, i: (b, 0, i, 0)),
                   pl.BlockSpec((1, tm, LANES), lambda b, i: (b, i, 0)),
                   pl.BlockSpec((1, H, MLA_V + V_PAD, tm), lambda b, i: (b, 0, 0, i))],
        out_shape=[jax.ShapeDtypeStruct((B, H, qk_dim, S), BF16),
                   jax.ShapeDtypeStruct((B, H, S, MLA_NOPE), BF16),
                   jax.ShapeDtypeStruct((B, S, LANES), BF16),
                   jax.ShapeDtypeStruct((B, H, MLA_V + V_PAD, S), BF16)],
        compiler_params=_params(2),
        name="mla_proj",
    )(x, mix_g.reshape(1, D), positions.reshape(T, 1), invf, sgn, wa2, q_norm.reshape(1, -1),
      wqb2, kv_norm.reshape(1, -1), w_kvb.astype(BF16))

    tq = min(TQ_ATTN, S)
    nq = S // tq
    hp = ATTN_HEADS_PER_STEP
    o = pl.pallas_call(
        functools.partial(_attn_kernel, tq=tq),
        grid=(B, H // hp, nq),
        in_specs=[pl.BlockSpec((1, hp, qk_dim, tq), lambda b, h, i: (b, h, 0, i)),
                  pl.BlockSpec((1, hp, S, MLA_NOPE), lambda b, h, i: (b, h, 0, 0),
                               pipeline_mode=pl.Buffered(1)),
                  pl.BlockSpec((1, S, LANES), lambda b, h, i: (b, 0, 0),
                               pipeline_mode=pl.Buffered(1)),
                  pl.BlockSpec((1, hp, MLA_V + V_PAD, S), lambda b, h, i: (b, h, 0, 0),
                               pipeline_mode=pl.Buffered(1))],
        out_specs=pl.BlockSpec((tq, hp * MLA_V), lambda b, h, i: (b * nq + i, h)),
        out_shape=jax.ShapeDtypeStruct((T, H * MLA_V), BF16),
        scratch_shapes=[pltpu.VMEM((hp, 8, tq), F32), pltpu.VMEM((hp, MLA_V + V_PAD, tq), F32)],
        compiler_params=_params(3),
        name="mla_attn",
    )(q, kn, kx, v)
    return o, w_o.astype(BF16)


def _ml_proj_kernel(x_ref, g_ref, w_ref, wg_ref, bif_ref, cw_ref, cb_ref,
                    qt_ref, k_ref, vt_ref, o_ref, gates_ref, pbuf, *, tiles_per_seq):
    tm = x_ref.shape[0]
    QK = ML_HEADS * ML_QK
    V = ML_HEADS * ML_V

    @pl.when(pl.program_id(0) % tiles_per_seq == 0)
    def _():
        pbuf[...] = jnp.zeros(pbuf.shape, F32)

    h = _rms(x_ref[...], g_ref[...]).astype(BF16)
    w = w_ref[...]
    p_vo = jnp.dot(h, w[:, 2 * QK:], preferred_element_type=F32)
    gates_ref[...] = jnp.dot(h, wg_ref[...], preferred_element_type=F32) + bif_ref[...]
    vt_ref[...] = p_vo[:, :V].T.astype(BF16)
    o_ref[...] = p_vo[:, V:]
    pqk = jnp.dot(h, w[:, :2 * QK], preferred_element_type=F32)
    tail = pbuf[...]
    acc = cb_ref[...] + cw_ref[ML_CONV - 1:ML_CONV, :] * pqk
    for kk in range(ML_CONV - 1):
        acc = acc + cw_ref[kk:kk + 1, :] * _shift_rows(pqk, tail, ML_CONV - 1 - kk)
    pbuf[...] = pqk[tm - 8:tm, :]
    qk = acc * _sigmoid(acc)
    qt_ref[...] = qk[:, :QK].T.astype(BF16)
    k_ref[...] = (qk[:, QK:] * ML_QK ** -0.5).astype(BF16)


def _ml_chunk_kernel(qt_ref, k_ref, vt_ref, o_ref, g_ref, on_ref, out_ref, c_scr, m_scr):
    L = k_ref.shape[0]
    H = ML_HEADS
    pair = LANES // ML_QK

    @pl.when(pl.program_id(1) == 0)
    def _():
        c_scr[...] = jnp.zeros(c_scr.shape, F32)
        m_scr[...] = jnp.zeros(m_scr.shape, F32)

    G = g_ref[...]
    GT = G.T
    lfG = -_softplus(-G)
    lfGT = -_softplus(-GT)
    row = lax.broadcasted_iota(jnp.int32, (L, L), 0)
    col = lax.broadcasted_iota(jnp.int32, (L, L), 1)
    causal = row <= col
    tril_b = (col <= row).astype(BF16)
    triu_b = causal.astype(BF16)
    fc_cols = sum(jnp.dot(tril_b, piece, preferred_element_type=F32) for piece in _split3(lfG))
    fc_rows = sum(jnp.dot(piece, triu_b, preferred_element_type=F32) for piece in _split3(lfGT))
    m_row = m_scr[...]
    ones_rows = (lax.broadcasted_iota(jnp.int32, (ML_V, L), 0) == 0).astype(BF16)
    row_half = lax.broadcasted_iota(jnp.int32, (LANES, L), 0) // ML_QK
    lane_half = lax.broadcasted_iota(jnp.int32, (L, LANES), 1) // ML_QK
    slane_half = lax.broadcasted_iota(jnp.int32, (1, LANES), 1) // ML_QK
    lane1 = lax.broadcasted_iota(jnp.int32, (1, LANES), 1)
    heads = range(H)
    pairs = range(H // pair)
    pr = [hd // pair for hd in heads]
    mm = [hd % pair for hd in heads]
    fc_row = [fc_rows[H + hd:H + hd + 1, :] for hd in heads]
    i_row = [GT[hd:hd + 1, :] for hd in heads]
    c_col = [G[:, hd:hd + 1] - fc_cols[:, H + hd:H + hd + 1] for hd in heads]
    m_prev = [m_row[:, hd:hd + 1] for hd in heads]
    dmat = [jnp.where(causal, fc_row[hd] + c_col[hd], -jnp.inf) for hd in heads]
    inter = [fc_row[hd] + m_prev[hd] for hd in heads]
    m_t = [jnp.maximum(inter[hd], jnp.max(dmat[hd], axis=0, keepdims=True)) for hd in heads]
    dw = [jnp.exp(dmat[hd] - m_t[hd]) for hd in heads]
    inter_w = [jnp.exp(inter[hd] - m_t[hd]) for hd in heads]
    qtp = [qt_ref[p * LANES:(p + 1) * LANES, :] for p in pairs]
    kp = [k_ref[:, p * LANES:(p + 1) * LANES] for p in pairs]
    qtm = [jnp.where(row_half == mm[hd], qtp[pr[hd]], jnp.zeros_like(qtp[0])) for hd in heads]
    km = [jnp.where(lane_half == mm[hd], kp[pr[hd]], jnp.zeros_like(kp[0])) for hd in heads]
    s = [_dot(kp[pr[hd]], qtm[hd]) * dw[hd] for hd in heads]
    vht = [jnp.concatenate([vt_ref[hd * ML_V:(hd + 1) * ML_V, :], ones_rows], axis=0)
           for hd in heads]
    ct = [c_scr[p] for p in pairs]
    nd = [_dot(vht[hd], s[hd]) + inter_w[hd] * _dot(ct[pr[hd]], qtm[hd]) for hd in heads]
    hid = [nd[hd][:ML_V] / jnp.maximum(jnp.abs(nd[hd][ML_V:ML_V + 1]), jnp.exp(-m_t[hd]))
           for hd in heads]
    hid = [x * lax.rsqrt(jnp.mean(x * x, axis=0, keepdims=True) + RMS_EPS) for x in hid]
    for hd in heads:
        cols = slice(hd * ML_V, (hd + 1) * ML_V)
        out_ref[:, cols] = (hid[hd].T * on_ref[:, cols] * _sigmoid(o_ref[:, cols])).astype(out_ref.dtype)
    m_new = [m_t[hd][:, L - 1:L] for hd in heads]
    fc_last = [fc_row[hd][:, L - 1:L] for hd in heads]
    decay = [jnp.exp(fc_last[hd] + m_prev[hd] - m_new[hd]) for hd in heads]
    w_row = [jnp.exp(fc_last[hd] - fc_row[hd] + i_row[hd] - m_new[hd]) for hd in heads]
    upd = [_dot(w_row[hd] * vht[hd].astype(F32), km[hd]) for hd in heads]
    for p in pairs:
        scale = decay[p * pair]
        for k in range(1, pair):
            scale = jnp.where(slane_half == k, decay[p * pair + k], scale)
        c_scr[p] = scale * ct[p] + sum(upd[p * pair + k] for k in range(pair))
    m_out = m_row
    for hd in heads:
        m_out = jnp.where(lane1 == hd, m_new[hd], m_out)
    m_scr[...] = m_out


def _mlstm(x, mix_g, w_in, b_if, conv_w, conv_b, out_norm, w_o, B, S):
    T, D = x.shape
    H = ML_HEADS
    QK = H * ML_QK
    V = H * ML_V
    main = 2 * QK + 2 * V
    w_main = w_in[:, :main].astype(BF16)
    w_gate = jnp.pad(w_in[:, main:], ((0, 0), (0, LANES - 2 * H))).astype(BF16)
    bif = jnp.pad(b_if, (0, LANES - 2 * H)).reshape(1, LANES)
    tm = min(TM_PROJ, S)
    nt = S // tm
    row_spec = lambda n: pl.BlockSpec((tm, n), lambda i: (i, 0))
    col_spec = lambda n: pl.BlockSpec((n, tm), lambda i: (0, i))
    qt, k, vt, o, gates = pl.pallas_call(
        functools.partial(_ml_proj_kernel, tiles_per_seq=nt),
        grid=(T // tm,),
        in_specs=[row_spec(D), _resident((1, D)), _resident((D, main)), _resident((D, LANES)),
                  _resident((1, LANES)), _resident((ML_CONV, 2 * QK)), _resident((1, 2 * QK))],
        out_specs=[col_spec(QK), row_spec(QK), col_spec(V), row_spec(V), row_spec(LANES)],
        out_shape=[jax.ShapeDtypeStruct((QK, T), BF16), jax.ShapeDtypeStruct((T, QK), BF16),
                   jax.ShapeDtypeStruct((V, T), BF16), jax.ShapeDtypeStruct((T, V), F32),
                   jax.ShapeDtypeStruct((T, LANES), F32)],
        scratch_shapes=[pltpu.VMEM((8, 2 * QK), F32)],
        compiler_params=_params(1),
        name="mlstm_proj",
    )(x, mix_g.reshape(1, D), w_main, w_gate, bif, conv_w, conv_b.reshape(1, -1))

    L = min(ML_CHUNK, S)
    nc = S // L
    chunk_spec = lambda n: pl.BlockSpec((L, n), lambda b, c: (b * nc + c, 0))
    chunk_t_spec = lambda n: pl.BlockSpec((n, L), lambda b, c: (0, b * nc + c))
    hid = pl.pallas_call(
        _ml_chunk_kernel,
        grid=(B, nc),
        in_specs=[chunk_t_spec(QK), chunk_spec(QK), chunk_t_spec(V), chunk_spec(V), chunk_spec(LANES),
                  _resident((1, V))],
        out_specs=chunk_spec(V),
        out_shape=jax.ShapeDtypeStruct((T, V), BF16),
        scratch_shapes=[pltpu.VMEM((H * ML_QK // LANES, 2 * ML_V, LANES), F32),
                        pltpu.VMEM((1, LANES), F32)],
        compiler_params=_params(2),
        name="mlstm_chunk",
    )(qt, k, vt, o, gates, out_norm.reshape(1, V))
    return hid, w_o.astype(BF16)


def _rw_proj_kernel(x_ref, g_ref, mu_ref, wr_ref, wk_ref, wv_ref, w1_ref, w2_ref, a1_ref, a2_ref,
                    g1_ref, g2_ref, w0_ref, a0_ref, kk_ref, ka_ref,
                    r_out, k_out, v_out, lw_out, a_out, kkraw_out, g_out, hbuf, *, tiles_per_seq):
    tm = x_ref.shape[0]
    D = x_ref.shape[1]

    @pl.when(pl.program_id(0) % tiles_per_seq == 0)
    def _():
        hbuf[...] = jnp.zeros(hbuf.shape, F32)

    h = _rms(x_ref[...], g_ref[...])
    xx = _shift_rows(h, hbuf[...], 1) - h
    hbuf[...] = h[tm - 8:tm, :]
    mix = lambda j: (h + xx * mu_ref[j:j + 1, :]).astype(BF16)
    r = jnp.dot(mix(0), wr_ref[...], preferred_element_type=F32)
    k = jnp.dot(mix(2), wk_ref[...], preferred_element_type=F32)
    v = jnp.dot(mix(3), wv_ref[...], preferred_element_type=F32)
    zw = w0_ref[...] + _dot(jnp.tanh(jnp.dot(mix(1), w1_ref[...], preferred_element_type=F32)),
                            w2_ref[...])
    log_decay = -math.exp(-0.5) * _sigmoid(zw)
    a = _sigmoid(a0_ref[...] + _dot(jnp.dot(mix(4), a1_ref[...], preferred_element_type=F32),
                                    a2_ref[...]))
    g = _dot(_sigmoid(jnp.dot(mix(5), g1_ref[...], preferred_element_type=F32)), g2_ref[...])
    r_out[...] = r.astype(r_out.dtype)
    v_out[...] = v.astype(v_out.dtype)
    lw_out[...] = log_decay
    a_out[...] = a.astype(a_out.dtype)
    kkraw_out[...] = (k * kk_ref[...]).astype(kkraw_out.dtype)
    k_out[...] = (k * (1.0 + (a - 1.0) * ka_ref[...])).astype(k_out.dtype)
    g_out[...] = g.astype(g_out.dtype)


def _cumsum_rows(x, row):
    d = 1
    while d < x.shape[0]:
        x = x + jnp.where(row >= d, pltpu.roll(x, d, 0), 0.0)
        d *= 2
    return x


def _group_sums(xs, ones_bd):
    n = xs[0].shape[0]
    pieces = []
    for x in xs:
        hi = x.astype(BF16)
        pieces += [hi, (x - hi.astype(F32)).astype(BF16)]
    both = jnp.dot(jnp.concatenate(pieces, axis=0), ones_bd, preferred_element_type=F32)
    return [both[2 * i * n:(2 * i + 1) * n] + both[(2 * i + 1) * n:(2 * i + 2) * n]
            for i in range(len(xs))]


def _rw_chunk_kernel(r_ref, k_ref, v_ref, lw_ref, a_ref, kk_ref, g_ref, rk_ref, lnw_ref, lnb_ref,
                     out_ref, s_scr):
    nb, L = r_ref.shape[0], RW_HEAD
    n_sub = r_ref.shape[1] // L
    W = s_scr.shape[1]
    N = RW_HEAD
    nh = W // N
    ng = r_ref.shape[2] // W

    @pl.when(pl.program_id(0) == 0)
    def _():
        s_scr[...] = jnp.zeros(s_scr.shape, F32)

    er = lax.broadcasted_iota(jnp.int32, (nh * L, W), 0)
    ec = lax.broadcasted_iota(jnp.int32, (nh * L, W), 1)
    same_head = (er // L) == (ec // N)
    sr = lax.broadcasted_iota(jnp.int32, (W, W), 0)
    sc = lax.broadcasted_iota(jnp.int32, (W, W), 1)
    state_bd = (sr // N) == (sc // N)
    ones_bd = state_bd.astype(BF16)
    lt = lax.broadcasted_iota(jnp.int32, (L, W), 0)
    ls = lax.broadcasted_iota(jnp.int32, (L, W), 1) % L
    strict = ls < lt
    incl = ls <= lt
    eye = (ls == lt).astype(F32)
    inv_n = 1.0 / N
    zero_b = jnp.zeros((nh * L, W), BF16)

    def expand(x):
        return jnp.where(same_head, jnp.concatenate([x.astype(BF16)] * nh, axis=0), zero_b)

    groups = range(nb * ng)
    bat = [g // ng for g in groups]
    cols = [slice((g % ng) * W, (g % ng + 1) * W) for g in groups]

    def chunk(sub, s0):
        rows = slice(sub * L, (sub + 1) * L)
        load = lambda ref: [ref[bat[g], rows, cols[g]].astype(F32) for g in groups]
        r, k, v, lw, kk, a_sig = (load(ref) for ref in (r_ref, k_ref, v_ref, lw_ref, kk_ref, a_ref))
        sums = _group_sums([jnp.concatenate([kk[g] * kk[g], r[g] * k[g] * rk_ref[:, cols[g]]], axis=0)
                            for g in groups], ones_bd)
        kk = [kk[g] / jnp.maximum(jnp.sqrt(sums[g][:L]), 1e-12) for g in groups]
        bonus = [sums[g][L:] * v[g] for g in groups]
        bv = [kk[g] * a_sig[g] for g in groups]
        cum = [_cumsum_rows(x, lt) for x in lw]
        cum_last = [x[L - 1:L, :] for x in cum]
        ar = [jnp.concatenate([-jnp.exp(cum[g] - lw[g]) * kk[g], jnp.exp(cum[g]) * r[g]],
                              axis=0).astype(BF16) for g in groups]
        p_inv = [jnp.exp(-x) for x in cum]
        b_e = [expand(p_inv[g] * bv[g]) for g in groups]
        k_e = [expand(p_inv[g] * k[g]) for g in groups]
        v_e = [expand(x) for x in v]
        gb = [_dot_nt(ar[g], b_e[g]) for g in groups]
        gk = [_dot_nt(ar[g], k_e[g]) for g in groups]
        m_ab = [jnp.where(strict, x[:L], 0.0) for x in gb]
        m_rb = [jnp.where(incl, x[L:], 0.0) for x in gb]
        m_akrk = [jnp.concatenate([jnp.where(strict, x[:L], 0.0), jnp.where(incl, x[L:], 0.0)],
                                  axis=0) for x in gk]

        pw = [_dot(x, expand(x)) for x in m_ab]
        tinv = [eye + x for x in m_ab]
        n_sq = int(math.log2(L)) - 1
        for it in range(n_sq):
            pw_e = [expand(x) for x in pw]
            if it + 1 < n_sq:
                both = [_dot(jnp.concatenate([pw[g], tinv[g]], axis=0), pw_e[g]) for g in groups]
                pw = [x[:L] for x in both]
                tinv = [tinv[g] + both[g][L:] for g in groups]
            else:
                tinv = [tinv[g] + _dot(tinv[g], pw_e[g]) for g in groups]

        ars = [_dot_nt(ar[g], s0[g]) for g in groups]
        mv = [_dot(m_akrk[g], v_e[g]) for g in groups]
        u = [_dot(tinv[g], expand(ars[g][:L] + mv[g][:L])) for g in groups]
        y = [ars[g][L:] + mv[g][L:] + _dot(m_rb[g], expand(u[g])) for g in groups]
        s_new = []
        for g in groups:
            p_out = jnp.exp(cum_last[g] - cum[g])
            upd = _dot_tn(jnp.concatenate([u[g], v[g]], axis=0),
                          jnp.concatenate([p_out * bv[g], p_out * k[g]], axis=0))
            s_new.append(s0[g] * jnp.exp(cum_last[g]) + jnp.where(state_bd, upd, 0.0))

        mean = [x * inv_n for x in _group_sums(y, ones_bd)]
        yc = [y[g] - mean[g] for g in groups]
        var = [x * inv_n for x in _group_sums([x * x for x in yc], ones_bd)]
        for g in groups:
            c = cols[g]
            yn = yc[g] * lax.rsqrt(var[g] + GN_EPS) * lnw_ref[:, c] + lnb_ref[:, c]
            out_ref[bat[g], rows, c] = ((yn + bonus[g]) * g_ref[bat[g], rows, c].astype(F32)).astype(out_ref.dtype)
        return s_new

    state = [s_scr[g] for g in groups]
    for sub in range(n_sub):
        state = chunk(sub, state)
    for g in groups:
        s_scr[g] = state[g]


def _rwkv(x, mix_g, mu, w_r, w_k, w_v, w0, w1, w2, a0, a1, a2, g1, g2, k_k, k_a, r_k,
          ln_w, ln_b, w_o, B, S):
    T, D = x.shape
    pad_cols = lambda w: jnp.pad(w, ((0, 0), (0, LANES - w.shape[1]))).astype(BF16)
    pad_rows = lambda w: jnp.pad(w, ((0, LANES - w.shape[0]), (0, 0))).astype(BF16)
    vec = lambda t: t.reshape(1, D)
    mu8 = jnp.pad(mu, ((0, 8 - mu.shape[0]), (0, 0)))
    tm = min(TM_PROJ, S)
    nt = S // tm
    row_spec = pl.BlockSpec((tm, D), lambda i: (i, 0))
    outs = pl.pallas_call(
        functools.partial(_rw_proj_kernel, tiles_per_seq=nt),
        grid=(T // tm,),
        in_specs=[row_spec, _resident((1, D)), _resident((8, D)),
                  _resident((D, D)), _resident((D, D)), _resident((D, D)),
                  _resident((D, LANES)), _resident((LANES, D)),
                  _resident((D, LANES)), _resident((LANES, D)),
                  _resident((D, LANES)), _resident((LANES, D)),
                  _resident((1, D)), _resident((1, D)), _resident((1, D)), _resident((1, D))],
        out_specs=[row_spec] * 7,
        out_shape=[jax.ShapeDtypeStruct((T, D), F32 if i == 3 else BF16) for i in range(7)],
        scratch_shapes=[pltpu.VMEM((8, D), F32)],
        compiler_params=_params(1),
        name="rwkv_proj",
    )(x, vec(mix_g), mu8, w_r.astype(BF16), w_k.astype(BF16), w_v.astype(BF16),
      pad_cols(w1), pad_rows(w2), pad_cols(a1), pad_rows(a2), pad_cols(g1), pad_rows(g2),
      vec(w0), vec(a0), vec(k_k), vec(k_a))
    r, k, v, lw, a, kkraw, g = outs

    L = min(RW_CHUNK * RW_CHUNKS_PER_STEP, S)
    nc = S // L
    W = RW_GROUP * RW_HEAD
    assert RW_CHUNK == RW_HEAD
    chunk_spec = pl.BlockSpec((B, L, D), lambda c: (0, c, 0))
    seq = lambda t: t.reshape(B, S, D)
    yg = pl.pallas_call(
        _rw_chunk_kernel,
        grid=(nc,),
        in_specs=[chunk_spec] * 7 + [_resident((1, D))] * 3,
        out_specs=chunk_spec,
        out_shape=jax.ShapeDtypeStruct((B, S, D), BF16),
        scratch_shapes=[pltpu.VMEM((B * D // W, W, W), F32)],
        compiler_params=_params(1),
        name="rwkv_chunk",
    )(seq(r), seq(k), seq(v), seq(lw), seq(a), seq(kkraw), seq(g), vec(r_k), vec(ln_w), vec(ln_b))
    return yg.reshape(T, D), w_o.astype(BF16)


def kernel(x, positions, ffn_norm, ffn_w_gate, ffn_w_up, ffn_w_down, mix_norm, final_norm, mla_w_a, mla_q_norm, mla_w_qb, mla_kv_norm, mla_w_kvb, mla_w_o, ml_w_in, ml_b_if, ml_conv_w, ml_conv_b, ml_out_norm, ml_w_o, rw_mu, rw_w_r, rw_w_k, rw_w_v, rw_w0, rw_w1, rw_w2, rw_a0, rw_a1, rw_a2, rw_g1, rw_g2, rw_k_k, rw_k_a, rw_r_k, rw_ln_w, rw_ln_b, rw_w_o):
    B, S, D = x.shape
    depth = mix_norm.shape[0]
    n_mixers = 3
    h = x.reshape(B * S, D)

    wg, wu, wd = (w.astype(BF16) for w in (ffn_w_gate, ffn_w_up, ffn_w_down))

    def ffn(h, layer, half, **kw):
        return _ffn(h, ffn_norm[layer, half], wg, wu, wd, (layer, half), **kw)

    for layer in range(depth):
        h = ffn(h, layer, 0)
        kind, j = layer % n_mixers, layer // n_mixers
        if kind == 0:
            mixer = _mla(h, positions, mix_norm[layer], mla_w_a[j], mla_q_norm[j], mla_w_qb[j],
                         mla_kv_norm[j], mla_w_kvb[j], mla_w_o[j], B, S)
        elif kind == 1:
            mixer = _mlstm(h, mix_norm[layer], ml_w_in[j], ml_b_if[j], ml_conv_w[j], ml_conv_b[j],
                           ml_out_norm[j], ml_w_o[j], B, S)
        else:
            mixer = _rwkv(h, mix_norm[layer], rw_mu[j], rw_w_r[j], rw_w_k[j], rw_w_v[j], rw_w0[j],
                          rw_w1[j], rw_w2[j], rw_a0[j], rw_a1[j], rw_a2[j], rw_g1[j], rw_g2[j],
                          rw_k_k[j], rw_k_a[j], rw_r_k[j], rw_ln_w[j], rw_ln_b[j], rw_w_o[j], B, S)
        h = ffn(h, layer, 1, mixer=mixer, final_g=final_norm if layer == depth - 1 else None)
    return h.reshape(B, S, D)
```

```python
import functools
import math

import jax
import jax.numpy as jnp
from jax import lax
from jax.experimental import pallas as pl
from jax.experimental.pallas import tpu as pltpu

F32 = jnp.float32
BF16 = jnp.bfloat16

RMS_EPS = 1e-6
GN_EPS = 64e-5
LANES = 128
MXU_DIM = 256
V7X_VMEM_BYTES = 64 * 1024 * 1024
VMEM_LIMIT = V7X_VMEM_BYTES * 7 // 8

MLA_HEADS = 8
MLA_Q_RANK = 512
MLA_KV_RANK = 256
MLA_NOPE = 128
MLA_ROPE = 64
MLA_V = 128
ROPE_THETA = 10000.0
ML_HEADS = 8
ML_QK = 64
ML_V = 128
ML_CONV = 4
RW_HEAD = 64

TM_FFN = 512
FFN_SUBTILES = 2
TM_PROJ = 512
TQ_ATTN = 512
V_PAD = 16
ATTN_HEADS_PER_STEP = 4
ML_CHUNK = 256
RW_CHUNK = 64
RW_CHUNKS_PER_STEP = 4
RW_GROUP = MXU_DIM // RW_HEAD


def _params(n_axes):
    return pltpu.CompilerParams(dimension_semantics=("arbitrary",) * n_axes,
                                vmem_limit_bytes=VMEM_LIMIT)


def _resident(shape):
    nd = len(shape)
    return pl.BlockSpec(shape, lambda *_: (0,) * nd, pipeline_mode=pl.Buffered(1))


def _rms(x, g, eps=RMS_EPS):
    return x * lax.rsqrt(jnp.mean(x * x, axis=-1, keepdims=True) + eps) * g


def _dot(a, b):
    return jnp.dot(a.astype(BF16), b.astype(BF16), preferred_element_type=F32)


def _dot_nt(a, b):
    return lax.dot_general(a.astype(BF16), b.astype(BF16), (((1,), (1,)), ((), ())),
                           preferred_element_type=F32)


def _dot_tn(a, b):
    return lax.dot_general(a.astype(BF16), b.astype(BF16), (((0,), (0,)), ((), ())),
                           preferred_element_type=F32)


def _split3(x):
    hi = x.astype(BF16)
    r1 = x - hi.astype(F32)
    mid = r1.astype(BF16)
    lo = (r1 - mid.astype(F32)).astype(BF16)
    return hi, mid, lo


def _shift_rows(x, tail, shift):
    rolled = pltpu.roll(x, shift, 0)
    row = lax.broadcasted_iota(jnp.int32, tail.shape, 0)
    head = jnp.where(row < shift, pltpu.roll(tail, shift, 0), rolled[0:8])
    return jnp.concatenate([head, rolled[8:]], axis=0)


def _softplus(y):
    return jnp.maximum(y, 0.0) + jnp.log1p(jnp.exp(-jnp.abs(y)))


def _sigmoid(y):
    return 1.0 / (1.0 + jnp.exp(-y))


def _ffn_kernel(*refs, mixer, final):
    x_ref, g_ref, wg_ref, wu_ref, wd_ref = refs[:5]
    o_ref = refs[-1]
    tm = x_ref.shape[0]
    subs = [slice(k * (tm // FFN_SUBTILES), (k + 1) * (tm // FFN_SUBTILES)) for k in range(FFN_SUBTILES)]
    x = [x_ref[r, :] for r in subs]
    if mixer:
        a_ref, wo_ref = refs[5:7]
        x = [x[k] + jnp.dot(a_ref[r, :], wo_ref[...], preferred_element_type=F32)
             for k, r in enumerate(subs)]
    h = [_rms(t, g_ref[...]).astype(BF16) for t in x]
    gate = [jnp.dot(t, wg_ref[...], preferred_element_type=F32) for t in h]
    up = [jnp.dot(t, wu_ref[...], preferred_element_type=F32) for t in h]
    act = [(g * _sigmoid(g) * u).astype(BF16) for g, u in zip(gate, up)]
    y = [jnp.dot(t, wd_ref[...], preferred_element_type=F32) for t in act]
    for k, r in enumerate(subs):
        out = x[k] + 0.5 * y[k]
        if final:
            out = _rms(out, refs[-2][...])
        o_ref[r, :] = out


def _ffn(x, g, wg, wu, wd, index, mixer=None, final_g=None):
    T, D = x.shape
    F = wg.shape[-1]
    tm = min(TM_FFN, T)
    row_spec = lambda n: pl.BlockSpec((tm, n), lambda i: (i, 0))
    picked = lambda r, c: pl.BlockSpec((None, None, r, c), lambda i: (*index, 0, 0),
                                       pipeline_mode=pl.Buffered(1))
    args = [x, g.reshape(1, D), wg, wu, wd]
    in_specs = [row_spec(D), _resident((1, D)), picked(D, F), picked(D, F), picked(F, D)]
    if mixer is not None:
        a, w_o = mixer
        args += [a, w_o]
        in_specs += [row_spec(a.shape[1]), _resident(w_o.shape)]
    if final_g is not None:
        args.append(final_g.reshape(1, D))
        in_specs.append(_resident((1, D)))
    return pl.pallas_call(
        functools.partial(_ffn_kernel, mixer=mixer is not None, final=final_g is not None),
        grid=(T // tm,),
        in_specs=in_specs,
        out_specs=row_spec(D),
        out_shape=jax.ShapeDtypeStruct((T, D), F32),
        compiler_params=_params(1),
        name="ffn",
    )(*args)


def _rope_tables(pos_col, invf):
    tm = pos_col.shape[0]
    nfreq = MLA_ROPE // 2
    nblk = LANES // nfreq
    rows = tm // nblk
    posb = jnp.broadcast_to(pos_col.astype(F32), (tm, LANES))
    lane_blk = lax.broadcasted_iota(jnp.int32, (rows, LANES), 1) // nfreq
    packed = posb[0:rows]
    for c in range(1, nblk):
        packed = jnp.where(lane_blk == c, posb[c * rows:(c + 1) * rows], packed)
    ang = packed * invf

    def spread(t):
        rolled = [t] + [pltpu.roll(t, nfreq * k, 1) for k in range(1, nblk)]
        blocks = []
        for c in range(nblk):
            out = rolled[(0 - c) % nblk]
            for dst in range(1, nblk):
                out = jnp.where(lane_blk == dst, rolled[(dst - c) % nblk], out)
            blocks.append(out)
        return jnp.concatenate(blocks, axis=0)

    return spread(jnp.cos(ang)), spread(jnp.sin(ang))


def _mla_proj_kernel(x_ref, g_ref, pos_ref, invf_ref, sgn_ref, wa_ref, qn_ref, wqb_ref,
                     kvn_ref, wkvb_ref, qt_ref, kn_ref, kx_ref, vt_ref, *, q_scale):
    H = MLA_HEADS
    h = _rms(x_ref[...], g_ref[...]).astype(BF16)
    lat = jnp.dot(h, wa_ref[...], preferred_element_type=F32)
    cq = _rms(lat[:, :MLA_Q_RANK], qn_ref[...]).astype(BF16)
    ckv = _rms(lat[:, MLA_Q_RANK:MLA_Q_RANK + MLA_KV_RANK], kvn_ref[...]).astype(BF16)
    q = jnp.dot(cq, wqb_ref[...], preferred_element_type=F32) * q_scale
    kv = jnp.dot(ckv, wkvb_ref[...], preferred_element_type=F32)
    cos, sin = _rope_tables(pos_ref[...], invf_ref[...])
    sin = sin * sgn_ref[...]
    kpe0 = MLA_Q_RANK + MLA_KV_RANK
    kx = lat[:, kpe0:kpe0 + LANES] * cos + lat[:, kpe0 + LANES:kpe0 + 2 * LANES] * sin
    kx_ref[0] = kx.astype(BF16)
    half = lax.broadcasted_iota(jnp.int32, kx.shape, 1) // MLA_ROPE
    pe0 = H * MLA_NOPE
    sw0 = pe0 + H * MLA_ROPE
    for hd in range(H):
        p, m = divmod(hd, LANES // MLA_ROPE)
        qpe = (q[:, pe0 + p * LANES:pe0 + (p + 1) * LANES] * cos
               + q[:, sw0 + p * LANES:sw0 + (p + 1) * LANES] * sin)
        sel = half == m
        qt_ref[0, hd, 0:LANES, :] = q[:, hd * MLA_NOPE:(hd + 1) * MLA_NOPE].T.astype(BF16)
        qt_ref[0, hd, LANES:2 * LANES, :] = jnp.where(sel, qpe, 0.0).T.astype(BF16)
        k0 = hd * (MLA_NOPE + MLA_V)
        kn_ref[0, hd] = kv[:, k0:k0 + MLA_NOPE].astype(BF16)
        vt_ref[0, hd, 0:MLA_V, :] = kv[:, k0 + MLA_NOPE:k0 + MLA_NOPE + MLA_V].T.astype(BF16)
        vt_ref[0, hd, MLA_V:MLA_V + V_PAD, :] = jnp.ones((V_PAD, kv.shape[0]), BF16)


def _attn_kernel(*refs, tq, n_cast):
    qt_ref, kn_ref, kx_ref, vt_ref = refs[:4]
    o_ref = refs[4 + n_cast]
    m_scr, acc_scr = refs[-2:]
    for src, dst in zip(refs[4:4 + n_cast], refs[5 + n_cast:5 + 2 * n_cast]):
        dst[...] = src[...].astype(dst.dtype)
    i = pl.program_id(2)
    heads = range(qt_ref.shape[1])
    dv = o_ref.shape[1] // qt_ref.shape[1]
    qt = [qt_ref[0, h] for h in heads]
    for h in heads:
        m_scr[h] = jnp.full(m_scr.shape[1:], -jnp.inf, F32)
        acc_scr[h] = jnp.zeros(acc_scr.shape[1:], F32)

    def keys(h, start, n):
        return jnp.concatenate([kn_ref[0, h, pl.ds(start, n), :], kx_ref[0, pl.ds(start, n), :]],
                               axis=1)

    def update(h, m_new, pv):
        m_old = m_scr[h, 0:1, :]
        m_scr[h, 0:1, :] = m_new
        acc_scr[h] = jnp.exp2(m_old - m_new) * acc_scr[h] + pv

    def step(j):
        start = pl.multiple_of(j * tq, tq)
        s = [jnp.dot(keys(h, start, tq), qt[h], preferred_element_type=F32)
             for h in heads]
        m_new = [jnp.maximum(m_scr[h, 0:1, :], jnp.max(s[h], axis=0, keepdims=True)) for h in heads]
        p = [jnp.exp2(s[h] - m_new[h]).astype(BF16) for h in heads]
        for h in heads:
            update(h, m_new[h], jnp.dot(vt_ref[0, h, :, pl.ds(start, tq)], p[h],
                                        preferred_element_type=F32))

    def diagonal_step(j):
        half = tq // 2
        start = pl.multiple_of(j * tq, tq)
        mid = pl.multiple_of(j * tq + half, half)
        key = lax.broadcasted_iota(jnp.int32, (half, tq), 0)
        qry = lax.broadcasted_iota(jnp.int32, (half, tq), 1)
        keep_top = key <= qry
        keep_bot = keep_top[:, :half]
        s_top = [jnp.where(keep_top, jnp.dot(keys(h, start, half), qt[h],
                                             preferred_element_type=F32), -jnp.inf) for h in heads]
        s_bot = [jnp.where(keep_bot, jnp.dot(keys(h, mid, half), qt[h][:, half:],
                                             preferred_element_type=F32), -jnp.inf) for h in heads]
        m_top = [jnp.max(x, axis=0, keepdims=True) for x in s_top]
        m_bot = [jnp.max(x, axis=0, keepdims=True) for x in s_bot]
        m_new = [jnp.maximum(m_scr[h, 0:1, :], jnp.concatenate(
            [m_top[h][:, :half], jnp.maximum(m_top[h][:, half:], m_bot[h])], axis=1)) for h in heads]
        p_top = [jnp.exp2(s_top[h] - m_new[h]).astype(BF16) for h in heads]
        p_bot = [jnp.exp2(s_bot[h] - m_new[h][:, half:]).astype(BF16) for h in heads]
        for h in heads:
            pv_top = jnp.dot(vt_ref[0, h, :, pl.ds(start, half)], p_top[h], preferred_element_type=F32)
            pv_bot = jnp.dot(vt_ref[0, h, :, pl.ds(mid, half)], p_bot[h], preferred_element_type=F32)
            update(h, m_new[h], pv_top + jnp.concatenate([jnp.zeros_like(pv_bot), pv_bot], axis=1))

    def pair(jj, c):
        step(2 * jj)
        step(2 * jj + 1)
        return c

    lax.fori_loop(0, i // 2, pair, 0)

    @pl.when(i % 2 == 1)
    def _():
        step(i - 1)

    diagonal_step(i)
    for h in heads:
        acc = acc_scr[h]
        o_ref[:, h * dv:(h + 1) * dv] = (acc[:dv] / acc[dv:dv + 1]).T.astype(o_ref.dtype)


def _mla(x, positions, mix_g, w_a, q_norm, w_qb, kv_norm, w_kvb, w_o, B, S, cast=()):
    T, D = x.shape
    H = MLA_HEADS
    kpe = w_a[:, MLA_Q_RANK + MLA_KV_RANK:]
    kpe_sw = jnp.concatenate([kpe[:, MLA_ROPE // 2:], kpe[:, :MLA_ROPE // 2]], axis=1)
    reps = LANES // MLA_ROPE
    wa2 = jnp.concatenate([w_a[:, :MLA_Q_RANK + MLA_KV_RANK], jnp.tile(kpe, (1, reps)),
                           jnp.tile(kpe_sw, (1, reps))], axis=1).astype(BF16)
    wq3 = w_qb.reshape(MLA_Q_RANK, H, MLA_NOPE + MLA_ROPE)
    q_nope = wq3[:, :, :MLA_NOPE].reshape(MLA_Q_RANK, H * MLA_NOPE)
    q_pe = wq3[:, :, MLA_NOPE:]
    q_pe_sw = jnp.concatenate([q_pe[:, :, MLA_ROPE // 2:], q_pe[:, :, :MLA_ROPE // 2]], axis=2)
    wqb2 = jnp.concatenate([q_nope, q_pe.reshape(MLA_Q_RANK, H * MLA_ROPE),
                            q_pe_sw.reshape(MLA_Q_RANK, H * MLA_ROPE)], axis=1).astype(BF16)
    inv_freq = 1.0 / (ROPE_THETA ** (jnp.arange(0, MLA_ROPE, 2, dtype=F32) / MLA_ROPE))
    invf = jnp.tile(inv_freq, LANES // (MLA_ROPE // 2)).reshape(1, LANES)
    first_half = (jnp.arange(LANES) % MLA_ROPE) < MLA_ROPE // 2
    sgn = jnp.where(first_half, -1.0, 1.0).astype(F32).reshape(1, LANES)
    q_scale = (MLA_NOPE + MLA_ROPE) ** -0.5 * math.log2(math.e)

    tm = min(TM_PROJ, S)
    nt = S // tm
    wa_cols = wa2.shape[1]
    wq_cols = wqb2.shape[1]
    kv_cols = w_kvb.shape[1]
    qk_dim = 2 * LANES
    q, kn, kx, v = pl.pallas_call(
        functools.partial(_mla_proj_kernel, q_scale=q_scale),
        grid=(B, nt),
        in_specs=[pl.BlockSpec((tm, D), lambda b, i: (b * nt + i, 0)),
                  _resident((1, D)),
                  pl.BlockSpec((tm, 1), lambda b, i: (b * nt + i, 0)),
                  _resident((1, LANES)), _resident((1, LANES)),
                  _resident((D, wa_cols)), _resident((1, MLA_Q_RANK)),
                  _resident((MLA_Q_RANK, wq_cols)), _resident((1, MLA_KV_RANK)),
                  _resident((MLA_KV_RANK, kv_cols))],
        out_specs=[pl.BlockSpec((1, H, qk_dim, tm), lambda b, i: (b, 0, 0, i)),
                   pl.BlockSpec((1, H, tm, MLA_NOPE), lambda b, i: (b, 0, i, 0)),
                   pl.BlockSpec((1, tm, LANES), lambda b, i: (b, i, 0)),
                   pl.BlockSpec((1, H, MLA_V + V_PAD, tm), lambda b, i: (b, 0, 0, i))],
        out_shape=[jax.ShapeDtypeStruct((B, H, qk_dim, S), BF16),
                   jax.ShapeDtypeStruct((B, H, S, MLA_NOPE), BF16),
                   jax.ShapeDtypeStruct((B, S, LANES), BF16),
                   jax.ShapeDtypeStruct((B, H, MLA_V + V_PAD, S), BF16)],
        compiler_params=_params(2),
        name="mla_proj",
    )(x, mix_g.reshape(1, D), positions.reshape(T, 1), invf, sgn, wa2, q_norm.reshape(1, -1),
      wqb2, kv_norm.reshape(1, -1), w_kvb.astype(BF16))

    tq = min(TQ_ATTN, S)
    nq = S // tq
    hp = ATTN_HEADS_PER_STEP
    steps = B * (H // hp) * nq
    slab = lambda b, h, i: ((b * (H // hp) + h) * nq + i, 0)
    cast_specs = [pl.BlockSpec((w.shape[0] // steps, w.shape[1]), slab) for w in cast]
    assert all(w.shape[0] % (16 * steps) == 0 for w in cast)
    o, *cast_out = pl.pallas_call(
        functools.partial(_attn_kernel, tq=tq, n_cast=len(cast)),
        grid=(B, H // hp, nq),
        in_specs=[pl.BlockSpec((1, hp, qk_dim, tq), lambda b, h, i: (b, h, 0, i)),
                  pl.BlockSpec((1, hp, S, MLA_NOPE), lambda b, h, i: (b, h, 0, 0),
                               pipeline_mode=pl.Buffered(1)),
                  pl.BlockSpec((1, S, LANES), lambda b, h, i: (b, 0, 0),
                               pipeline_mode=pl.Buffered(1)),
                  pl.BlockSpec((1, hp, MLA_V + V_PAD, S), lambda b, h, i: (b, h, 0, 0),
                               pipeline_mode=pl.Buffered(1))] + cast_specs,
        out_specs=[pl.BlockSpec((tq, hp * MLA_V), lambda b, h, i: (b * nq + i, h))] + cast_specs,
        out_shape=[jax.ShapeDtypeStruct((T, H * MLA_V), BF16)]
        + [jax.ShapeDtypeStruct(w.shape, BF16) for w in cast],
        scratch_shapes=[pltpu.VMEM((hp, 8, tq), F32), pltpu.VMEM((hp, MLA_V + V_PAD, tq), F32)],
        compiler_params=_params(3),
        name="mla_attn",
    )(q, kn, kx, v, *cast)
    return (o, w_o.astype(BF16)), cast_out


def _ml_proj_kernel(x_ref, g_ref, w_ref, wg_ref, bif_ref, cw_ref, cb_ref,
                    qt_ref, k_ref, vt_ref, o_ref, gates_ref, pbuf, *, tiles_per_seq):
    tm = x_ref.shape[0]
    QK = ML_HEADS * ML_QK
    V = ML_HEADS * ML_V

    @pl.when(pl.program_id(0) % tiles_per_seq == 0)
    def _():
        pbuf[...] = jnp.zeros(pbuf.shape, F32)

    h = _rms(x_ref[...], g_ref[...]).astype(BF16)
    w = w_ref[...]
    p_vo = jnp.dot(h, w[:, 2 * QK:], preferred_element_type=F32)
    gates_ref[...] = jnp.dot(h, wg_ref[...], preferred_element_type=F32) + bif_ref[...]
    vt_ref[...] = p_vo[:, :V].T.astype(BF16)
    o_ref[...] = p_vo[:, V:]
    pqk = jnp.dot(h, w[:, :2 * QK], preferred_element_type=F32)
    tail = pbuf[...]
    acc = cb_ref[...] + cw_ref[ML_CONV - 1:ML_CONV, :] * pqk
    for kk in range(ML_CONV - 1):
        acc = acc + cw_ref[kk:kk + 1, :] * _shift_rows(pqk, tail, ML_CONV - 1 - kk)
    pbuf[...] = pqk[tm - 8:tm, :]
    qk = acc * _sigmoid(acc)
    qt_ref[...] = qk[:, :QK].T.astype(BF16)
    k_ref[...] = (qk[:, QK:] * ML_QK ** -0.5).astype(BF16)


def _ml_chunk_kernel(qt_ref, k_ref, vt_ref, o_ref, g_ref, on_ref, out_ref, c_scr, m_scr):
    L = k_ref.shape[0]
    H = ML_HEADS
    pair = LANES // ML_QK

    @pl.when(pl.program_id(1) == 0)
    def _():
        c_scr[...] = jnp.zeros(c_scr.shape, F32)
        m_scr[...] = jnp.zeros(m_scr.shape, F32)

    G = g_ref[...]
    GT = G.T
    lfG = -_softplus(-G)
    lfGT = -_softplus(-GT)
    row = lax.broadcasted_iota(jnp.int32, (L, L), 0)
    col = lax.broadcasted_iota(jnp.int32, (L, L), 1)
    causal = row <= col
    tril_b = (col <= row).astype(BF16)
    triu_b = causal.astype(BF16)
    fc_cols = sum(jnp.dot(tril_b, piece, preferred_element_type=F32) for piece in _split3(lfG))
    fc_rows = sum(jnp.dot(piece, triu_b, preferred_element_type=F32) for piece in _split3(lfGT))
    m_row = m_scr[...]
    ones_rows = (lax.broadcasted_iota(jnp.int32, (ML_V, L), 0) == 0).astype(BF16)
    row_half = lax.broadcasted_iota(jnp.int32, (LANES, L), 0) // ML_QK
    lane_half = lax.broadcasted_iota(jnp.int32, (L, LANES), 1) // ML_QK
    slane_half = lax.broadcasted_iota(jnp.int32, (1, LANES), 1) // ML_QK
    lane1 = lax.broadcasted_iota(jnp.int32, (1, LANES), 1)
    heads = range(H)
    pairs = range(H // pair)
    pr = [hd // pair for hd in heads]
    mm = [hd % pair for hd in heads]
    fc_row = [fc_rows[H + hd:H + hd + 1, :] for hd in heads]
    i_row = [GT[hd:hd + 1, :] for hd in heads]
    c_col = [G[:, hd:hd + 1] - fc_cols[:, H + hd:H + hd + 1] for hd in heads]
    m_prev = [m_row[:, hd:hd + 1] for hd in heads]
    dmat = [jnp.where(causal, fc_row[hd] + c_col[hd], -jnp.inf) for hd in heads]
    inter = [fc_row[hd] + m_prev[hd] for hd in heads]
    m_t = [jnp.maximum(inter[hd], jnp.max(dmat[hd], axis=0, keepdims=True)) for hd in heads]
    dw = [jnp.exp(dmat[hd] - m_t[hd]) for hd in heads]
    inter_w = [jnp.exp(inter[hd] - m_t[hd]) for hd in heads]
    qtp = [qt_ref[p * LANES:(p + 1) * LANES, :] for p in pairs]
    kp = [k_ref[:, p * LANES:(p + 1) * LANES] for p in pairs]
    qtm = [jnp.where(row_half == mm[hd], qtp[pr[hd]], jnp.zeros_like(qtp[0])) for hd in heads]
    km = [jnp.where(lane_half == mm[hd], kp[pr[hd]], jnp.zeros_like(kp[0])) for hd in heads]
    s = [_dot(kp[pr[hd]], qtm[hd]) * dw[hd] for hd in heads]
    vht = [jnp.concatenate([vt_ref[hd * ML_V:(hd + 1) * ML_V, :], ones_rows], axis=0)
           for hd in heads]
    ct = [c_scr[p] for p in pairs]
    nd = [_dot(vht[hd], s[hd]) + inter_w[hd] * _dot(ct[pr[hd]], qtm[hd]) for hd in heads]
    hid = [nd[hd][:ML_V] / jnp.maximum(jnp.abs(nd[hd][ML_V:ML_V + 1]), jnp.exp(-m_t[hd]))
           for hd in heads]
    hid = [x * lax.rsqrt(jnp.mean(x * x, axis=0, keepdims=True) + RMS_EPS) for x in hid]
    for hd in heads:
        cols = slice(hd * ML_V, (hd + 1) * ML_V)
        out_ref[:, cols] = (hid[hd].T * on_ref[:, cols] * _sigmoid(o_ref[:, cols])).astype(out_ref.dtype)
    m_new = [m_t[hd][:, L - 1:L] for hd in heads]
    fc_last = [fc_row[hd][:, L - 1:L] for hd in heads]
    decay = [jnp.exp(fc_last[hd] + m_prev[hd] - m_new[hd]) for hd in heads]
    w_row = [jnp.exp(fc_last[hd] - fc_row[hd] + i_row[hd] - m_new[hd]) for hd in heads]
    upd = [_dot(w_row[hd] * vht[hd].astype(F32), km[hd]) for hd in heads]
    for p in pairs:
        scale = decay[p * pair]
        for k in range(1, pair):
            scale = jnp.where(slane_half == k, decay[p * pair + k], scale)
        c_scr[p] = scale * ct[p] + sum(upd[p * pair + k] for k in range(pair))
    m_out = m_row
    for hd in heads:
        m_out = jnp.where(lane1 == hd, m_new[hd], m_out)
    m_scr[...] = m_out


def _mlstm(x, mix_g, w_in, b_if, conv_w, conv_b, out_norm, w_o, B, S):
    T, D = x.shape
    H = ML_HEADS
    QK = H * ML_QK
    V = H * ML_V
    main = 2 * QK + 2 * V
    w_main = w_in[:, :main].astype(BF16)
    w_gate = jnp.pad(w_in[:, main:], ((0, 0), (0, LANES - 2 * H))).astype(BF16)
    bif = jnp.pad(b_if, (0, LANES - 2 * H)).reshape(1, LANES)
    tm = min(TM_PROJ, S)
    nt = S // tm
    row_spec = lambda n: pl.BlockSpec((tm, n), lambda i: (i, 0))
    col_spec = lambda n: pl.BlockSpec((n, tm), lambda i: (0, i))
    qt, k, vt, o, gates = pl.pallas_call(
        functools.partial(_ml_proj_kernel, tiles_per_seq=nt),
        grid=(T // tm,),
        in_specs=[row_spec(D), _resident((1, D)), _resident((D, main)), _resident((D, LANES)),
                  _resident((1, LANES)), _resident((ML_CONV, 2 * QK)), _resident((1, 2 * QK))],
        out_specs=[col_spec(QK), row_spec(QK), col_spec(V), row_spec(V), row_spec(LANES)],
        out_shape=[jax.ShapeDtypeStruct((QK, T), BF16), jax.ShapeDtypeStruct((T, QK), BF16),
                   jax.ShapeDtypeStruct((V, T), BF16), jax.ShapeDtypeStruct((T, V), F32),
                   jax.ShapeDtypeStruct((T, LANES), F32)],
        scratch_shapes=[pltpu.VMEM((8, 2 * QK), F32)],
        compiler_params=_params(1),
        name="mlstm_proj",
    )(x, mix_g.reshape(1, D), w_main, w_gate, bif, conv_w, conv_b.reshape(1, -1))

    L = min(ML_CHUNK, S)
    nc = S // L
    chunk_spec = lambda n: pl.BlockSpec((L, n), lambda b, c: (b * nc + c, 0))
    chunk_t_spec = lambda n: pl.BlockSpec((n, L), lambda b, c: (0, b * nc + c))
    hid = pl.pallas_call(
        _ml_chunk_kernel,
        grid=(B, nc),
        in_specs=[chunk_t_spec(QK), chunk_spec(QK), chunk_t_spec(V), chunk_spec(V), chunk_spec(LANES),
                  _resident((1, V))],
        out_specs=chunk_spec(V),
        out_shape=jax.ShapeDtypeStruct((T, V), BF16),
        scratch_shapes=[pltpu.VMEM((H * ML_QK // LANES, 2 * ML_V, LANES), F32),
                        pltpu.VMEM((1, LANES), F32)],
        compiler_params=_params(2),
        name="mlstm_chunk",
    )(qt, k, vt, o, gates, out_norm.reshape(1, V))
    return hid, w_o.astype(BF16)


def _rw_proj_kernel(x_ref, g_ref, mu_ref, wr_ref, wk_ref, wv_ref, w1_ref, w2_ref, a1_ref, a2_ref,
                    g1_ref, g2_ref, w0_ref, a0_ref, kk_ref, ka_ref,
                    r_out, k_out, v_out, lw_out, a_out, kkraw_out, g_out, hbuf, *, tiles_per_seq):
    tm = x_ref.shape[0]
    D = x_ref.shape[1]

    @pl.when(pl.program_id(0) % tiles_per_seq == 0)
    def _():
        hbuf[...] = jnp.zeros(hbuf.shape, F32)

    h = _rms(x_ref[...], g_ref[...])
    xx = _shift_rows(h, hbuf[...], 1) - h
    hbuf[...] = h[tm - 8:tm, :]
    mix = lambda j: (h + xx * mu_ref[j:j + 1, :]).astype(BF16)
    r = jnp.dot(mix(0), wr_ref[...], preferred_element_type=F32)
    k = jnp.dot(mix(2), wk_ref[...], preferred_element_type=F32)
    v = jnp.dot(mix(3), wv_ref[...], preferred_element_type=F32)
    zw = w0_ref[...] + _dot(jnp.tanh(jnp.dot(mix(1), w1_ref[...], preferred_element_type=F32)),
                            w2_ref[...])
    log_decay = -math.exp(-0.5) * _sigmoid(zw)
    a = _sigmoid(a0_ref[...] + _dot(jnp.dot(mix(4), a1_ref[...], preferred_element_type=F32),
                                    a2_ref[...]))
    g = _dot(_sigmoid(jnp.dot(mix(5), g1_ref[...], preferred_element_type=F32)), g2_ref[...])
    r_out[...] = r.astype(r_out.dtype)
    v_out[...] = v.astype(v_out.dtype)
    lw_out[...] = log_decay
    a_out[...] = a.astype(a_out.dtype)
    kkraw_out[...] = (k * kk_ref[...]).astype(kkraw_out.dtype)
    k_out[...] = (k * (1.0 + (a - 1.0) * ka_ref[...])).astype(k_out.dtype)
    g_out[...] = g.astype(g_out.dtype)


def _cumsum_rows(x, row):
    d = 1
    while d < x.shape[0]:
        x = x + jnp.where(row >= d, pltpu.roll(x, d, 0), 0.0)
        d *= 2
    return x


def _group_sums(xs, ones_bd):
    n = xs[0].shape[0]
    pieces = []
    for x in xs:
        hi = x.astype(BF16)
        pieces += [hi, (x - hi.astype(F32)).astype(BF16)]
    both = jnp.dot(jnp.concatenate(pieces, axis=0), ones_bd, preferred_element_type=F32)
    return [both[2 * i * n:(2 * i + 1) * n] + both[(2 * i + 1) * n:(2 * i + 2) * n]
            for i in range(len(xs))]


def _rw_chunk_kernel(r_ref, k_ref, v_ref, lw_ref, a_ref, kk_ref, g_ref, rk_ref, lnw_ref, lnb_ref,
                     out_ref, s_scr):
    nb, L = r_ref.shape[0], RW_HEAD
    n_sub = r_ref.shape[1] // L
    W = s_scr.shape[1]
    N = RW_HEAD
    nh = W // N
    ng = r_ref.shape[2] // W

    @pl.when(pl.program_id(0) == 0)
    def _():
        s_scr[...] = jnp.zeros(s_scr.shape, F32)

    er = lax.broadcasted_iota(jnp.int32, (nh * L, W), 0)
    ec = lax.broadcasted_iota(jnp.int32, (nh * L, W), 1)
    same_head = (er // L) == (ec // N)
    sr = lax.broadcasted_iota(jnp.int32, (W, W), 0)
    sc = lax.broadcasted_iota(jnp.int32, (W, W), 1)
    state_bd = (sr // N) == (sc // N)
    ones_bd = state_bd.astype(BF16)
    lt = lax.broadcasted_iota(jnp.int32, (L, W), 0)
    ls = lax.broadcasted_iota(jnp.int32, (L, W), 1) % L
    strict = ls < lt
    incl = ls <= lt
    eye = (ls == lt).astype(F32)
    inv_n = 1.0 / N
    zero_b = jnp.zeros((nh * L, W), BF16)

    def expand(x):
        return jnp.where(same_head, jnp.concatenate([x.astype(BF16)] * nh, axis=0), zero_b)

    groups = range(nb * ng)
    bat = [g // ng for g in groups]
    cols = [slice((g % ng) * W, (g % ng + 1) * W) for g in groups]

    def chunk(sub, s0):
        rows = slice(sub * L, (sub + 1) * L)
        load = lambda ref: [ref[bat[g], rows, cols[g]].astype(F32) for g in groups]
        r, k, v, lw, kk, a_sig = (load(ref) for ref in (r_ref, k_ref, v_ref, lw_ref, kk_ref, a_ref))
        sums = _group_sums([jnp.concatenate([kk[g] * kk[g], r[g] * k[g] * rk_ref[:, cols[g]]], axis=0)
                            for g in groups], ones_bd)
        kk = [kk[g] / jnp.maximum(jnp.sqrt(sums[g][:L]), 1e-12) for g in groups]
        bonus = [sums[g][L:] * v[g] for g in groups]
        bv = [kk[g] * a_sig[g] for g in groups]
        cum = [_cumsum_rows(x, lt) for x in lw]
        cum_last = [x[L - 1:L, :] for x in cum]
        ar = [jnp.concatenate([-jnp.exp(cum[g] - lw[g]) * kk[g], jnp.exp(cum[g]) * r[g]],
                              axis=0).astype(BF16) for g in groups]
        p_inv = [jnp.exp(-x) for x in cum]
        b_e = [expand(p_inv[g] * bv[g]) for g in groups]
        k_e = [expand(p_inv[g] * k[g]) for g in groups]
        v_e = [expand(x) for x in v]
        gb = [_dot_nt(ar[g], b_e[g]) for g in groups]
        gk = [_dot_nt(ar[g], k_e[g]) for g in groups]
        m_ab = [jnp.where(strict, x[:L], 0.0) for x in gb]
        m_rb = [jnp.where(incl, x[L:], 0.0) for x in gb]
        m_akrk = [jnp.concatenate([jnp.where(strict, x[:L], 0.0), jnp.where(incl, x[L:], 0.0)],
                                  axis=0) for x in gk]

        pw = [_dot(x, expand(x)) for x in m_ab]
        tinv = [eye + x for x in m_ab]
        n_sq = int(math.log2(L)) - 1
        for it in range(n_sq):
            pw_e = [expand(x) for x in pw]
            if it + 1 < n_sq:
                both = [_dot(jnp.concatenate([pw[g], tinv[g]], axis=0), pw_e[g]) for g in groups]
                pw = [x[:L] for x in both]
                tinv = [tinv[g] + both[g][L:] for g in groups]
            else:
                tinv = [tinv[g] + _dot(tinv[g], pw_e[g]) for g in groups]

        ars = [_dot_nt(ar[g], s0[g]) for g in groups]
        mv = [_dot(m_akrk[g], v_e[g]) for g in groups]
        u = [_dot(tinv[g], expand(ars[g][:L] + mv[g][:L])) for g in groups]
        y = [ars[g][L:] + mv[g][L:] + _dot(m_rb[g], expand(u[g])) for g in groups]
        s_new = []
        for g in groups:
            p_out = jnp.exp(cum_last[g] - cum[g])
            upd = _dot_tn(jnp.concatenate([u[g], v[g]], axis=0),
                          jnp.concatenate([p_out * bv[g], p_out * k[g]], axis=0))
            s_new.append(s0[g] * jnp.exp(cum_last[g]) + jnp.where(state_bd, upd, 0.0))

        mean = [x * inv_n for x in _group_sums(y, ones_bd)]
        yc = [y[g] - mean[g] for g in groups]
        var = [x * inv_n for x in _group_sums([x * x for x in yc], ones_bd)]
        for g in groups:
            c = cols[g]
            yn = yc[g] * lax.rsqrt(var[g] + GN_EPS) * lnw_ref[:, c] + lnb_ref[:, c]
            out_ref[bat[g], rows, c] = ((yn + bonus[g]) * g_ref[bat[g], rows, c].astype(F32)).astype(out_ref.dtype)
        return s_new

    state = [s_scr[g] for g in groups]
    for sub in range(n_sub):
        state = chunk(sub, state)
    for g in groups:
        s_scr[g] = state[g]


def _rwkv(x, mix_g, mu, w_r, w_k, w_v, w0, w1, w2, a0, a1, a2, g1, g2, k_k, k_a, r_k,
          ln_w, ln_b, w_o, B, S):
    T, D = x.shape
    pad_cols = lambda w: jnp.pad(w, ((0, 0), (0, LANES - w.shape[1]))).astype(BF16)
    pad_rows = lambda w: jnp.pad(w, ((0, LANES - w.shape[0]), (0, 0))).astype(BF16)
    vec = lambda t: t.reshape(1, D)
    mu8 = jnp.pad(mu, ((0, 8 - mu.shape[0]), (0, 0)))
    tm = min(TM_PROJ, S)
    nt = S // tm
    row_spec = pl.BlockSpec((tm, D), lambda i: (i, 0))
    outs = pl.pallas_call(
        functools.partial(_rw_proj_kernel, tiles_per_seq=nt),
        grid=(T // tm,),
        in_specs=[row_spec, _resident((1, D)), _resident((8, D)),
                  _resident((D, D)), _resident((D, D)), _resident((D, D)),
                  _resident((D, LANES)), _resident((LANES, D)),
                  _resident((D, LANES)), _resident((LANES, D)),
                  _resident((D, LANES)), _resident((LANES, D)),
                  _resident((1, D)), _resident((1, D)), _resident((1, D)), _resident((1, D))],
        out_specs=[row_spec] * 7,
        out_shape=[jax.ShapeDtypeStruct((T, D), F32 if i == 3 else BF16) for i in range(7)],
        scratch_shapes=[pltpu.VMEM((8, D), F32)],
        compiler_params=_params(1),
        name="rwkv_proj",
    )(x, vec(mix_g), mu8, w_r.astype(BF16), w_k.astype(BF16), w_v.astype(BF16),
      pad_cols(w1), pad_rows(w2), pad_cols(a1), pad_rows(a2), pad_cols(g1), pad_rows(g2),
      vec(w0), vec(a0), vec(k_k), vec(k_a))
    r, k, v, lw, a, kkraw, g = outs

    L = min(RW_CHUNK * RW_CHUNKS_PER_STEP, S)
    nc = S // L
    W = RW_GROUP * RW_HEAD
    assert RW_CHUNK == RW_HEAD
    chunk_spec = pl.BlockSpec((B, L, D), lambda c: (0, c, 0))
    seq = lambda t: t.reshape(B, S, D)
    yg = pl.pallas_call(
        _rw_chunk_kernel,
        grid=(nc,),
        in_specs=[chunk_spec] * 7 + [_resident((1, D))] * 3,
        out_specs=chunk_spec,
        out_shape=jax.ShapeDtypeStruct((B, S, D), BF16),
        scratch_shapes=[pltpu.VMEM((B * D // W, W, W), F32)],
        compiler_params=_params(1),
        name="rwkv_chunk",
    )(seq(r), seq(k), seq(v), seq(lw), seq(a), seq(kkraw), seq(g), vec(r_k), vec(ln_w), vec(ln_b))
    return yg.reshape(T, D), w_o.astype(BF16)


def kernel(x, positions, ffn_norm, ffn_w_gate, ffn_w_up, ffn_w_down, mix_norm, final_norm, mla_w_a, mla_q_norm, mla_w_qb, mla_kv_norm, mla_w_kvb, mla_w_o, ml_w_in, ml_b_if, ml_conv_w, ml_conv_b, ml_out_norm, ml_w_o, rw_mu, rw_w_r, rw_w_k, rw_w_v, rw_w0, rw_w1, rw_w2, rw_a0, rw_a1, rw_a2, rw_g1, rw_g2, rw_k_k, rw_k_a, rw_r_k, rw_ln_w, rw_ln_b, rw_w_o):
    B, S, D = x.shape
    depth = mix_norm.shape[0]
    n_mixers = 3
    h = x.reshape(B * S, D)

    ffn_w = (ffn_w_gate, ffn_w_up, ffn_w_down)
    first_w = [w[0, 0].astype(BF16)[None, None] for w in ffn_w]
    stacked_w = None

    def ffn(h, layer, half, **kw):
        if stacked_w is None:
            return _ffn(h, ffn_norm[layer, half], *first_w, (0, 0), **kw)
        return _ffn(h, ffn_norm[layer, half], *stacked_w, (layer, half), **kw)

    for layer in range(depth):
        h = ffn(h, layer, 0)
        kind, j = layer % n_mixers, layer // n_mixers
        if kind == 0:
            cast = [w.reshape(-1, w.shape[-1]) for w in ffn_w] if stacked_w is None else []
            mixer, cast_out = _mla(h, positions, mix_norm[layer], mla_w_a[j], mla_q_norm[j],
                                   mla_w_qb[j], mla_kv_norm[j], mla_w_kvb[j], mla_w_o[j], B, S, cast)
            if cast_out:
                stacked_w = [c.reshape(w.shape) for c, w in zip(cast_out, ffn_w)]
        elif kind == 1:
            mixer = _mlstm(h, mix_norm[layer], ml_w_in[j], ml_b_if[j], ml_conv_w[j], ml_conv_b[j],
                           ml_out_norm[j], ml_w_o[j], B, S)
        else:
            mixer = _rwkv(h, mix_norm[layer], rw_mu[j], rw_w_r[j], rw_w_k[j], rw_w_v[j], rw_w0[j],
                          rw_w1[j], rw_w2[j], rw_a0[j], rw_a1[j], rw_a2[j], rw_g1[j], rw_g2[j],
                          rw_k_k[j], rw_k_a[j], rw_r_k[j], rw_ln_w[j], rw_ln_b[j], rw_w_o[j], B, S)
        h = ffn(h, layer, 1, mixer=mixer, final_g=final_norm if layer == depth - 1 else None)
    return h.reshape(B, S, D)
```

```python
import functools
import math

import jax
import jax.numpy as jnp
from jax import lax
from jax.experimental import pallas as pl
from jax.experimental.pallas import tpu as pltpu

F32 = jnp.float32
BF16 = jnp.bfloat16

RMS_EPS = 1e-6
GN_EPS = 64e-5
LANES = 128
MXU_DIM = 256
V7X_VMEM_BYTES = 64 * 1024 * 1024
VMEM_LIMIT = V7X_VMEM_BYTES * 7 // 8

MLA_HEADS = 8
MLA_Q_RANK = 512
MLA_KV_RANK = 256
MLA_NOPE = 128
MLA_ROPE = 64
MLA_V = 128
ROPE_THETA = 10000.0
ML_HEADS = 8
ML_QK = 64
ML_V = 128
ML_CONV = 4
RW_HEAD = 64

TM_FFN = 512
FFN_SUBTILES = 2
TM_PROJ = 512
TQ_ATTN = 512
V_PAD = 16
ATTN_HEADS_PER_STEP = 4
ML_CHUNK = 256
RW_CHUNK = 64
RW_CHUNKS_PER_STEP = 4
RW_GROUP = MXU_DIM // RW_HEAD


def _params(n_axes):
    return pltpu.CompilerParams(dimension_semantics=("arbitrary",) * n_axes,
                                vmem_limit_bytes=VMEM_LIMIT)


def _resident(shape):
    nd = len(shape)
    return pl.BlockSpec(shape, lambda *_: (0,) * nd, pipeline_mode=pl.Buffered(1))


def _rms(x, g, eps=RMS_EPS):
    return x * lax.rsqrt(jnp.mean(x * x, axis=-1, keepdims=True) + eps) * g


def _dot(a, b):
    return jnp.dot(a.astype(BF16), b.astype(BF16), preferred_element_type=F32)


def _dot_nt(a, b):
    return lax.dot_general(a.astype(BF16), b.astype(BF16), (((1,), (1,)), ((), ())),
                           preferred_element_type=F32)


def _dot_tn(a, b):
    return lax.dot_general(a.astype(BF16), b.astype(BF16), (((0,), (0,)), ((), ())),
                           preferred_element_type=F32)


def _split3(x):
    hi = x.astype(BF16)
    r1 = x - hi.astype(F32)
    mid = r1.astype(BF16)
    lo = (r1 - mid.astype(F32)).astype(BF16)
    return hi, mid, lo


def _shift_rows(x, tail, shift):
    rolled = pltpu.roll(x, shift, 0)
    row = lax.broadcasted_iota(jnp.int32, tail.shape, 0)
    head = jnp.where(row < shift, pltpu.roll(tail, shift, 0), rolled[0:8])
    return jnp.concatenate([head, rolled[8:]], axis=0)


def _softplus(y):
    return jnp.maximum(y, 0.0) + jnp.log1p(jnp.exp(-jnp.abs(y)))


def _sigmoid(y):
    return 1.0 / (1.0 + jnp.exp(-y))


def _ffn_kernel(*refs, mixer, final):
    x_ref, g_ref, wg_ref, wu_ref, wd_ref = refs[:5]
    o_ref = refs[-1]
    tm = x_ref.shape[0]
    subs = [slice(k * (tm // FFN_SUBTILES), (k + 1) * (tm // FFN_SUBTILES)) for k in range(FFN_SUBTILES)]
    x = [x_ref[r, :] for r in subs]
    if mixer:
        a_ref, wo_ref = refs[5:7]
        x = [x[k] + jnp.dot(a_ref[r, :], wo_ref[...], preferred_element_type=F32)
             for k, r in enumerate(subs)]
    h = [_rms(t, g_ref[...]).astype(BF16) for t in x]
    gate = [jnp.dot(t, wg_ref[...], preferred_element_type=F32) for t in h]
    up = [jnp.dot(t, wu_ref[...], preferred_element_type=F32) for t in h]
    act = [(g * _sigmoid(g) * u).astype(BF16) for g, u in zip(gate, up)]
    y = [jnp.dot(t, wd_ref[...], preferred_element_type=F32) for t in act]
    for k, r in enumerate(subs):
        out = x[k] + 0.5 * y[k]
        if final:
            out = _rms(out, refs[-2][...])
        o_ref[r, :] = out


def _ffn(x, g, wg, wu, wd, index, mixer=None, final_g=None):
    T, D = x.shape
    F = wg.shape[-1]
    tm = min(TM_FFN, T)
    row_spec = lambda n: pl.BlockSpec((tm, n), lambda i: (i, 0))
    picked = lambda r, c: pl.BlockSpec((None, None, r, c), lambda i: (*index, 0, 0),
                                       pipeline_mode=pl.Buffered(1))
    args = [x, g.reshape(1, D), wg, wu, wd]
    in_specs = [row_spec(D), _resident((1, D)), picked(D, F), picked(D, F), picked(F, D)]
    if mixer is not None:
        a, w_o = mixer
        args += [a, w_o]
        in_specs += [row_spec(a.shape[1]), _resident(w_o.shape)]
    if final_g is not None:
        args.append(final_g.reshape(1, D))
        in_specs.append(_resident((1, D)))
    return pl.pallas_call(
        functools.partial(_ffn_kernel, mixer=mixer is not None, final=final_g is not None),
        grid=(T // tm,),
        in_specs=in_specs,
        out_specs=row_spec(D),
        out_shape=jax.ShapeDtypeStruct((T, D), F32),
        compiler_params=_params(1),
        name="ffn",
    )(*args)


def _cast_kernel(*refs):
    n = len(refs) // 2
    for src, dst in zip(refs[:n], refs[n:]):
        dst[...] = src[...].astype(dst.dtype)


def _cast_first(ws):
    steps = 8
    specs = [pl.BlockSpec((None, None, w.shape[2] // steps, w.shape[3]), lambda i: (0, 0, i, 0))
             for w in ws]
    assert all(w.shape[2] % (16 * steps) == 0 for w in ws)
    return pl.pallas_call(
        _cast_kernel,
        grid=(steps,),
        in_specs=specs,
        out_specs=specs,
        out_shape=[jax.ShapeDtypeStruct((1, 1) + w.shape[2:], BF16) for w in ws],
        compiler_params=_params(1),
        name="cast_first",
    )(*ws)


def _rope_tables(pos_col, invf):
    tm = pos_col.shape[0]
    nfreq = MLA_ROPE // 2
    nblk = LANES // nfreq
    rows = tm // nblk
    posb = jnp.broadcast_to(pos_col.astype(F32), (tm, LANES))
    lane_blk = lax.broadcasted_iota(jnp.int32, (rows, LANES), 1) // nfreq
    packed = posb[0:rows]
    for c in range(1, nblk):
        packed = jnp.where(lane_blk == c, posb[c * rows:(c + 1) * rows], packed)
    ang = packed * invf

    def spread(t):
        rolled = [t] + [pltpu.roll(t, nfreq * k, 1) for k in range(1, nblk)]
        blocks = []
        for c in range(nblk):
            out = rolled[(0 - c) % nblk]
            for dst in range(1, nblk):
                out = jnp.where(lane_blk == dst, rolled[(dst - c) % nblk], out)
            blocks.append(out)
        return jnp.concatenate(blocks, axis=0)

    return spread(jnp.cos(ang)), spread(jnp.sin(ang))


def _mla_proj_kernel(x_ref, g_ref, pos_ref, invf_ref, sgn_ref, wa_ref, qn_ref, wqb_ref,
                     kvn_ref, wkvb_ref, qt_ref, kn_ref, kx_ref, vt_ref, *, q_scale):
    H = MLA_HEADS
    h = _rms(x_ref[...], g_ref[...]).astype(BF16)
    lat = jnp.dot(h, wa_ref[...], preferred_element_type=F32)
    cq = _rms(lat[:, :MLA_Q_RANK], qn_ref[...]).astype(BF16)
    ckv = _rms(lat[:, MLA_Q_RANK:MLA_Q_RANK + MLA_KV_RANK], kvn_ref[...]).astype(BF16)
    q = jnp.dot(cq, wqb_ref[...], preferred_element_type=F32) * q_scale
    kv = jnp.dot(ckv, wkvb_ref[...], preferred_element_type=F32)
    cos, sin = _rope_tables(pos_ref[...], invf_ref[...])
    sin = sin * sgn_ref[...]
    kpe0 = MLA_Q_RANK + MLA_KV_RANK
    kx = lat[:, kpe0:kpe0 + LANES] * cos + lat[:, kpe0 + LANES:kpe0 + 2 * LANES] * sin
    kx_ref[0] = kx.astype(BF16)
    half = lax.broadcasted_iota(jnp.int32, kx.shape, 1) // MLA_ROPE
    pe0 = H * MLA_NOPE
    sw0 = pe0 + H * MLA_ROPE
    for hd in range(H):
        p, m = divmod(hd, LANES // MLA_ROPE)
        qpe = (q[:, pe0 + p * LANES:pe0 + (p + 1) * LANES] * cos
               + q[:, sw0 + p * LANES:sw0 + (p + 1) * LANES] * sin)
        sel = half == m
        qt_ref[0, hd, 0:LANES, :] = q[:, hd * MLA_NOPE:(hd + 1) * MLA_NOPE].T.astype(BF16)
        qt_ref[0, hd, LANES:2 * LANES, :] = jnp.where(sel, qpe, 0.0).T.astype(BF16)
        k0 = hd * (MLA_NOPE + MLA_V)
        kn_ref[0, hd] = kv[:, k0:k0 + MLA_NOPE].astype(BF16)
        vt_ref[0, hd, 0:MLA_V, :] = kv[:, k0 + MLA_NOPE:k0 + MLA_NOPE + MLA_V].T.astype(BF16)
        vt_ref[0, hd, MLA_V:MLA_V + V_PAD, :] = jnp.ones((V_PAD, kv.shape[0]), BF16)


def _attn_kernel(*refs, tq, n_cast):
    qt_ref, kn_ref, kx_ref, vt_ref = refs[:4]
    o_ref = refs[4 + n_cast]
    m_scr, acc_scr = refs[-2:]
    for src, dst in zip(refs[4:4 + n_cast], refs[5 + n_cast:5 + 2 * n_cast]):
        dst[...] = src[...].astype(dst.dtype)
    i = pl.program_id(2)
    heads = range(qt_ref.shape[1])
    dv = o_ref.shape[1] // qt_ref.shape[1]
    qt = [qt_ref[0, h] for h in heads]
    for h in heads:
        m_scr[h] = jnp.full(m_scr.shape[1:], -jnp.inf, F32)
        acc_scr[h] = jnp.zeros(acc_scr.shape[1:], F32)

    def keys(h, start, n):
        return jnp.concatenate([kn_ref[0, h, pl.ds(start, n), :], kx_ref[0, pl.ds(start, n), :]],
                               axis=1)

    def update(h, m_new, pv):
        m_old = m_scr[h, 0:1, :]
        m_scr[h, 0:1, :] = m_new
        acc_scr[h] = jnp.exp2(m_old - m_new) * acc_scr[h] + pv

    def step(j):
        start = pl.multiple_of(j * tq, tq)
        s = [jnp.dot(keys(h, start, tq), qt[h], preferred_element_type=F32)
             for h in heads]
        m_new = [jnp.maximum(m_scr[h, 0:1, :], jnp.max(s[h], axis=0, keepdims=True)) for h in heads]
        p = [jnp.exp2(s[h] - m_new[h]).astype(BF16) for h in heads]
        for h in heads:
            update(h, m_new[h], jnp.dot(vt_ref[0, h, :, pl.ds(start, tq)], p[h],
                                        preferred_element_type=F32))

    def diagonal_step(j):
        half = tq // 2
        start = pl.multiple_of(j * tq, tq)
        mid = pl.multiple_of(j * tq + half, half)
        key = lax.broadcasted_iota(jnp.int32, (half, tq), 0)
        qry = lax.broadcasted_iota(jnp.int32, (half, tq), 1)
        keep_top = key <= qry
        keep_bot = keep_top[:, :half]
        s_top = [jnp.where(keep_top, jnp.dot(keys(h, start, half), qt[h],
                                             preferred_element_type=F32), -jnp.inf) for h in heads]
        s_bot = [jnp.where(keep_bot, jnp.dot(keys(h, mid, half), qt[h][:, half:],
                                             preferred_element_type=F32), -jnp.inf) for h in heads]
        m_top = [jnp.max(x, axis=0, keepdims=True) for x in s_top]
        m_bot = [jnp.max(x, axis=0, keepdims=True) for x in s_bot]
        m_new = [jnp.maximum(m_scr[h, 0:1, :], jnp.concatenate(
            [m_top[h][:, :half], jnp.maximum(m_top[h][:, half:], m_bot[h])], axis=1)) for h in heads]
        p_top = [jnp.exp2(s_top[h] - m_new[h]).astype(BF16) for h in heads]
        p_bot = [jnp.exp2(s_bot[h] - m_new[h][:, half:]).astype(BF16) for h in heads]
        for h in heads:
            pv_top = jnp.dot(vt_ref[0, h, :, pl.ds(start, half)], p_top[h], preferred_element_type=F32)
            pv_bot = jnp.dot(vt_ref[0, h, :, pl.ds(mid, half)], p_bot[h], preferred_element_type=F32)
            update(h, m_new[h], pv_top + jnp.concatenate([jnp.zeros_like(pv_bot), pv_bot], axis=1))

    def pair(jj, c):
        step(2 * jj)
        step(2 * jj + 1)
        return c

    lax.fori_loop(0, i // 2, pair, 0)

    @pl.when(i % 2 == 1)
    def _():
        step(i - 1)

    diagonal_step(i)
    for h in heads:
        acc = acc_scr[h]
        o_ref[:, h * dv:(h + 1) * dv] = (acc[:dv] / acc[dv:dv + 1]).T.astype(o_ref.dtype)


def _mla(x, positions, mix_g, w_a, q_norm, w_qb, kv_norm, w_kvb, w_o, B, S, cast=()):
    T, D = x.shape
    H = MLA_HEADS
    kpe = w_a[:, MLA_Q_RANK + MLA_KV_RANK:]
    kpe_sw = jnp.concatenate([kpe[:, MLA_ROPE // 2:], kpe[:, :MLA_ROPE // 2]], axis=1)
    reps = LANES // MLA_ROPE
    wa2 = jnp.concatenate([w_a[:, :MLA_Q_RANK + MLA_KV_RANK], jnp.tile(kpe, (1, reps)),
                           jnp.tile(kpe_sw, (1, reps))], axis=1).astype(BF16)
    wq3 = w_qb.reshape(MLA_Q_RANK, H, MLA_NOPE + MLA_ROPE)
    q_nope = wq3[:, :, :MLA_NOPE].reshape(MLA_Q_RANK, H * MLA_NOPE)
    q_pe = wq3[:, :, MLA_NOPE:]
    q_pe_sw = jnp.concatenate([q_pe[:, :, MLA_ROPE // 2:], q_pe[:, :, :MLA_ROPE // 2]], axis=2)
    wqb2 = jnp.concatenate([q_nope, q_pe.reshape(MLA_Q_RANK, H * MLA_ROPE),
                            q_pe_sw.reshape(MLA_Q_RANK, H * MLA_ROPE)], axis=1).astype(BF16)
    inv_freq = 1.0 / (ROPE_THETA ** (jnp.arange(0, MLA_ROPE, 2, dtype=F32) / MLA_ROPE))
    invf = jnp.tile(inv_freq, LANES // (MLA_ROPE // 2)).reshape(1, LANES)
    first_half = (jnp.arange(LANES) % MLA_ROPE) < MLA_ROPE // 2
    sgn = jnp.where(first_half, -1.0, 1.0).astype(F32).reshape(1, LANES)
    q_scale = (MLA_NOPE + MLA_ROPE) ** -0.5 * math.log2(math.e)

    tm = min(TM_PROJ, S)
    nt = S // tm
    wa_cols = wa2.shape[1]
    wq_cols = wqb2.shape[1]
    kv_cols = w_kvb.shape[1]
    qk_dim = 2 * LANES
    q, kn, kx, v = pl.pallas_call(
        functools.partial(_mla_proj_kernel, q_scale=q_scale),
        grid=(B, nt),
        in_specs=[pl.BlockSpec((tm, D), lambda b, i: (b * nt + i, 0)),
                  _resident((1, D)),
                  pl.BlockSpec((tm, 1), lambda b, i: (b * nt + i, 0)),
                  _resident((1, LANES)), _resident((1, LANES)),
                  _resident((D, wa_cols)), _resident((1, MLA_Q_RANK)),
                  _resident((MLA_Q_RANK, wq_cols)), _resident((1, MLA_KV_RANK)),
                  _resident((MLA_KV_RANK, kv_cols))],
        out_specs=[pl.BlockSpec((1, H, qk_dim, tm), lambda b, i: (b, 0, 0, i)),
                   pl.BlockSpec((1, H, tm, MLA_NOPE), lambda b, i: (b, 0, i, 0)),
                   pl.BlockSpec((1, tm, LANES), lambda b, i: (b, i, 0)),
                   pl.BlockSpec((1, H, MLA_V + V_PAD, tm), lambda b, i: (b, 0, 0, i))],
        out_shape=[jax.ShapeDtypeStruct((B, H, qk_dim, S), BF16),
                   jax.ShapeDtypeStruct((B, H, S, MLA_NOPE), BF16),
                   jax.ShapeDtypeStruct((B, S, LANES), BF16),
                   jax.ShapeDtypeStruct((B, H, MLA_V + V_PAD, S), BF16)],
        compiler_params=_params(2),
        name="mla_proj",
    )(x, mix_g.reshape(1, D), positions.reshape(T, 1), invf, sgn, wa2, q_norm.reshape(1, -1),
      wqb2, kv_norm.reshape(1, -1), w_kvb.astype(BF16))

    tq = min(TQ_ATTN, S)
    nq = S // tq
    hp = ATTN_HEADS_PER_STEP
    steps = B * (H // hp) * nq
    slab = lambda b, h, i: ((b * (H // hp) + h) * nq + i, 0)
    cast_specs = [pl.BlockSpec((w.shape[0] // steps, w.shape[1]), slab) for w in cast]
    assert all(w.shape[0] % (16 * steps) == 0 for w in cast)
    o, *cast_out = pl.pallas_call(
        functools.partial(_attn_kernel, tq=tq, n_cast=len(cast)),
        grid=(B, H // hp, nq),
        in_specs=[pl.BlockSpec((1, hp, qk_dim, tq), lambda b, h, i: (b, h, 0, i)),
                  pl.BlockSpec((1, hp, S, MLA_NOPE), lambda b, h, i: (b, h, 0, 0),
                               pipeline_mode=pl.Buffered(1)),
                  pl.BlockSpec((1, S, LANES), lambda b, h, i: (b, 0, 0),
                               pipeline_mode=pl.Buffered(1)),
                  pl.BlockSpec((1, hp, MLA_V + V_PAD, S), lambda b, h, i: (b, h, 0, 0),
                               pipeline_mode=pl.Buffered(1))] + cast_specs,
        out_specs=[pl.BlockSpec((tq, hp * MLA_V), lambda b, h, i: (b * nq + i, h))] + cast_specs,
        out_shape=[jax.ShapeDtypeStruct((T, H * MLA_V), BF16)]
        + [jax.ShapeDtypeStruct(w.shape, BF16) for w in cast],
        scratch_shapes=[pltpu.VMEM((hp, 8, tq), F32), pltpu.VMEM((hp, MLA_V + V_PAD, tq), F32)],
        compiler_params=_params(3),
        name="mla_attn",
    )(q, kn, kx, v, *cast)
    return (o, w_o.astype(BF16)), cast_out


def _ml_proj_kernel(x_ref, g_ref, w_ref, wg_ref, bif_ref, cw_ref, cb_ref,
                    qt_ref, k_ref, vt_ref, o_ref, gates_ref, pbuf, *, tiles_per_seq):
    tm = x_ref.shape[0]
    QK = ML_HEADS * ML_QK
    V = ML_HEADS * ML_V

    @pl.when(pl.program_id(0) % tiles_per_seq == 0)
    def _():
        pbuf[...] = jnp.zeros(pbuf.shape, F32)

    h = _rms(x_ref[...], g_ref[...]).astype(BF16)
    w = w_ref[...]
    p_vo = jnp.dot(h, w[:, 2 * QK:], preferred_element_type=F32)
    gates_ref[...] = jnp.dot(h, wg_ref[...], preferred_element_type=F32) + bif_ref[...]
    vt_ref[...] = p_vo[:, :V].T.astype(BF16)
    o_ref[...] = p_vo[:, V:]
    pqk = jnp.dot(h, w[:, :2 * QK], preferred_element_type=F32)
    tail = pbuf[...]
    acc = cb_ref[...] + cw_ref[ML_CONV - 1:ML_CONV, :] * pqk
    for kk in range(ML_CONV - 1):
        acc = acc + cw_ref[kk:kk + 1, :] * _shift_rows(pqk, tail, ML_CONV - 1 - kk)
    pbuf[...] = pqk[tm - 8:tm, :]
    qk = acc * _sigmoid(acc)
    qt_ref[...] = qk[:, :QK].T.astype(BF16)
    k_ref[...] = (qk[:, QK:] * ML_QK ** -0.5).astype(BF16)


def _ml_chunk_kernel(qt_ref, k_ref, vt_ref, o_ref, g_ref, on_ref, out_ref, c_scr, m_scr):
    L = k_ref.shape[0]
    H = ML_HEADS
    pair = LANES // ML_QK

    @pl.when(pl.program_id(1) == 0)
    def _():
        c_scr[...] = jnp.zeros(c_scr.shape, F32)
        m_scr[...] = jnp.zeros(m_scr.shape, F32)

    G = g_ref[...]
    GT = G.T
    lfG = -_softplus(-G)
    lfGT = -_softplus(-GT)
    row = lax.broadcasted_iota(jnp.int32, (L, L), 0)
    col = lax.broadcasted_iota(jnp.int32, (L, L), 1)
    causal = row <= col
    tril_b = (col <= row).astype(BF16)
    triu_b = causal.astype(BF16)
    fc_cols = sum(jnp.dot(tril_b, piece, preferred_element_type=F32) for piece in _split3(lfG))
    fc_rows = sum(jnp.dot(piece, triu_b, preferred_element_type=F32) for piece in _split3(lfGT))
    m_row = m_scr[...]
    ones_rows = (lax.broadcasted_iota(jnp.int32, (ML_V, L), 0) == 0).astype(BF16)
    row_half = lax.broadcasted_iota(jnp.int32, (LANES, L), 0) // ML_QK
    lane_half = lax.broadcasted_iota(jnp.int32, (L, LANES), 1) // ML_QK
    slane_half = lax.broadcasted_iota(jnp.int32, (1, LANES), 1) // ML_QK
    lane1 = lax.broadcasted_iota(jnp.int32, (1, LANES), 1)
    heads = range(H)
    pairs = range(H // pair)
    pr = [hd // pair for hd in heads]
    mm = [hd % pair for hd in heads]
    fc_row = [fc_rows[H + hd:H + hd + 1, :] for hd in heads]
    i_row = [GT[hd:hd + 1, :] for hd in heads]
    c_col = [G[:, hd:hd + 1] - fc_cols[:, H + hd:H + hd + 1] for hd in heads]
    m_prev = [m_row[:, hd:hd + 1] for hd in heads]
    dmat = [jnp.where(causal, fc_row[hd] + c_col[hd], -jnp.inf) for hd in heads]
    inter = [fc_row[hd] + m_prev[hd] for hd in heads]
    m_t = [jnp.maximum(inter[hd], jnp.max(dmat[hd], axis=0, keepdims=True)) for hd in heads]
    dw = [jnp.exp(dmat[hd] - m_t[hd]) for hd in heads]
    inter_w = [jnp.exp(inter[hd] - m_t[hd]) for hd in heads]
    qtp = [qt_ref[p * LANES:(p + 1) * LANES, :] for p in pairs]
    kp = [k_ref[:, p * LANES:(p + 1) * LANES] for p in pairs]
    qtm = [jnp.where(row_half == mm[hd], qtp[pr[hd]], jnp.zeros_like(qtp[0])) for hd in heads]
    km = [jnp.where(lane_half == mm[hd], kp[pr[hd]], jnp.zeros_like(kp[0])) for hd in heads]
    s = [_dot(kp[pr[hd]], qtm[hd]) * dw[hd] for hd in heads]
    vht = [jnp.concatenate([vt_ref[hd * ML_V:(hd + 1) * ML_V, :], ones_rows], axis=0)
           for hd in heads]
    ct = [c_scr[p] for p in pairs]
    nd = [_dot(vht[hd], s[hd]) + inter_w[hd] * _dot(ct[pr[hd]], qtm[hd]) for hd in heads]
    hid = [nd[hd][:ML_V] / jnp.maximum(jnp.abs(nd[hd][ML_V:ML_V + 1]), jnp.exp(-m_t[hd]))
           for hd in heads]
    hid = [x * lax.rsqrt(jnp.mean(x * x, axis=0, keepdims=True) + RMS_EPS) for x in hid]
    for hd in heads:
        cols = slice(hd * ML_V, (hd + 1) * ML_V)
        out_ref[:, cols] = (hid[hd].T * on_ref[:, cols] * _sigmoid(o_ref[:, cols])).astype(out_ref.dtype)
    m_new = [m_t[hd][:, L - 1:L] for hd in heads]
    fc_last = [fc_row[hd][:, L - 1:L] for hd in heads]
    decay = [jnp.exp(fc_last[hd] + m_prev[hd] - m_new[hd]) for hd in heads]
    w_row = [jnp.exp(fc_last[hd] - fc_row[hd] + i_row[hd] - m_new[hd]) for hd in heads]
    upd = [_dot(w_row[hd] * vht[hd].astype(F32), km[hd]) for hd in heads]
    for p in pairs:
        scale = decay[p * pair]
        for k in range(1, pair):
            scale = jnp.where(slane_half == k, decay[p * pair + k], scale)
        c_scr[p] = scale * ct[p] + sum(upd[p * pair + k] for k in range(pair))
    m_out = m_row
    for hd in heads:
        m_out = jnp.where(lane1 == hd, m_new[hd], m_out)
    m_scr[...] = m_out


def _mlstm(x, mix_g, w_in, b_if, conv_w, conv_b, out_norm, w_o, B, S):
    T, D = x.shape
    H = ML_HEADS
    QK = H * ML_QK
    V = H * ML_V
    main = 2 * QK + 2 * V
    w_main = w_in[:, :main].astype(BF16)
    w_gate = jnp.pad(w_in[:, main:], ((0, 0), (0, LANES - 2 * H))).astype(BF16)
    bif = jnp.pad(b_if, (0, LANES - 2 * H)).reshape(1, LANES)
    tm = min(TM_PROJ, S)
    nt = S // tm
    row_spec = lambda n: pl.BlockSpec((tm, n), lambda i: (i, 0))
    col_spec = lambda n: pl.BlockSpec((n, tm), lambda i: (0, i))
    qt, k, vt, o, gates = pl.pallas_call(
        functools.partial(_ml_proj_kernel, tiles_per_seq=nt),
        grid=(T // tm,),
        in_specs=[row_spec(D), _resident((1, D)), _resident((D, main)), _resident((D, LANES)),
                  _resident((1, LANES)), _resident((ML_CONV, 2 * QK)), _resident((1, 2 * QK))],
        out_specs=[col_spec(QK), row_spec(QK), col_spec(V), row_spec(V), row_spec(LANES)],
        out_shape=[jax.ShapeDtypeStruct((QK, T), BF16), jax.ShapeDtypeStruct((T, QK), BF16),
                   jax.ShapeDtypeStruct((V, T), BF16), jax.ShapeDtypeStruct((T, V), F32),
                   jax.ShapeDtypeStruct((T, LANES), F32)],
        scratch_shapes=[pltpu.VMEM((8, 2 * QK), F32)],
        compiler_params=_params(1),
        name="mlstm_proj",
    )(x, mix_g.reshape(1, D), w_main, w_gate, bif, conv_w, conv_b.reshape(1, -1))

    L = min(ML_CHUNK, S)
    nc = S // L
    chunk_spec = lambda n: pl.BlockSpec((L, n), lambda b, c: (b * nc + c, 0))
    chunk_t_spec = lambda n: pl.BlockSpec((n, L), lambda b, c: (0, b * nc + c))
    hid = pl.pallas_call(
        _ml_chunk_kernel,
        grid=(B, nc),
        in_specs=[chunk_t_spec(QK), chunk_spec(QK), chunk_t_spec(V), chunk_spec(V), chunk_spec(LANES),
                  _resident((1, V))],
        out_specs=chunk_spec(V),
        out_shape=jax.ShapeDtypeStruct((T, V), BF16),
        scratch_shapes=[pltpu.VMEM((H * ML_QK // LANES, 2 * ML_V, LANES), F32),
                        pltpu.VMEM((1, LANES), F32)],
        compiler_params=_params(2),
        name="mlstm_chunk",
    )(qt, k, vt, o, gates, out_norm.reshape(1, V))
    return hid, w_o.astype(BF16)


def _rw_proj_kernel(x_ref, g_ref, mu_ref, wr_ref, wk_ref, wv_ref, w1_ref, w2_ref, a1_ref, a2_ref,
                    g1_ref, g2_ref, w0_ref, a0_ref, kk_ref, ka_ref,
                    r_out, k_out, v_out, lw_out, a_out, kkraw_out, g_out, hbuf, *, tiles_per_seq):
    tm = x_ref.shape[0]
    D = x_ref.shape[1]

    @pl.when(pl.program_id(0) % tiles_per_seq == 0)
    def _():
        hbuf[...] = jnp.zeros(hbuf.shape, F32)

    h = _rms(x_ref[...], g_ref[...])
    xx = _shift_rows(h, hbuf[...], 1) - h
    hbuf[...] = h[tm - 8:tm, :]
    mix = lambda j: (h + xx * mu_ref[j:j + 1, :]).astype(BF16)
    r = jnp.dot(mix(0), wr_ref[...], preferred_element_type=F32)
    k = jnp.dot(mix(2), wk_ref[...], preferred_element_type=F32)
    v = jnp.dot(mix(3), wv_ref[...], preferred_element_type=F32)
    zw = w0_ref[...] + _dot(jnp.tanh(jnp.dot(mix(1), w1_ref[...], preferred_element_type=F32)),
                            w2_ref[...])
    log_decay = -math.exp(-0.5) * _sigmoid(zw)
    a = _sigmoid(a0_ref[...] + _dot(jnp.dot(mix(4), a1_ref[...], preferred_element_type=F32),
                                    a2_ref[...]))
    g = _dot(_sigmoid(jnp.dot(mix(5), g1_ref[...], preferred_element_type=F32)), g2_ref[...])
    r_out[...] = r.astype(r_out.dtype)
    v_out[...] = v.astype(v_out.dtype)
    lw_out[...] = log_decay
    a_out[...] = a.astype(a_out.dtype)
    kkraw_out[...] = (k * kk_ref[...]).astype(kkraw_out.dtype)
    k_out[...] = (k * (1.0 + (a - 1.0) * ka_ref[...])).astype(k_out.dtype)
    g_out[...] = g.astype(g_out.dtype)


def _cumsum_rows(x, row):
    d = 1
    while d < x.shape[0]:
        x = x + jnp.where(row >= d, pltpu.roll(x, d, 0), 0.0)
        d *= 2
    return x


def _group_sums(xs, ones_bd):
    n = xs[0].shape[0]
    pieces = []
    for x in xs:
        hi = x.astype(BF16)
        pieces += [hi, (x - hi.astype(F32)).astype(BF16)]
    both = jnp.dot(jnp.concatenate(pieces, axis=0), ones_bd, preferred_element_type=F32)
    return [both[2 * i * n:(2 * i + 1) * n] + both[(2 * i + 1) * n:(2 * i + 2) * n]
            for i in range(len(xs))]


def _rw_chunk_kernel(r_ref, k_ref, v_ref, lw_ref, a_ref, kk_ref, g_ref, rk_ref, lnw_ref, lnb_ref,
                     out_ref, s_scr):
    nb, L = r_ref.shape[0], RW_HEAD
    n_sub = r_ref.shape[1] // L
    W = s_scr.shape[1]
    N = RW_HEAD
    nh = W // N
    ng = r_ref.shape[2] // W

    @pl.when(pl.program_id(0) == 0)
    def _():
        s_scr[...] = jnp.zeros(s_scr.shape, F32)

    er = lax.broadcasted_iota(jnp.int32, (nh * L, W), 0)
    ec = lax.broadcasted_iota(jnp.int32, (nh * L, W), 1)
    same_head = (er // L) == (ec // N)
    sr = lax.broadcasted_iota(jnp.int32, (W, W), 0)
    sc = lax.broadcasted_iota(jnp.int32, (W, W), 1)
    state_bd = (sr // N) == (sc // N)
    ones_bd = state_bd.astype(BF16)
    lt = lax.broadcasted_iota(jnp.int32, (L, W), 0)
    ls = lax.broadcasted_iota(jnp.int32, (L, W), 1) % L
    strict = ls < lt
    incl = ls <= lt
    eye = (ls == lt).astype(F32)
    inv_n = 1.0 / N
    zero_b = jnp.zeros((nh * L, W), BF16)

    def expand(x):
        return jnp.where(same_head, jnp.concatenate([x.astype(BF16)] * nh, axis=0), zero_b)

    groups = range(nb * ng)
    bat = [g // ng for g in groups]
    cols = [slice((g % ng) * W, (g % ng + 1) * W) for g in groups]

    def chunk(sub, s0):
        rows = slice(sub * L, (sub + 1) * L)
        load = lambda ref: [ref[bat[g], rows, cols[g]].astype(F32) for g in groups]
        r, k, v, lw, kk, a_sig = (load(ref) for ref in (r_ref, k_ref, v_ref, lw_ref, kk_ref, a_ref))
        sums = _group_sums([jnp.concatenate([kk[g] * kk[g], r[g] * k[g] * rk_ref[:, cols[g]]], axis=0)
                            for g in groups], ones_bd)
        kk = [kk[g] / jnp.maximum(jnp.sqrt(sums[g][:L]), 1e-12) for g in groups]
        bonus = [sums[g][L:] * v[g] for g in groups]
        bv = [kk[g] * a_sig[g] for g in groups]
        cum = [_cumsum_rows(x, lt) for x in lw]
        cum_last = [x[L - 1:L, :] for x in cum]
        ar = [jnp.concatenate([-jnp.exp(cum[g] - lw[g]) * kk[g], jnp.exp(cum[g]) * r[g]],
                              axis=0).astype(BF16) for g in groups]
        p_inv = [jnp.exp(-x) for x in cum]
        b_e = [expand(p_inv[g] * bv[g]) for g in groups]
        k_e = [expand(p_inv[g] * k[g]) for g in groups]
        v_e = [expand(x) for x in v]
        gb = [_dot_nt(ar[g], b_e[g]) for g in groups]
        gk = [_dot_nt(ar[g], k_e[g]) for g in groups]
        m_ab = [jnp.where(strict, x[:L], 0.0) for x in gb]
        m_rb = [jnp.where(incl, x[L:], 0.0) for x in gb]
        m_akrk = [jnp.concatenate([jnp.where(strict, x[:L], 0.0), jnp.where(incl, x[L:], 0.0)],
                                  axis=0) for x in gk]

        pw = [_dot(x, expand(x)) for x in m_ab]
        tinv = [eye + x for x in m_ab]
        n_sq = int(math.log2(L)) - 1
        for it in range(n_sq):
            pw_e = [expand(x) for x in pw]
            if it + 1 < n_sq:
                both = [_dot(jnp.concatenate([pw[g], tinv[g]], axis=0), pw_e[g]) for g in groups]
                pw = [x[:L] for x in both]
                tinv = [tinv[g] + both[g][L:] for g in groups]
            else:
                tinv = [tinv[g] + _dot(tinv[g], pw_e[g]) for g in groups]

        ars = [_dot_nt(ar[g], s0[g]) for g in groups]
        mv = [_dot(m_akrk[g], v_e[g]) for g in groups]
        u = [_dot(tinv[g], expand(ars[g][:L] + mv[g][:L])) for g in groups]
        y = [ars[g][L:] + mv[g][L:] + _dot(m_rb[g], expand(u[g])) for g in groups]
        s_new = []
        for g in groups:
            p_out = jnp.exp(cum_last[g] - cum[g])
            upd = _dot_tn(jnp.concatenate([u[g], v[g]], axis=0),
                          jnp.concatenate([p_out * bv[g], p_out * k[g]], axis=0))
            s_new.append(s0[g] * jnp.exp(cum_last[g]) + jnp.where(state_bd, upd, 0.0))

        mean = [x * inv_n for x in _group_sums(y, ones_bd)]
        yc = [y[g] - mean[g] for g in groups]
        var = [x * inv_n for x in _group_sums([x * x for x in yc], ones_bd)]
        for g in groups:
            c = cols[g]
            yn = yc[g] * lax.rsqrt(var[g] + GN_EPS) * lnw_ref[:, c] + lnb_ref[:, c]
            out_ref[bat[g], rows, c] = ((yn + bonus[g]) * g_ref[bat[g], rows, c].astype(F32)).astype(out_ref.dtype)
        return s_new

    state = [s_scr[g] for g in groups]
    for sub in range(n_sub):
        state = chunk(sub, state)
    for g in groups:
        s_scr[g] = state[g]


def _rwkv(x, mix_g, mu, w_r, w_k, w_v, w0, w1, w2, a0, a1, a2, g1, g2, k_k, k_a, r_k,
          ln_w, ln_b, w_o, B, S):
    T, D = x.shape
    pad_cols = lambda w: jnp.pad(w, ((0, 0), (0, LANES - w.shape[1]))).astype(BF16)
    pad_rows = lambda w: jnp.pad(w, ((0, LANES - w.shape[0]), (0, 0))).astype(BF16)
    vec = lambda t: t.reshape(1, D)
    mu8 = jnp.pad(mu, ((0, 8 - mu.shape[0]), (0, 0)))
    tm = min(TM_PROJ, S)
    nt = S // tm
    row_spec = pl.BlockSpec((tm, D), lambda i: (i, 0))
    outs = pl.pallas_call(
        functools.partial(_rw_proj_kernel, tiles_per_seq=nt),
        grid=(T // tm,),
        in_specs=[row_spec, _resident((1, D)), _resident((8, D)),
                  _resident((D, D)), _resident((D, D)), _resident((D, D)),
                  _resident((D, LANES)), _resident((LANES, D)),
                  _resident((D, LANES)), _resident((LANES, D)),
                  _resident((D, LANES)), _resident((LANES, D)),
                  _resident((1, D)), _resident((1, D)), _resident((1, D)), _resident((1, D))],
        out_specs=[row_spec] * 7,
        out_shape=[jax.ShapeDtypeStruct((T, D), F32 if i == 3 else BF16) for i in range(7)],
        scratch_shapes=[pltpu.VMEM((8, D), F32)],
        compiler_params=_params(1),
        name="rwkv_proj",
    )(x, vec(mix_g), mu8, w_r.astype(BF16), w_k.astype(BF16), w_v.astype(BF16),
      pad_cols(w1), pad_rows(w2), pad_cols(a1), pad_rows(a2), pad_cols(g1), pad_rows(g2),
      vec(w0), vec(a0), vec(k_k), vec(k_a))
    r, k, v, lw, a, kkraw, g = outs

    L = min(RW_CHUNK * RW_CHUNKS_PER_STEP, S)
    nc = S // L
    W = RW_GROUP * RW_HEAD
    assert RW_CHUNK == RW_HEAD
    chunk_spec = pl.BlockSpec((B, L, D), lambda c: (0, c, 0))
    seq = lambda t: t.reshape(B, S, D)
    yg = pl.pallas_call(
        _rw_chunk_kernel,
        grid=(nc,),
        in_specs=[chunk_spec] * 7 + [_resident((1, D))] * 3,
        out_specs=chunk_spec,
        out_shape=jax.ShapeDtypeStruct((B, S, D), BF16),
        scratch_shapes=[pltpu.VMEM((B * D // W, W, W), F32)],
        compiler_params=_params(1),
        name="rwkv_chunk",
    )(seq(r), seq(k), seq(v), seq(lw), seq(a), seq(kkraw), seq(g), vec(r_k), vec(ln_w), vec(ln_b))
    return yg.reshape(T, D), w_o.astype(BF16)


def kernel(x, positions, ffn_norm, ffn_w_gate, ffn_w_up, ffn_w_down, mix_norm, final_norm, mla_w_a, mla_q_norm, mla_w_qb, mla_kv_norm, mla_w_kvb, mla_w_o, ml_w_in, ml_b_if, ml_conv_w, ml_conv_b, ml_out_norm, ml_w_o, rw_mu, rw_w_r, rw_w_k, rw_w_v, rw_w0, rw_w1, rw_w2, rw_a0, rw_a1, rw_a2, rw_g1, rw_g2, rw_k_k, rw_k_a, rw_r_k, rw_ln_w, rw_ln_b, rw_w_o):
    B, S, D = x.shape
    depth = mix_norm.shape[0]
    n_mixers = 3
    h = x.reshape(B * S, D)

    ffn_w = (ffn_w_gate, ffn_w_up, ffn_w_down)
    first_w = _cast_first(ffn_w)
    stacked_w = None

    def ffn(h, layer, half, **kw):
        if stacked_w is None:
            return _ffn(h, ffn_norm[layer, half], *first_w, (0, 0), **kw)
        return _ffn(h, ffn_norm[layer, half], *stacked_w, (layer, half), **kw)

    for layer in range(depth):
        h = ffn(h, layer, 0)
        kind, j = layer % n_mixers, layer // n_mixers
        if kind == 0:
            cast = [w.reshape(-1, w.shape[-1]) for w in ffn_w] if stacked_w is None else []
            mixer, cast_out = _mla(h, positions, mix_norm[layer], mla_w_a[j], mla_q_norm[j],
                                   mla_w_qb[j], mla_kv_norm[j], mla_w_kvb[j], mla_w_o[j], B, S, cast)
            if cast_out:
                stacked_w = [c.reshape(w.shape) for c, w in zip(cast_out, ffn_w)]
        elif kind == 1:
            mixer = _mlstm(h, mix_norm[layer], ml_w_in[j], ml_b_if[j], ml_conv_w[j], ml_conv_b[j],
                           ml_out_norm[j], ml_w_o[j], B, S)
        else:
            mixer = _rwkv(h, mix_norm[layer], rw_mu[j], rw_w_r[j], rw_w_k[j], rw_w_v[j], rw_w0[j],
                          rw_w1[j], rw_w2[j], rw_a0[j], rw_a1[j], rw_a2[j], rw_g1[j], rw_g2[j],
                          rw_k_k[j], rw_k_a[j], rw_r_k[j], rw_ln_w[j], rw_ln_b[j], rw_w_o[j], B, S)
        h = ffn(h, layer, 1, mixer=mixer, final_g=final_norm if layer == depth - 1 else None)
    return h.reshape(B, S, D)
```
